```python
import jax
import jax.numpy as jnp
from jax import lax

D_MODEL = 4096
BATCH = 2
SEQ = 8192
DEPTH = 1

CHUNK = 64
ATT_BLOCK = 128
DSA_BLOCK = 64
ROPE_THETA = 500000.0
EPS = 1e-6

D_FF = 11008

MLA_HEADS = 12
MLA_Q_RANK = 1536
MLA_KV_RANK = 512
MLA_NOPE = 128
MLA_ROPE = 64
MLA_V = 128

DSA_HEADS = 12
DSA_HEAD_DIM = 128
DSA_ROT = DSA_HEAD_DIM // 4
DSA_MAX_TOPK = 256
IDX_HEADS = 32
IDX_DIM = 64
IDX_ROT = IDX_DIM // 4

MEM_LEN = 256
MEM_HEADS = 4
MEM_HEAD_DIM = 256

N_BRANCH = 3

IN_SIZES = (
    MLA_Q_RANK,
    MLA_KV_RANK,
    MLA_ROPE,
    DSA_HEADS * DSA_HEAD_DIM,
    DSA_HEADS * DSA_HEAD_DIM,
    DSA_HEADS * DSA_HEAD_DIM,
    IDX_HEADS * IDX_DIM,
    IDX_DIM,
    IDX_HEADS,
    MEM_HEADS * MEM_HEAD_DIM,
)
D_IN = sum(IN_SIZES)

kernel_name = 'hybrid_chunk_causal_mla_dsa_memory_macaron'


def _rmsnorm(x, g):
    xf = x.astype(jnp.float32)
    y = xf * lax.rsqrt(jnp.mean(xf * xf, axis=-1, keepdims=True) + EPS)
    return (y * g.astype(jnp.float32)).astype(x.dtype)


def _rope(x, positions, rot_dim):
    half = rot_dim // 2
    inv_freq = ROPE_THETA ** (-jnp.arange(half, dtype=jnp.float32) / half)
    ang = positions.astype(jnp.float32)[:, :, None] * inv_freq
    cos = jnp.cos(ang)[:, :, None, :]
    sin = jnp.sin(ang)[:, :, None, :]
    xr = x[..., :rot_dim].astype(jnp.float32)
    x1, x2 = xr[..., :half], xr[..., half:]
    rot = jnp.concatenate([x1 * cos - x2 * sin, x2 * cos + x1 * sin], axis=-1).astype(x.dtype)
    return jnp.concatenate([rot, x[..., rot_dim:]], axis=-1)


def _split_cols(y, sizes):
    out, start = [], 0
    for n in sizes:
        out.append(y[..., start:start + n])
        start += n
    return out


def _swiglu(h, w_gate, w_up, w_down):
    return (jax.nn.silu(h @ w_gate) * (h @ w_up)) @ w_down


def _chunk_causal_attention(q, k, v):
    B, S, H, Dk = q.shape
    scale = Dk ** -0.5
    nblk = S // ATT_BLOCK
    key_chunk = jnp.arange(S) // CHUNK
    q_blocks = jnp.moveaxis(q.reshape(B, nblk, ATT_BLOCK, H, Dk), 1, 0)

    def one_block(args):
        q_blk, bi = args
        q_chunk = (bi * ATT_BLOCK + jnp.arange(ATT_BLOCK)) // CHUNK
        visible = key_chunk[None, :] <= q_chunk[:, None]
        s = jnp.einsum('bqhd,bshd->bhqs', q_blk, k, preferred_element_type=jnp.float32) * scale
        s = jnp.where(visible[None, None], s, -jnp.inf)
        p = jax.nn.softmax(s, axis=-1).astype(v.dtype)
        return jnp.einsum('bhqs,bshd->bqhd', p, v)

    o = lax.map(one_block, (q_blocks, jnp.arange(nblk)))
    return jnp.moveaxis(o, 0, 1).reshape(B, S, H * v.shape[-1])


def _mla_branch(c_q, c_kv, k_r, positions, g_cq, w_uq, g_ckv, w_ukv, g_q, g_k):
    B, S, _ = c_q.shape
    q = (_rmsnorm(c_q, g_cq) @ w_uq).reshape(B, S, MLA_HEADS, MLA_ROPE + MLA_NOPE)
    kv = (_rmsnorm(c_kv, g_ckv) @ w_ukv).reshape(B, S, MLA_HEADS, MLA_NOPE + MLA_V)
    k_nope, v = kv[..., :MLA_NOPE], kv[..., MLA_NOPE:]
    k_rope = jnp.broadcast_to(k_r[:, :, None, :], (B, S, MLA_HEADS, MLA_ROPE))
    k = jnp.concatenate([k_rope, k_nope], axis=-1)
    q = _rope(_rmsnorm(q, g_q), positions, MLA_ROPE)
    k = _rope(_rmsnorm(k, g_k), positions, MLA_ROPE)
    return _chunk_causal_attention(q, k, v)


def _dsa_branch(q_d, k_d, v_d, q_i, k_i, w_i, positions, g_q, g_k):
    B, S, _ = q_d.shape
    H, Dh = DSA_HEADS, DSA_HEAD_DIM
    topk = min(DSA_MAX_TOPK, S // 4)
    q = _rope(_rmsnorm(q_d.reshape(B, S, H, Dh), g_q), positions, DSA_ROT)
    k = _rope(_rmsnorm(k_d.reshape(B, S, H, Dh), g_k), positions, DSA_ROT)
    k_flat = k.reshape(B, S, H * Dh)
    v_flat = v_d
    qi = _rope(q_i.reshape(B, S, IDX_HEADS, IDX_DIM), positions, IDX_ROT)
    ki = _rope(k_i[:, :, None, :], positions, IDX_ROT)[:, :, 0, :]
    nblk = S // DSA_BLOCK
    key_chunk = jnp.arange(S) // CHUNK
    scale = Dh ** -0.5

    def to_blocks(a):
        return jnp.moveaxis(a.reshape((B, nblk, DSA_BLOCK) + a.shape[2:]), 1, 0)

    gather_rows = jax.vmap(lambda a, i: a[i])

    def one_block(args):
        q_blk, qi_blk, wi_blk, bi = args
        q_chunk = (bi * DSA_BLOCK + jnp.arange(DSA_BLOCK)) // CHUNK
        admissible = key_chunk[None, :] <= q_chunk[:, None]
        logits = jnp.einsum('bqhd,bsd->bqhs', qi_blk, ki,
                            preferred_element_type=jnp.float32) * (IDX_DIM ** -0.5)
        score = jnp.einsum('bqhs,bqh->bqs', jax.nn.relu(logits),
                           wi_blk.astype(jnp.float32)) * (IDX_HEADS ** -0.5)
        score = jnp.where(admissible[None], score, -jnp.inf)
        top_val, top_idx = lax.top_k(score, topk)
        valid = jnp.isfinite(top_val)
        flat_idx = top_idx.reshape(B, DSA_BLOCK * topk)
        k_sel = gather_rows(k_flat, flat_idx).reshape(B, DSA_BLOCK, topk, H, Dh)
        v_sel = gather_rows(v_flat, flat_idx).reshape(B, DSA_BLOCK, topk, H, Dh)
        s = jnp.einsum('bqhd,bqkhd->bhqk', q_blk, k_sel, preferred_element_type=jnp.float32) * scale
        s = jnp.where(valid[:, None], s, -jnp.inf)
        p = jax.nn.softmax(s, axis=-1).astype(v_sel.dtype)
        return jnp.einsum('bhqk,bqkhd->bqhd', p, v_sel)

    o = lax.map(one_block, (to_blocks(q), to_blocks(qi), to_blocks(w_i), jnp.arange(nblk)))
    return jnp.moveaxis(o, 0, 1).reshape(B, S, H * Dh)


def _memory_branch(q_m, mem, g_mem, w_mem_kv, g_q, g_k):
    B, S, _ = q_m.shape
    M = mem.shape[1]
    q = _rmsnorm(q_m.reshape(B, S, MEM_HEADS, MEM_HEAD_DIM), g_q)
    kv = _rmsnorm(mem, g_mem) @ w_mem_kv
    k = _rmsnorm(kv[..., :MEM_HEADS * MEM_HEAD_DIM].reshape(B, M, MEM_HEADS, MEM_HEAD_DIM), g_k)
    v = kv[..., MEM_HEADS * MEM_HEAD_DIM:].reshape(B, M, MEM_HEADS, MEM_HEAD_DIM)
    s = jnp.einsum('bshd,bmhd->bhsm', q, k, preferred_element_type=jnp.float32) * (MEM_HEAD_DIM ** -0.5)
    p = jax.nn.softmax(s, axis=-1).astype(v.dtype)
    return jnp.einsum('bhsm,bmhd->bshd', p, v).reshape(B, S, MEM_HEADS * MEM_HEAD_DIM)


def _hybrid_mixer(h, mem, positions, w_in, g_cq, w_uq, g_ckv, w_ukv, g_q_mla, g_k_mla,
                  g_q_dsa, g_k_dsa, g_mem, w_mem_kv, g_q_mem, g_k_mem,
                  w_o_mla, w_o_dsa, w_o_mem, w_gate, w_out):
    c_q, c_kv, k_r, q_d, k_d, v_d, q_i, k_i, w_i, q_m = _split_cols(h @ w_in, IN_SIZES)
    y_mla = _mla_branch(c_q, c_kv, k_r, positions, g_cq, w_uq, g_ckv, w_ukv, g_q_mla, g_k_mla)
    y_dsa = _dsa_branch(q_d, k_d, v_d, q_i, k_i, w_i, positions, g_q_dsa, g_k_dsa)
    y_mem = _memory_branch(q_m, mem, g_mem, w_mem_kv, g_q_mem, g_k_mem)
    branches = ((y_mla, w_o_mla), (y_dsa, w_o_dsa), (y_mem, w_o_mem))
    gated = [jax.nn.sigmoid(h @ w_gate[i]) * (y @ w_o) for i, (y, w_o) in enumerate(branches)]
    return (gated[0] + gated[1] + gated[2]) @ w_out


def setup_inputs(seed: int = 0) -> dict:
    key = jax.random.key(seed)
    keys = jax.random.split(key, 32)
    counter = [0]

    def nk():
        k = keys[counter[0]]
        counter[0] += 1
        return k

    def w(shape, fan_in):
        return jax.random.normal(nk(), shape, jnp.float32) * fan_in ** -0.5

    def g(shape):
        return 1.0 + 0.02 * jax.random.normal(nk(), shape, jnp.float32)

    L, D = DEPTH, D_MODEL
    x = jax.random.normal(nk(), (BATCH, SEQ, D), jnp.float32)
    mem = jax.random.normal(nk(), (BATCH, MEM_LEN, D), jnp.float32)
    offset = jax.random.randint(nk(), (BATCH, 1), 0, 16) * CHUNK
    positions = (offset + jnp.arange(SEQ, dtype=jnp.int32)[None, :]).astype(jnp.int32)
    return {
        'x': x,
        'mem': mem,
        'positions': positions,
        'g_ff1': g((L, D)),
        'w_ff1_gate': w((L, D, D_FF), D),
        'w_ff1_up': w((L, D, D_FF), D),
        'w_ff1_down': w((L, D_FF, D), D_FF),
        'g_mix': g((L, D)),
        'w_in': w((L, D, D_IN), D),
        'g_cq': g((L, MLA_Q_RANK)),
        'w_uq': w((L, MLA_Q_RANK, MLA_HEADS * (MLA_ROPE + MLA_NOPE)), MLA_Q_RANK),
        'g_ckv': g((L, MLA_KV_RANK)),
        'w_ukv': w((L, MLA_KV_RANK, MLA_HEADS * (MLA_NOPE + MLA_V)), MLA_KV_RANK),
        'g_q_mla': g((L, MLA_ROPE + MLA_NOPE)),
        'g_k_mla': g((L, MLA_ROPE + MLA_NOPE)),
        'g_q_dsa': g((L, DSA_HEAD_DIM)),
        'g_k_dsa': g((L, DSA_HEAD_DIM)),
        'g_mem': g((L, D)),
        'w_mem_kv': w((L, D, 2 * MEM_HEADS * MEM_HEAD_DIM), D),
        'g_q_mem': g((L, MEM_HEAD_DIM)),
        'g_k_mem': g((L, MEM_HEAD_DIM)),
        'w_o_mla': w((L, MLA_HEADS * MLA_V, D), MLA_HEADS * MLA_V),
        'w_o_dsa': w((L, DSA_HEADS * DSA_HEAD_DIM, D), DSA_HEADS * DSA_HEAD_DIM),
        'w_o_mem': w((L, MEM_HEADS * MEM_HEAD_DIM, D), MEM_HEADS * MEM_HEAD_DIM),
        'w_gate': w((L, N_BRANCH, D, D), D),
        'w_out': w((L, D, D), D),
        'g_ff2': g((L, D)),
        'w_ff2_gate': w((L, D, D_FF), D),
        'w_ff2_up': w((L, D, D_FF), D),
        'w_ff2_down': w((L, D_FF, D), D_FF),
    }


def reference(x, mem, positions, g_ff1, w_ff1_gate, w_ff1_up, w_ff1_down, g_mix, w_in,
              g_cq, w_uq, g_ckv, w_ukv, g_q_mla, g_k_mla, g_q_dsa, g_k_dsa,
              g_mem, w_mem_kv, g_q_mem, g_k_mem, w_o_mla, w_o_dsa, w_o_mem,
              w_gate, w_out, g_ff2, w_ff2_gate, w_ff2_up, w_ff2_down):
    for l in range(DEPTH):
        x = x + 0.5 * _swiglu(_rmsnorm(x, g_ff1[l]), w_ff1_gate[l], w_ff1_up[l], w_ff1_down[l])
        h = _rmsnorm(x, g_mix[l])
        x = x + _hybrid_mixer(h, mem, positions, w_in[l], g_cq[l], w_uq[l], g_ckv[l], w_ukv[l],
                              g_q_mla[l], g_k_mla[l], g_q_dsa[l], g_k_dsa[l],
                              g_mem[l], w_mem_kv[l], g_q_mem[l], g_k_mem[l],
                              w_o_mla[l], w_o_dsa[l], w_o_mem[l], w_gate[l], w_out[l])
        x = x + 0.5 * _swiglu(_rmsnorm(x, g_ff2[l]), w_ff2_gate[l], w_ff2_up[l], w_ff2_down[l])
    return x
```

```python
import functools

import jax
import jax.numpy as jnp
from jax import lax
from jax.experimental import pallas as pl
from jax.experimental.pallas import tpu as pltpu

CHUNK = 64
ROPE_THETA = 500000.0
EPS = 1e-6

MLA_HEADS = 12
MLA_NOPE = 128
MLA_ROPE = 64
MLA_V = 128
MLA_QK = MLA_ROPE + MLA_NOPE
MLA_QK_PAD = 256

DSA_HEADS = 12
DSA_HEAD_DIM = 128
DSA_ROT = DSA_HEAD_DIM // 4
DSA_MAX_TOPK = 256
IDX_HEADS = 32
IDX_DIM = 64
IDX_ROT = IDX_DIM // 4

MEM_HEADS = 4
MEM_HEAD_DIM = 256

LANES = 128
VMEM_LIMIT_BYTES = 56 * 1024 * 1024

F32 = jnp.float32
BF16 = jnp.bfloat16
NEG_INF = float("-inf")
INT_MIN = -2 ** 31
NEG_INF_KEY = (0xFF800000 - (1 << 32)) ^ 0x7FFFFFFF


def _params(*semantics):
    return pltpu.CompilerParams(dimension_semantics=semantics, vmem_limit_bytes=VMEM_LIMIT_BYTES)


def _block(dim, preferred):
    if dim <= preferred:
        return dim
    b = preferred - preferred % LANES
    while b >= LANES:
        if dim % b == 0:
            return b
        b -= LANES
    return dim


def _rms_scale(x, width):
    return lax.rsqrt(jnp.sum(x * x, axis=-1, keepdims=True) * (1.0 / width) + EPS)


def _dot(a, b):
    return jnp.dot(a, b, preferred_element_type=F32)


def _dot_nt(a, b):
    return lax.dot_general(a, b, (((1,), (1,)), ((), ())), preferred_element_type=F32)


def _sigmoid(x):
    return 1.0 / (1.0 + jnp.exp(-x))


def _chunk_of(frame):
    return lax.shift_right_logical(frame, CHUNK.bit_length() - 1)


def _rmsnorm_kernel(x_ref, g_ref, o_ref):
    x = x_ref[...]
    o_ref[...] = (x * _rms_scale(x, x.shape[-1]) * g_ref[...]).astype(o_ref.dtype)


def _rmsnorm(x, g, out_dtype=BF16):
    m, d = x.shape
    bm = _block(m, 512)
    return pl.pallas_call(
        _rmsnorm_kernel,
        grid=(m // bm,),
        in_specs=[pl.BlockSpec((bm, d), lambda i: (i, 0)), pl.BlockSpec((1, d), lambda i: (0, 0))],
        out_specs=pl.BlockSpec((bm, d), lambda i: (i, 0)),
        out_shape=jax.ShapeDtypeStruct((m, d), out_dtype),
        compiler_params=_params("parallel"),
        name="rmsnorm",
    )(x, g.reshape(1, d))


def _matmul_kernel(a_ref, b_ref, o_ref):
    o_ref[...] = _dot(a_ref[...], b_ref[...]).astype(o_ref.dtype)


def _matmul(a, b, out_dtype, bm=1024, bn=512, name="matmul"):
    m, k = a.shape
    n = b.shape[1]
    bm, bn = _block(m, bm), _block(n, bn)
    return pl.pallas_call(
        _matmul_kernel,
        grid=(m // bm, n // bn),
        in_specs=[pl.BlockSpec((bm, k), lambda i, j: (i, 0)), pl.BlockSpec((k, bn), lambda i, j: (0, j))],
        out_specs=pl.BlockSpec((bm, bn), lambda i, j: (i, j)),
        out_shape=jax.ShapeDtypeStruct((m, n), out_dtype),
        compiler_params=_params("parallel", "parallel"),
        name=name,
    )(a, b)


def _swiglu_up_kernel(a_ref, wg_ref, wu_ref, o_ref):
    a = a_ref[...]
    g = _dot(a, wg_ref[...])
    u = _dot(a, wu_ref[...])
    o_ref[...] = (g * _sigmoid(g) * u).astype(o_ref.dtype)


def _swiglu_up(a, wg, wu, bm=1024, bn=512):
    m, k = a.shape
    n = wg.shape[1]
    bm, bn = _block(m, bm), _block(n, bn)
    return pl.pallas_call(
        _swiglu_up_kernel,
        grid=(m // bm, n // bn),
        in_specs=[pl.BlockSpec((bm, k), lambda i, j: (i, 0)),
                  pl.BlockSpec((k, bn), lambda i, j: (0, j)),
                  pl.BlockSpec((k, bn), lambda i, j: (0, j))],
        out_specs=pl.BlockSpec((bm, bn), lambda i, j: (i, j)),
        out_shape=jax.ShapeDtypeStruct((m, n), BF16),
        compiler_params=_params("parallel", "parallel"),
        name="swiglu_up",
    )(a, wg, wu)


def _matmul_residual_kernel(a_ref, b_ref, x_ref, o_ref, *, scale):
    o_ref[...] = x_ref[...] + scale * _dot(a_ref[...], b_ref[...])


def _matmul_residual(a, b, x, scale, bm=512, bn=512, name="matmul_residual"):
    m, k = a.shape
    n = b.shape[1]
    bm, bn = _block(m, bm), _block(n, bn)
    return pl.pallas_call(
        functools.partial(_matmul_residual_kernel, scale=scale),
        grid=(m // bm, n // bn),
        in_specs=[pl.BlockSpec((bm, k), lambda i, j: (i, 0)),
                  pl.BlockSpec((k, bn), lambda i, j: (0, j)),
                  pl.BlockSpec((bm, bn), lambda i, j: (i, j))],
        out_specs=pl.BlockSpec((bm, bn), lambda i, j: (i, j)),
        out_shape=jax.ShapeDtypeStruct((m, n), F32),
        compiler_params=_params("parallel", "parallel"),
        name=name,
    )(a, b, x)


def _ffn_half_step(x, g, w_gate, w_up, w_down):
    h = _rmsnorm(x, g)
    a = _swiglu_up(h, w_gate.astype(BF16), w_up.astype(BF16))
    return _matmul_residual(a, w_down.astype(BF16), x, 0.5, name="ffn_down")


def _gate_combine_kernel(h_ref, ya_ref, yb_ref, yc_ref, wg_ref, wa_ref, wb_ref, wc_ref, o_ref):
    h = h_ref[...]
    out = _sigmoid(_dot(h, wg_ref[0])) * _dot(ya_ref[...], wa_ref[...])
    out = out + _sigmoid(_dot(h, wg_ref[1])) * _dot(yb_ref[...], wb_ref[...])
    out = out + _sigmoid(_dot(h, wg_ref[2])) * _dot(yc_ref[...], wc_ref[...])
    o_ref[...] = out.astype(o_ref.dtype)


def _gate_combine(h, ys, w_gate, w_os, bm=512, bn=512):
    m, d = h.shape
    n = w_gate.shape[-1]
    bm, bn = _block(m, bm), _block(n, bn)
    row = lambda w: pl.BlockSpec((bm, w), lambda i, j: (i, 0))
    col = lambda k: pl.BlockSpec((k, bn), lambda i, j: (0, j))
    return pl.pallas_call(
        _gate_combine_kernel,
        grid=(m // bm, n // bn),
        in_specs=[row(d)] + [row(y.shape[1]) for y in ys]
                 + [pl.BlockSpec((3, d, bn), lambda i, j: (0, 0, j))] + [col(w.shape[0]) for w in w_os],
        out_specs=pl.BlockSpec((bm, bn), lambda i, j: (i, j)),
        out_shape=jax.ShapeDtypeStruct((m, n), BF16),
        compiler_params=_params("parallel", "parallel"),
        name="gate_combine",
    )(h, *ys, w_gate, *w_os)


def _rope_tables_kernel(pos_ref, invf_ref, sgn_ref, cos_ref, sin_ref):
    pos = pos_ref[...].astype(F32)
    for t in range(invf_ref.shape[0]):
        ang = pos * invf_ref[t:t + 1, :]
        cos_ref[t] = jnp.cos(ang)
        sin_ref[t] = jnp.sin(ang) * sgn_ref[t:t + 1, :]


def _rope_lane_tables(rot, period):
    half = rot // 2
    inv_freq = ROPE_THETA ** (-jnp.arange(half, dtype=F32) / half)
    lane = jnp.arange(LANES) % period
    invf = jnp.where(lane < rot, inv_freq[lane % half], 0.0)
    sgn = jnp.where(lane < half, -1.0, jnp.where(lane < rot, 1.0, 0.0))
    return invf.astype(F32), sgn.astype(F32)


def _rope_tables(positions):
    t = positions.size
    specs = [_rope_lane_tables(MLA_ROPE, LANES), _rope_lane_tables(DSA_ROT, LANES),
             _rope_lane_tables(IDX_ROT, IDX_DIM)]
    invf = jnp.stack([s[0] for s in specs])
    sgn = jnp.stack([s[1] for s in specs])
    bm = _block(t, 1024)
    n = invf.shape[0]
    const = pl.BlockSpec((n, LANES), lambda i: (0, 0))
    out = pl.BlockSpec((n, bm, LANES), lambda i: (0, i, 0))
    return pl.pallas_call(
        _rope_tables_kernel,
        grid=(t // bm,),
        in_specs=[pl.BlockSpec((bm, 1), lambda i: (i, 0)), const, const],
        out_specs=[out, out],
        out_shape=[jax.ShapeDtypeStruct((n, t, LANES), F32)] * 2,
        compiler_params=_params("parallel"),
        name="rope_tables",
    )(positions.reshape(t, 1), invf, sgn)


def _rope_tile(x, cos, sin, half, period):
    lane = lax.broadcasted_iota(jnp.int32, x.shape, 1) & (period - 1)
    partner = jnp.where(lane < half, pltpu.roll(x, LANES - half, 1), pltpu.roll(x, half, 1))
    return x * cos + partner * sin


def _mla_q_kernel(cq_ref, gcq_ref, w_ref, gq_ref, cos_ref, sin_ref, o_ref):
    x = cq_ref[...]
    xn = (x * _rms_scale(x, x.shape[-1]) * gcq_ref[...]).astype(BF16)
    q = _dot(xn, w_ref[...])
    cos, sin = cos_ref[0], sin_ref[0]
    g = gq_ref[...]
    for h in range(MLA_HEADS):
        lo = h * MLA_QK_PAD
        qh = q[:, lo:lo + MLA_QK_PAD]
        y = qh * _rms_scale(qh, MLA_QK) * g
        o_ref[:, lo:lo + LANES] = _rope_tile(y[:, :LANES], cos, sin, MLA_ROPE // 2, LANES).astype(o_ref.dtype)
        o_ref[:, lo + LANES:lo + MLA_QK_PAD] = y[:, LANES:].astype(o_ref.dtype)


def _mla_kv_kernel(ckv_ref, kr_ref, gckv_ref, wk_ref, wv_ref, gk_ref, cos_ref, sin_ref, k_ref, v_ref):
    x = ckv_ref[...]
    xn = (x * _rms_scale(x, x.shape[-1]) * gckv_ref[...]).astype(BF16)
    k_nope = _dot(xn, wk_ref[...])
    v_ref[...] = _dot(xn, wv_ref[...]).astype(v_ref.dtype)
    kr = kr_ref[...]
    lane = lax.broadcasted_iota(jnp.int32, kr.shape, 1)
    kr = jnp.where(lane < MLA_ROPE, kr, 0.0)
    cos, sin = cos_ref[0], sin_ref[0]
    g = gk_ref[...]
    for h in range(MLA_HEADS):
        lo = h * MLA_QK_PAD
        ka = k_nope[:, lo:lo + LANES] + kr
        kb = k_nope[:, lo + LANES:lo + MLA_QK_PAD]
        ms = (jnp.sum(ka * ka, axis=-1, keepdims=True) + jnp.sum(kb * kb, axis=-1, keepdims=True)) * (1.0 / MLA_QK)
        r = lax.rsqrt(ms + EPS)
        ya = ka * r * g[:, :LANES]
        yb = kb * r * g[:, LANES:]
        k_ref[:, lo:lo + LANES] = _rope_tile(ya, cos, sin, MLA_ROPE // 2, LANES).astype(k_ref.dtype)
        k_ref[:, lo + LANES:lo + MLA_QK_PAD] = yb.astype(k_ref.dtype)


def _pad_last(a, width):
    return jnp.pad(a, [(0, 0)] * (a.ndim - 1) + [(0, width - a.shape[-1])])


def _mla_prep(proj, col, g_cq, w_uq, g_ckv, w_ukv, g_q, g_k, cos, sin):
    t = proj.shape[0]
    q_rank, kv_rank = w_uq.shape[0], w_ukv.shape[0]
    hp = MLA_HEADS * MLA_QK_PAD
    w_q = _pad_last(w_uq.reshape(q_rank, MLA_HEADS, MLA_QK), MLA_QK_PAD).reshape(q_rank, hp).astype(BF16)
    w_kv = w_ukv.reshape(kv_rank, MLA_HEADS, MLA_NOPE + MLA_V)
    w_k = jnp.pad(w_kv[..., :MLA_NOPE], ((0, 0), (0, 0), (MLA_ROPE, MLA_QK_PAD - MLA_QK)))
    w_k = w_k.reshape(kv_rank, hp).astype(BF16)
    w_v = w_kv[..., MLA_NOPE:].reshape(kv_rank, MLA_HEADS * MLA_V).astype(BF16)
    g_qp = _pad_last(g_q, MLA_QK_PAD).reshape(1, MLA_QK_PAD)
    g_kp = _pad_last(g_k, MLA_QK_PAD).reshape(1, MLA_QK_PAD)

    bm = _block(t, 512)
    rows = lambda w, c: pl.BlockSpec((bm, w), lambda i, c=c: (i, c))
    full = lambda a: pl.BlockSpec(a.shape, lambda i: (0,) * a.ndim)
    table = pl.BlockSpec((1, bm, LANES), lambda i: (0, i, 0))
    g_cq2, g_ckv2 = g_cq.reshape(1, -1), g_ckv.reshape(1, -1)

    q = pl.pallas_call(
        _mla_q_kernel,
        grid=(t // bm,),
        in_specs=[rows(q_rank, col["c_q"] // q_rank), full(g_cq2), full(w_q), full(g_qp), table, table],
        out_specs=pl.BlockSpec((bm, hp), lambda i: (i, 0)),
        out_shape=jax.ShapeDtypeStruct((t, hp), BF16),
        compiler_params=_params("parallel"),
        name="mla_q_prep",
    )(proj, g_cq2, w_q, g_qp, cos, sin)

    k, v = pl.pallas_call(
        _mla_kv_kernel,
        grid=(t // bm,),
        in_specs=[rows(kv_rank, col["c_kv"] // kv_rank), rows(LANES, col["k_r"] // LANES), full(g_ckv2),
                  full(w_k), full(w_v), full(g_kp), table, table],
        out_specs=[pl.BlockSpec((bm, hp), lambda i: (i, 0)),
                   pl.BlockSpec((bm, MLA_HEADS * MLA_V), lambda i: (i, 0))],
        out_shape=[jax.ShapeDtypeStruct((t, hp), BF16), jax.ShapeDtypeStruct((t, MLA_HEADS * MLA_V), BF16)],
        compiler_params=_params("parallel"),
        name="mla_kv_prep",
    )(proj, proj, g_ckv2, w_k, w_v, g_kp, cos, sin)
    return q, k, v


def _flash_step(q, k, v, visible, m_ref, l_ref, acc_ref, scale):
    s = _dot_nt(q, k) * scale
    if visible is not None:
        s = jnp.where(visible, s, NEG_INF)
    m_prev = m_ref[...]
    m_new = jnp.maximum(m_prev, jnp.max(s, axis=-1, keepdims=True))
    m_safe = jnp.where(m_new == NEG_INF, 0.0, m_new)
    alpha = jnp.exp(m_prev - m_safe)
    p = jnp.exp(s - m_safe)
    l_ref[...] = alpha * l_ref[...] + jnp.sum(p, axis=-1, keepdims=True)
    acc_ref[...] = alpha * acc_ref[...] + _dot(p.astype(v.dtype), v)
    m_ref[...] = m_new


def _flash_init(m_ref, l_ref, acc_ref):
    m_ref[...] = jnp.full(m_ref.shape, NEG_INF, F32)
    l_ref[...] = jnp.zeros(l_ref.shape, F32)
    acc_ref[...] = jnp.zeros(acc_ref.shape, F32)


def _chunk_causal_attn_kernel(q_ref, k_ref, v_ref, o_ref, m_ref, l_ref, acc_ref, *, tq, scale):
    i = pl.program_id(2)
    _flash_init(m_ref, l_ref, acc_ref)
    q = q_ref[...]

    def earlier_block(j, carry):
        off = pl.multiple_of(j * tq, tq)
        _flash_step(q, k_ref[pl.ds(off, tq), :], v_ref[pl.ds(off, tq), :], None, m_ref, l_ref, acc_ref, scale)
        return carry

    lax.fori_loop(0, i, earlier_block, 0)
    off = pl.multiple_of(i * tq, tq)
    row_chunk = _chunk_of(lax.broadcasted_iota(jnp.int32, (tq, tq), 0))
    col_chunk = _chunk_of(lax.broadcasted_iota(jnp.int32, (tq, tq), 1))
    _flash_step(q, k_ref[pl.ds(off, tq), :], v_ref[pl.ds(off, tq), :], col_chunk <= row_chunk,
                m_ref, l_ref, acc_ref, scale)
    o_ref[...] = (acc_ref[...] / l_ref[...]).astype(o_ref.dtype)


def _flash_scratch(tq, dv):
    return [pltpu.VMEM((tq, 1), F32), pltpu.VMEM((tq, 1), F32), pltpu.VMEM((tq, dv), F32)]


def _chunk_causal_attention(q, k, v, batch, heads, dk, dv, scale, tq=256):
    t = q.shape[0]
    s = t // batch
    tq = _block(s, tq)
    nq = s // tq
    return pl.pallas_call(
        functools.partial(_chunk_causal_attn_kernel, tq=tq, scale=scale),
        grid=(batch, heads, nq),
        in_specs=[pl.BlockSpec((tq, dk), lambda b, h, i: (b * nq + i, h)),
                  pl.BlockSpec((s, dk), lambda b, h, i: (b, h)),
                  pl.BlockSpec((s, dv), lambda b, h, i: (b, h))],
        out_specs=pl.BlockSpec((tq, dv), lambda b, h, i: (b * nq + i, h)),
        out_shape=jax.ShapeDtypeStruct((t, heads * dv), BF16),
        scratch_shapes=_flash_scratch(tq, dv),
        compiler_params=_params("parallel", "parallel", "parallel"),
        name="chunk_causal_attention",
    )(q, k, v)


def _masked_attn_kernel(q_ref, k_ref, v_ref, sel_ref, o_ref, m_ref, l_ref, acc_ref, *, tq, scale):
    i = pl.program_id(2)
    _flash_init(m_ref, l_ref, acc_ref)
    q = q_ref[...]

    def key_block(j, carry):
        off = pl.multiple_of(j * tq, tq)
        visible = sel_ref[:, pl.ds(off, tq)].astype(F32) > 0.0
        _flash_step(q, k_ref[pl.ds(off, tq), :], v_ref[pl.ds(off, tq), :], visible, m_ref, l_ref, acc_ref, scale)
        return carry

    lax.fori_loop(0, i + 1, key_block, 0)
    o_ref[...] = (acc_ref[...] / l_ref[...]).astype(o_ref.dtype)


def _masked_attention(q, k, v, selected, batch, heads, dh, scale, tq=256):
    t = q.shape[0]
    s = t // batch
    tq = _block(s, tq)
    nq = s // tq
    return pl.pallas_call(
        functools.partial(_masked_attn_kernel, tq=tq, scale=scale),
        grid=(batch, heads, nq),
        in_specs=[pl.BlockSpec((tq, dh), lambda b, h, i: (b * nq + i, h)),
                  pl.BlockSpec((s, dh), lambda b, h, i: (b, h)),
                  pl.BlockSpec((s, dh), lambda b, h, i: (b, h)),
                  pl.BlockSpec((tq, s), lambda b, h, i: (b * nq + i, 0))],
        out_specs=pl.BlockSpec((tq, dh), lambda b, h, i: (b * nq + i, h)),
        out_shape=jax.ShapeDtypeStruct((t, heads * dh), BF16),
        scratch_shapes=_flash_scratch(tq, dh),
        compiler_params=_params("parallel", "parallel", "parallel"),
        name="masked_attention",
    )(q, k, v, selected)


def _head_norm_rope_kernel(x_ref, g_ref, cos_ref, sin_ref, o_ref, *, heads):
    cos, sin = cos_ref[0], sin_ref[0]
    g = g_ref[...]
    for h in range(heads):
        x = x_ref[:, h * LANES:(h + 1) * LANES]
        y = x * _rms_scale(x, LANES) * g
        o_ref[:, h * LANES:(h + 1) * LANES] = _rope_tile(y, cos, sin, DSA_ROT // 2, LANES).astype(o_ref.dtype)


def _head_norm_rope(proj, col_block, g, cos, sin):
    t = proj.shape[0]
    width = DSA_HEADS * DSA_HEAD_DIM
    bm = _block(t, 512)
    table = pl.BlockSpec((1, bm, LANES), lambda i: (1, i, 0))
    return pl.pallas_call(
        functools.partial(_head_norm_rope_kernel, heads=DSA_HEADS),
        grid=(t // bm,),
        in_specs=[pl.BlockSpec((bm, width), lambda i: (i, col_block)),
                  pl.BlockSpec((1, LANES), lambda i: (0, 0)), table, table],
        out_specs=pl.BlockSpec((bm, width), lambda i: (i, 0)),
        out_shape=jax.ShapeDtypeStruct((t, width), BF16),
        compiler_params=_params("parallel"),
        name="head_norm_rope",
    )(proj, g.reshape(1, LANES), cos, sin)


def _rope_only_kernel(x_ref, cos_ref, sin_ref, o_ref):
    cos, sin = cos_ref[0], sin_ref[0]
    for c in range(x_ref.shape[1] // LANES):
        x = x_ref[:, c * LANES:(c + 1) * LANES]
        o_ref[:, c * LANES:(c + 1) * LANES] = _rope_tile(x, cos, sin, IDX_ROT // 2, IDX_DIM).astype(o_ref.dtype)


def _rope_only(proj, width, col_block, cos, sin):
    t = proj.shape[0]
    bm = _block(t, 512)
    table = pl.BlockSpec((1, bm, LANES), lambda i: (2, i, 0))
    return pl.pallas_call(
        _rope_only_kernel,
        grid=(t // bm,),
        in_specs=[pl.BlockSpec((bm, width), lambda i: (i, col_block)), table, table],
        out_specs=pl.BlockSpec((bm, width), lambda i: (i, 0)),
        out_shape=jax.ShapeDtypeStruct((t, width), BF16),
        compiler_params=_params("parallel"),
        name="indexer_rope",
    )(proj, cos, sin)


def _sortable_key(x):
    bits = lax.bitcast_convert_type(x, jnp.int32)
    return bits ^ ((bits >> 31) & 0x7FFFFFFF)


def _indexer_kernel(qi_ref, kit_ref, w_ref, sel_ref, key_ref, wb_ref, *, tq, tk, topk, seq, w_scale):
    i = pl.program_id(1)
    lanes_per_tile = tk // LANES
    n_tiles = (i + 1) * (tq // tk)
    row_chunk = _chunk_of(i * tq + lax.broadcasted_iota(jnp.int32, (tq, 1), 0))

    w = w_ref[...] * w_scale
    for h in range(IDX_HEADS):
        wb_ref[h] = jnp.broadcast_to(w[:, h:h + 1], (tq, LANES))

    def score_tile(j, carry):
        off = pl.multiple_of(j * tk, tk)
        kt = kit_ref[:, pl.ds(off, tk)]
        acc = [jnp.zeros((tq, LANES), F32) for _ in range(lanes_per_tile)]
        for h in range(IDX_HEADS):
            logits = _dot(qi_ref[h], kt)
            wh = wb_ref[h]
            for c in range(lanes_per_tile):
                acc[c] = acc[c] + jnp.maximum(logits[:, c * LANES:(c + 1) * LANES], 0.0) * wh
        for c in range(lanes_per_tile):
            col = off + c * LANES + lax.broadcasted_iota(jnp.int32, (1, LANES), 1)
            score = jnp.where(_chunk_of(col) <= row_chunk, acc[c], NEG_INF)
            key_ref[:, pl.ds(off + c * LANES, LANES)] = _sortable_key(score)
        return carry

    lax.fori_loop(0, n_tiles, score_tile, 0)

    def count(pred):
        def tile(j, acc):
            off = pl.multiple_of(j * tk, tk)
            for c in range(lanes_per_tile):
                kt = key_ref[:, pl.ds(off + c * LANES, LANES)]
                acc = acc + jnp.where(pred(kt, off + c * LANES), 1.0, 0.0)
            return acc
        acc = lax.fori_loop(0, n_tiles, tile, jnp.zeros((tq, LANES), F32))
        return jnp.sum(acc, axis=-1, keepdims=True)

    def value_bit(b, lo):
        cand = lo + lax.shift_left(jnp.int32(1), 31 - b)
        cand_b = jnp.broadcast_to(cand, (tq, LANES))
        n_ge = count(lambda kt, _: kt >= cand_b)
        return jnp.where(n_ge >= topk, cand, lo)

    thr = lax.fori_loop(0, 32, value_bit, jnp.full((tq, 1), INT_MIN, jnp.int32))
    thr_b = jnp.broadcast_to(thr, (tq, LANES))
    n_ge = count(lambda kt, _: kt >= thr_b)
    surplus = jnp.logical_and(n_ge > topk, thr > NEG_INF_KEY)

    def write_tile(j, carry):
        off = pl.multiple_of(j * tk, tk)
        for c in range(lanes_per_tile):
            kt = key_ref[:, pl.ds(off + c * LANES, LANES)]
            chosen = jnp.logical_and(kt >= thr_b, kt > NEG_INF_KEY)
            sel_ref[:, pl.ds(off + c * LANES, LANES)] = jnp.where(chosen, 1.0, 0.0).astype(sel_ref.dtype)
        return carry

    lax.fori_loop(0, n_tiles, write_tile, 0)

    def clear_tile(j, carry):
        off = pl.multiple_of(j * tk, tk)
        sel_ref[:, pl.ds(off, tk)] = jnp.zeros((tq, tk), sel_ref.dtype)
        return carry

    lax.fori_loop(n_tiles, seq // tk, clear_tile, 0)

    @pl.when(jnp.max(jnp.where(surplus, 1.0, 0.0)) > 0.0)
    def _():
        n_gt = count(lambda kt, _: kt > thr_b)
        need = topk - n_gt

        def tied_before(kt, first_col, bound_b):
            col = first_col + lax.broadcasted_iota(jnp.int32, (tq, LANES), 1)
            return jnp.logical_and(kt == thr_b, col < bound_b)

        def position_bit(b, base):
            cand = base + lax.shift_left(jnp.int32(1), (seq - 1).bit_length() - 1 - b)
            cand_b = jnp.broadcast_to(cand, (tq, LANES))
            n_before = count(lambda kt, c0: tied_before(kt, c0, cand_b))
            return jnp.where(n_before < need, cand, base)

        last = lax.fori_loop(0, (seq - 1).bit_length(), position_bit, jnp.zeros((tq, 1), jnp.int32))
        last_b = jnp.broadcast_to(last, (tq, LANES))

        def rewrite_tile(j, carry):
            off = pl.multiple_of(j * tk, tk)
            for c in range(lanes_per_tile):
                c0 = off + c * LANES
                kt = key_ref[:, pl.ds(c0, LANES)]
                col = c0 + lax.broadcasted_iota(jnp.int32, (tq, LANES), 1)
                chosen = jnp.logical_or(kt > thr_b, jnp.logical_and(kt == thr_b, col <= last_b))
                chosen = jnp.logical_and(chosen, kt > NEG_INF_KEY)
                sel_ref[:, pl.ds(c0, LANES)] = jnp.where(chosen, 1.0, 0.0).astype(sel_ref.dtype)
            return carry

        lax.fori_loop(0, n_tiles, rewrite_tile, 0)


def _indexer_select(qi, kit, w, w_col_block, batch, topk, tq=128, tk=256):
    s = qi.shape[2]
    tq = _block(s, tq)
    tk = min(tk, tq)
    nq = s // tq
    w_scale = (IDX_DIM ** -0.5) * (IDX_HEADS ** -0.5)
    return pl.pallas_call(
        functools.partial(_indexer_kernel, tq=tq, tk=tk, topk=topk, seq=s, w_scale=w_scale),
        grid=(batch, nq),
        in_specs=[pl.BlockSpec((None, IDX_HEADS, tq, IDX_DIM), lambda b, i: (b, 0, i, 0)),
                  pl.BlockSpec((None, IDX_DIM, s), lambda b, i: (b, 0, 0)),
                  pl.BlockSpec((tq, LANES), lambda b, i: (b * nq + i, w_col_block))],
        out_specs=pl.BlockSpec((tq, s), lambda b, i: (b * nq + i, 0)),
        out_shape=jax.ShapeDtypeStruct((batch * s, s), BF16),
        scratch_shapes=[pltpu.VMEM((tq, s), jnp.int32), pltpu.VMEM((IDX_HEADS, tq, LANES), F32)],
        compiler_params=_params("parallel", "parallel"),
        name="indexer_select",
    )(qi, kit, w)


def _memory_attn_kernel(q_ref, kv_ref, gq_ref, gk_ref, o_ref, *, scale):
    hd = MEM_HEAD_DIM
    gq, gk = gq_ref[...], gk_ref[...]
    for h in range(MEM_HEADS):
        q = q_ref[:, h * hd:(h + 1) * hd]
        k = kv_ref[:, h * hd:(h + 1) * hd]
        v = kv_ref[:, (MEM_HEADS + h) * hd:(MEM_HEADS + h + 1) * hd].astype(BF16)
        qn = (q * _rms_scale(q, hd) * gq).astype(BF16)
        kn = (k * _rms_scale(k, hd) * gk).astype(BF16)
        s = _dot_nt(qn, kn) * scale
        e = jnp.exp(s - jnp.max(s, axis=-1, keepdims=True))
        p = e / jnp.sum(e, axis=-1, keepdims=True)
        o_ref[:, h * hd:(h + 1) * hd] = _dot(p.astype(BF16), v).astype(o_ref.dtype)


def _memory_attention(proj, col_block, kv, g_q, g_k, batch, tq=512):
    t = proj.shape[0]
    s = t // batch
    m = kv.shape[0] // batch
    width = MEM_HEADS * MEM_HEAD_DIM
    tq = _block(s, tq)
    nq = s // tq
    g_spec = pl.BlockSpec((1, MEM_HEAD_DIM), lambda b, i: (0, 0))
    return pl.pallas_call(
        functools.partial(_memory_attn_kernel, scale=MEM_HEAD_DIM ** -0.5),
        grid=(batch, nq),
        in_specs=[pl.BlockSpec((tq, width), lambda b, i: (b * nq + i, col_block)),
                  pl.BlockSpec((m, 2 * width), lambda b, i: (b, 0)), g_spec, g_spec],
        out_specs=pl.BlockSpec((tq, width), lambda b, i: (b * nq + i, 0)),
        out_shape=jax.ShapeDtypeStruct((t, width), BF16),
        compiler_params=_params("parallel", "parallel"),
        name="memory_attention",
    )(proj, kv, g_q.reshape(1, -1), g_k.reshape(1, -1))


def _projection_layout(q_rank, kv_rank):
    dsa = DSA_HEADS * DSA_HEAD_DIM
    sizes = [("c_q", q_rank), ("c_kv", kv_rank), ("k_r", MLA_ROPE), ("q_d", dsa), ("k_d", dsa), ("v_d", dsa),
             ("q_i", IDX_HEADS * IDX_DIM), ("k_i", IDX_DIM), ("w_i", IDX_HEADS), ("q_m", MEM_HEADS * MEM_HEAD_DIM)]
    src, start = {}, 0
    for name, n in sizes:
        src[name] = (start, n)
        start += n
    return src, start


def _hybrid_mixer(x, h, mem, positions, w_in, g_cq, w_uq, g_ckv, w_ukv, g_q_mla, g_k_mla, g_q_dsa, g_k_dsa,
                  g_mem, w_mem_kv, g_q_mem, g_k_mem, w_o_mla, w_o_dsa, w_o_mem, w_gate, w_out, batch):
    t, d = h.shape
    s = t // batch
    q_rank, kv_rank = w_uq.shape[0], w_ukv.shape[0]
    src, d_in = _projection_layout(q_rank, kv_rank)
    assert d_in == w_in.shape[1]
    dsa = DSA_HEADS * DSA_HEAD_DIM
    assert q_rank == dsa, "the reordered projection layout assumes equal widths for these groups"

    take = lambda name: w_in[:, src[name][0]:src[name][0] + src[name][1]]
    pieces = [("c_q", take("c_q")), ("q_d", take("q_d")), ("k_d", take("k_d")), ("v_d", take("v_d")),
              ("q_i", take("q_i")), ("q_m", take("q_m")), ("c_kv", take("c_kv")),
              ("k_r", jnp.concatenate([take("k_r"), take("k_i")], axis=1)),
              ("w_i", _pad_last(take("w_i"), LANES))]
    col, start = {}, 0
    for name, piece in pieces:
        assert start % piece.shape[1] == 0, name
        col[name] = start
        start += piece.shape[1]
    w_in_r = jnp.concatenate([p for _, p in pieces], axis=1).astype(BF16)

    proj = _matmul(h, w_in_r, F32, bm=1024, bn=768, name="input_projection")
    cos, sin = _rope_tables(positions)

    q_a, k_a, v_a = _mla_prep(proj, col, g_cq, w_uq, g_ckv, w_ukv, g_q_mla, g_k_mla, cos, sin)
    y_mla = _chunk_causal_attention(q_a, k_a, v_a, batch, MLA_HEADS, MLA_QK_PAD, MLA_V, MLA_QK ** -0.5)

    q_b = _head_norm_rope(proj, col["q_d"] // dsa, g_q_dsa, cos, sin)
    k_b = _head_norm_rope(proj, col["k_d"] // dsa, g_k_dsa, cos, sin)
    v_b = proj[:, col["v_d"]:col["v_d"] + dsa].astype(BF16)
    qi_width = IDX_HEADS * IDX_DIM
    qi = _rope_only(proj, qi_width, col["q_i"] // qi_width, cos, sin)
    qi = qi.reshape(batch, s, IDX_HEADS, IDX_DIM).transpose(0, 2, 1, 3)
    ki = _rope_only(proj, LANES, col["k_r"] // LANES, cos, sin)[:, IDX_DIM:]
    kit = ki.reshape(batch, s, IDX_DIM).transpose(0, 2, 1)
    topk = min(DSA_MAX_TOPK, s // 4)
    selected = _indexer_select(qi, kit, proj, col["w_i"] // LANES, batch, topk)
    y_dsa = _masked_attention(q_b, k_b, v_b, selected, batch, DSA_HEADS, DSA_HEAD_DIM, DSA_HEAD_DIM ** -0.5)

    m_len = mem.shape[1]
    mem_n = _rmsnorm(mem.reshape(batch * m_len, d), g_mem)
    kv_mem = _matmul(mem_n, w_mem_kv.astype(BF16), F32, bm=512, bn=512, name="memory_kv")
    y_mem = _memory_attention(proj, col["q_m"] // (MEM_HEADS * MEM_HEAD_DIM), kv_mem, g_q_mem, g_k_mem, batch)

    gated = _gate_combine(h, (y_mla, y_dsa, y_mem), w_gate.astype(BF16),
                          (w_o_mla.astype(BF16), w_o_dsa.astype(BF16), w_o_mem.astype(BF16)))
    return _matmul_residual(gated, w_out.astype(BF16), x, 1.0, name="mixer_out")


def kernel(x, mem, positions, g_ff1, w_ff1_gate, w_ff1_up, w_ff1_down, g_mix, w_in, g_cq, w_uq, g_ckv, w_ukv, g_q_mla, g_k_mla, g_q_dsa, g_k_dsa, g_mem, w_mem_kv, g_q_mem, g_k_mem, w_o_mla, w_o_dsa, w_o_mem, w_gate, w_out, g_ff2, w_ff2_gate, w_ff2_up, w_ff2_down):
    batch, seq, d = x.shape
    xt = x.reshape(batch * seq, d)
    for l in range(g_ff1.shape[0]):
        xt = _ffn_half_step(xt, g_ff1[l], w_ff1_gate[l], w_ff1_up[l], w_ff1_down[l])
        h = _rmsnorm(xt, g_mix[l])
        xt = _hybrid_mixer(xt, h, mem, positions, w_in[l], g_cq[l], w_uq[l], g_ckv[l], w_ukv[l],
                           g_q_mla[l], g_k_mla[l], g_q_dsa[l], g_k_dsa[l], g_mem[l], w_mem_kv[l],
                           g_q_mem[l], g_k_mem[l], w_o_mla[l], w_o_dsa[l], w_o_mem[l], w_gate[l], w_out[l], batch)
        xt = _ffn_half_step(xt, g_ff2[l], w_ff2_gate[l], w_ff2_up[l], w_ff2_down[l])
    return xt.reshape(batch, seq, d)
```

```python
import functools

import jax
import jax.numpy as jnp
from jax import lax
from jax.experimental import pallas as pl
from jax.experimental.pallas import tpu as pltpu

CHUNK = 64
ROPE_THETA = 500000.0
EPS = 1e-6

MLA_HEADS = 12
MLA_NOPE = 128
MLA_ROPE = 64
MLA_V = 128
MLA_QK = MLA_ROPE + MLA_NOPE
MLA_QK_PAD = 256

DSA_HEADS = 12
DSA_HEAD_DIM = 128
DSA_ROT = DSA_HEAD_DIM // 4
DSA_MAX_TOPK = 256
IDX_HEADS = 32
IDX_DIM = 64
IDX_ROT = IDX_DIM // 4

MEM_HEADS = 4
MEM_HEAD_DIM = 256

LANES = 128
VMEM_LIMIT_BYTES = 56 * 1024 * 1024

F32 = jnp.float32
BF16 = jnp.bfloat16
NEG_INF = float("-inf")
INT_MIN = -2 ** 31
NEG_INF_KEY = (0xFF800000 - (1 << 32)) ^ 0x7FFFFFFF


def _params(*semantics):
    return pltpu.CompilerParams(dimension_semantics=semantics, vmem_limit_bytes=VMEM_LIMIT_BYTES)


def _block(dim, preferred):
    if dim <= preferred:
        return dim
    b = preferred - preferred % LANES
    while b >= LANES:
        if dim % b == 0:
            return b
        b -= LANES
    return dim


def _rms_scale(x, width):
    return lax.rsqrt(jnp.sum(x * x, axis=-1, keepdims=True) * (1.0 / width) + EPS)


def _dot(a, b):
    return jnp.dot(a, b, preferred_element_type=F32)


def _dot_nt(a, b):
    return lax.dot_general(a, b, (((1,), (1,)), ((), ())), preferred_element_type=F32)


def _sigmoid(x):
    return 1.0 / (1.0 + jnp.exp(-x))


def _chunk_of(frame):
    return lax.shift_right_logical(frame, CHUNK.bit_length() - 1)


def _rmsnorm_kernel(x_ref, g_ref, o_ref):
    x = x_ref[...]
    o_ref[...] = (x * _rms_scale(x, x.shape[-1]) * g_ref[...]).astype(o_ref.dtype)


def _rmsnorm(x, g, out_dtype=BF16):
    m, d = x.shape
    bm = _block(m, 512)
    return pl.pallas_call(
        _rmsnorm_kernel,
        grid=(m // bm,),
        in_specs=[pl.BlockSpec((bm, d), lambda i: (i, 0)), pl.BlockSpec((1, d), lambda i: (0, 0))],
        out_specs=pl.BlockSpec((bm, d), lambda i: (i, 0)),
        out_shape=jax.ShapeDtypeStruct((m, d), out_dtype),
        compiler_params=_params("parallel"),
        name="rmsnorm",
    )(x, g.reshape(1, d))


def _matmul_kernel(a_ref, b_ref, o_ref):
    o_ref[...] = _dot(a_ref[...], b_ref[...]).astype(o_ref.dtype)


def _matmul(a, b, out_dtype, bm=1024, bn=512, name="matmul"):
    m, k = a.shape
    n = b.shape[1]
    bm, bn = _block(m, bm), _block(n, bn)
    return pl.pallas_call(
        _matmul_kernel,
        grid=(m // bm, n // bn),
        in_specs=[pl.BlockSpec((bm, k), lambda i, j: (i, 0)), pl.BlockSpec((k, bn), lambda i, j: (0, j))],
        out_specs=pl.BlockSpec((bm, bn), lambda i, j: (i, j)),
        out_shape=jax.ShapeDtypeStruct((m, n), out_dtype),
        compiler_params=_params("parallel", "parallel"),
        name=name,
    )(a, b)


def _swiglu_up_kernel(a_ref, wg_ref, wu_ref, o_ref):
    a = a_ref[...]
    g = _dot(a, wg_ref[...])
    u = _dot(a, wu_ref[...])
    o_ref[...] = (g * _sigmoid(g) * u).astype(o_ref.dtype)


def _swiglu_up(a, wg, wu, bm=1024, bn=512):
    m, k = a.shape
    n = wg.shape[1]
    bm, bn = _block(m, bm), _block(n, bn)
    return pl.pallas_call(
        _swiglu_up_kernel,
        grid=(m // bm, n // bn),
        in_specs=[pl.BlockSpec((bm, k), lambda i, j: (i, 0)),
                  pl.BlockSpec((k, bn), lambda i, j: (0, j)),
                  pl.BlockSpec((k, bn), lambda i, j: (0, j))],
        out_specs=pl.BlockSpec((bm, bn), lambda i, j: (i, j)),
        out_shape=jax.ShapeDtypeStruct((m, n), BF16),
        compiler_params=_params("parallel", "parallel"),
        name="swiglu_up",
    )(a, wg, wu)


def _matmul_residual_kernel(a_ref, b_ref, x_ref, o_ref, *, scale):
    o_ref[...] = x_ref[...] + scale * _dot(a_ref[...], b_ref[...])


def _matmul_residual(a, b, x, scale, bm=512, bn=512, name="matmul_residual"):
    m, k = a.shape
    n = b.shape[1]
    bm, bn = _block(m, bm), _block(n, bn)
    return pl.pallas_call(
        functools.partial(_matmul_residual_kernel, scale=scale),
        grid=(m // bm, n // bn),
        in_specs=[pl.BlockSpec((bm, k), lambda i, j: (i, 0)),
                  pl.BlockSpec((k, bn), lambda i, j: (0, j)),
                  pl.BlockSpec((bm, bn), lambda i, j: (i, j))],
        out_specs=pl.BlockSpec((bm, bn), lambda i, j: (i, j)),
        out_shape=jax.ShapeDtypeStruct((m, n), F32),
        compiler_params=_params("parallel", "parallel"),
        name=name,
    )(a, b, x)


def _ffn_half_step(x, g, w_gate, w_up, w_down):
    h = _rmsnorm(x, g)
    a = _swiglu_up(h, w_gate.astype(BF16), w_up.astype(BF16))
    return _matmul_residual(a, w_down.astype(BF16), x, 0.5, name="ffn_down")


def _gate_combine_kernel(h_ref, ya_ref, yb_ref, yc_ref, wg_ref, wa_ref, wb_ref, wc_ref, o_ref):
    h = h_ref[...]
    out = _sigmoid(_dot(h, wg_ref[0])) * _dot(ya_ref[...], wa_ref[...])
    out = out + _sigmoid(_dot(h, wg_ref[1])) * _dot(yb_ref[...], wb_ref[...])
    out = out + _sigmoid(_dot(h, wg_ref[2])) * _dot(yc_ref[...], wc_ref[...])
    o_ref[...] = out.astype(o_ref.dtype)


def _gate_combine(h, ys, w_gate, w_os, bm=512, bn=512):
    m, d = h.shape
    n = w_gate.shape[-1]
    bm, bn = _block(m, bm), _block(n, bn)
    row = lambda w: pl.BlockSpec((bm, w), lambda i, j: (i, 0))
    col = lambda k: pl.BlockSpec((k, bn), lambda i, j: (0, j))
    return pl.pallas_call(
        _gate_combine_kernel,
        grid=(m // bm, n // bn),
        in_specs=[row(d)] + [row(y.shape[1]) for y in ys]
                 + [pl.BlockSpec((3, d, bn), lambda i, j: (0, 0, j))] + [col(w.shape[0]) for w in w_os],
        out_specs=pl.BlockSpec((bm, bn), lambda i, j: (i, j)),
        out_shape=jax.ShapeDtypeStruct((m, n), BF16),
        compiler_params=_params("parallel", "parallel"),
        name="gate_combine",
    )(h, *ys, w_gate, *w_os)


def _rope_tables_kernel(pos_ref, invf_ref, sgn_ref, cos_ref, sin_ref):
    pos = pos_ref[...].astype(F32)
    for t in range(invf_ref.shape[0]):
        ang = pos * invf_ref[t:t + 1, :]
        cos_ref[t] = jnp.cos(ang)
        sin_ref[t] = jnp.sin(ang) * sgn_ref[t:t + 1, :]


def _rope_lane_tables(rot, period):
    half = rot // 2
    inv_freq = ROPE_THETA ** (-jnp.arange(half, dtype=F32) / half)
    lane = jnp.arange(LANES) % period
    invf = jnp.where(lane < rot, inv_freq[lane % half], 0.0)
    sgn = jnp.where(lane < half, -1.0, jnp.where(lane < rot, 1.0, 0.0))
    return invf.astype(F32), sgn.astype(F32)


def _rope_tables(positions):
    t = positions.size
    specs = [_rope_lane_tables(MLA_ROPE, LANES), _rope_lane_tables(DSA_ROT, LANES),
             _rope_lane_tables(IDX_ROT, IDX_DIM)]
    invf = jnp.stack([s[0] for s in specs])
    sgn = jnp.stack([s[1] for s in specs])
    bm = _block(t, 1024)
    n = invf.shape[0]
    const = pl.BlockSpec((n, LANES), lambda i: (0, 0))
    out = pl.BlockSpec((n, bm, LANES), lambda i: (0, i, 0))
    return pl.pallas_call(
        _rope_tables_kernel,
        grid=(t // bm,),
        in_specs=[pl.BlockSpec((bm, 1), lambda i: (i, 0)), const, const],
        out_specs=[out, out],
        out_shape=[jax.ShapeDtypeStruct((n, t, LANES), F32)] * 2,
        compiler_params=_params("parallel"),
        name="rope_tables",
    )(positions.reshape(t, 1), invf, sgn)


def _rope_tile(x, cos, sin, half, period):
    lane = lax.broadcasted_iota(jnp.int32, x.shape, 1) & (period - 1)
    partner = jnp.where(lane < half, pltpu.roll(x, LANES - half, 1), pltpu.roll(x, half, 1))
    return x * cos + partner * sin


def _mla_q_kernel(cq_ref, gcq_ref, w_ref, gq_ref, cos_ref, sin_ref, o_ref):
    x = cq_ref[...]
    xn = (x * _rms_scale(x, x.shape[-1]) * gcq_ref[...]).astype(BF16)
    q = _dot(xn, w_ref[...])
    cos, sin = cos_ref[0], sin_ref[0]
    g = gq_ref[...]
    for h in range(MLA_HEADS):
        lo = h * MLA_QK_PAD
        qh = q[:, lo:lo + MLA_QK_PAD]
        y = qh * _rms_scale(qh, MLA_QK) * g
        o_ref[:, lo:lo + LANES] = _rope_tile(y[:, :LANES], cos, sin, MLA_ROPE // 2, LANES).astype(o_ref.dtype)
        o_ref[:, lo + LANES:lo + MLA_QK_PAD] = y[:, LANES:].astype(o_ref.dtype)


def _mla_kv_kernel(ckv_ref, kr_ref, gckv_ref, wk_ref, wv_ref, gk_ref, cos_ref, sin_ref, k_ref, v_ref):
    x = ckv_ref[...]
    xn = (x * _rms_scale(x, x.shape[-1]) * gckv_ref[...]).astype(BF16)
    k_nope = _dot(xn, wk_ref[...])
    v_ref[...] = _dot(xn, wv_ref[...]).astype(v_ref.dtype)
    kr = kr_ref[...]
    lane = lax.broadcasted_iota(jnp.int32, kr.shape, 1)
    kr = jnp.where(lane < MLA_ROPE, kr, 0.0)
    cos, sin = cos_ref[0], sin_ref[0]
    g = gk_ref[...]
    for h in range(MLA_HEADS):
        lo = h * MLA_QK_PAD
        ka = k_nope[:, lo:lo + LANES] + kr
        kb = k_nope[:, lo + LANES:lo + MLA_QK_PAD]
        ms = (jnp.sum(ka * ka, axis=-1, keepdims=True) + jnp.sum(kb * kb, axis=-1, keepdims=True)) * (1.0 / MLA_QK)
        r = lax.rsqrt(ms + EPS)
        ya = ka * r * g[:, :LANES]
        yb = kb * r * g[:, LANES:]
        k_ref[:, lo:lo + LANES] = _rope_tile(ya, cos, sin, MLA_ROPE // 2, LANES).astype(k_ref.dtype)
        k_ref[:, lo + LANES:lo + MLA_QK_PAD] = yb.astype(k_ref.dtype)


def _pad_last(a, width):
    return jnp.pad(a, [(0, 0)] * (a.ndim - 1) + [(0, width - a.shape[-1])])


def _mla_prep(proj, col, g_cq, w_uq, g_ckv, w_ukv, g_q, g_k, cos, sin):
    t = proj.shape[0]
    q_rank, kv_rank = w_uq.shape[0], w_ukv.shape[0]
    hp = MLA_HEADS * MLA_QK_PAD
    w_q = _pad_last(w_uq.reshape(q_rank, MLA_HEADS, MLA_QK), MLA_QK_PAD).reshape(q_rank, hp).astype(BF16)
    w_kv = w_ukv.reshape(kv_rank, MLA_HEADS, MLA_NOPE + MLA_V)
    w_k = jnp.pad(w_kv[..., :MLA_NOPE], ((0, 0), (0, 0), (MLA_ROPE, MLA_QK_PAD - MLA_QK)))
    w_k = w_k.reshape(kv_rank, hp).astype(BF16)
    w_v = w_kv[..., MLA_NOPE:].reshape(kv_rank, MLA_HEADS * MLA_V).astype(BF16)
    g_qp = _pad_last(g_q, MLA_QK_PAD).reshape(1, MLA_QK_PAD)
    g_kp = _pad_last(g_k, MLA_QK_PAD).reshape(1, MLA_QK_PAD)

    bm = _block(t, 512)
    rows = lambda w, c: pl.BlockSpec((bm, w), lambda i, c=c: (i, c))
    full = lambda a: pl.BlockSpec(a.shape, lambda i: (0,) * a.ndim)
    table = pl.BlockSpec((1, bm, LANES), lambda i: (0, i, 0))
    g_cq2, g_ckv2 = g_cq.reshape(1, -1), g_ckv.reshape(1, -1)

    q = pl.pallas_call(
        _mla_q_kernel,
        grid=(t // bm,),
        in_specs=[rows(q_rank, col["c_q"] // q_rank), full(g_cq2), full(w_q), full(g_qp), table, table],
        out_specs=pl.BlockSpec((bm, hp), lambda i: (i, 0)),
        out_shape=jax.ShapeDtypeStruct((t, hp), BF16),
        compiler_params=_params("parallel"),
        name="mla_q_prep",
    )(proj, g_cq2, w_q, g_qp, cos, sin)

    k, v = pl.pallas_call(
        _mla_kv_kernel,
        grid=(t // bm,),
        in_specs=[rows(kv_rank, col["c_kv"] // kv_rank), rows(LANES, col["k_r"] // LANES), full(g_ckv2),
                  full(w_k), full(w_v), full(g_kp), table, table],
        out_specs=[pl.BlockSpec((bm, hp), lambda i: (i, 0)),
                   pl.BlockSpec((bm, MLA_HEADS * MLA_V), lambda i: (i, 0))],
        out_shape=[jax.ShapeDtypeStruct((t, hp), BF16), jax.ShapeDtypeStruct((t, MLA_HEADS * MLA_V), BF16)],
        compiler_params=_params("parallel"),
        name="mla_kv_prep",
    )(proj, proj, g_ckv2, w_k, w_v, g_kp, cos, sin)
    return q, k, v


ATTN_HEADS_PER_STEP = 2


def _flash_step(q, k, vt, visible, m_ref, l_ref, acc_ref, h, scale):
    s = _dot_nt(k, q) * scale
    if visible is not None:
        s = jnp.where(visible, s, NEG_INF)
    m_prev = m_ref[h]
    m_new = jnp.maximum(m_prev, jnp.max(s, axis=0, keepdims=True))
    m_safe = jnp.where(m_new == NEG_INF, 0.0, m_new)
    alpha = jnp.exp(m_prev - m_safe)
    p = jnp.exp(s - m_safe)
    l_ref[h] = alpha * l_ref[h] + jnp.sum(p, axis=0, keepdims=True)
    acc_ref[h] = alpha * acc_ref[h] + _dot(vt, p.astype(vt.dtype))
    m_ref[h] = m_new


def _attn_kernel(q_ref, k_ref, vt_ref, *rest, tq, dk, dv, scale, selected_keys):
    if selected_keys:
        sel_ref, o_ref, m_ref, l_ref, acc_ref = rest
    else:
        o_ref, m_ref, l_ref, acc_ref = rest
    i = pl.program_id(2)
    heads = range(ATTN_HEADS_PER_STEP)
    m_ref[...] = jnp.full(m_ref.shape, NEG_INF, F32)
    l_ref[...] = jnp.zeros(l_ref.shape, F32)
    acc_ref[...] = jnp.zeros(acc_ref.shape, F32)
    qs = [q_ref[:, h * dk:(h + 1) * dk] for h in heads]

    def key_block(j, visible_of):
        off = pl.multiple_of(j * tq, tq)
        visible = visible_of(off)
        for h in heads:
            _flash_step(qs[h], k_ref[pl.ds(off, tq), h * dk:(h + 1) * dk], vt_ref[h, :, pl.ds(off, tq)],
                        visible, m_ref, l_ref, acc_ref, h, scale)

    def loop_body(visible_of):
        def body(j, carry):
            key_block(j, visible_of)
            return carry
        return body

    if selected_keys:
        lax.fori_loop(0, i + 1, loop_body(lambda off: sel_ref[pl.ds(off, tq), :].astype(F32) > 0.0), 0)
    else:
        lax.fori_loop(0, i, loop_body(lambda off: None), 0)
        key_chunk = _chunk_of(lax.broadcasted_iota(jnp.int32, (tq, tq), 0))
        query_chunk = _chunk_of(lax.broadcasted_iota(jnp.int32, (tq, tq), 1))
        key_block(i, lambda off: key_chunk <= query_chunk)

    for h in heads:
        o = acc_ref[h] / l_ref[h]
        o_ref[:, h * dv:(h + 1) * dv] = o.T.astype(o_ref.dtype)


def _attention(q, k, vt, selected_t, batch, heads, dk, dv, scale, name, tq=256):
    t = q.shape[0]
    s = t // batch
    tq = _block(s, tq)
    nq = s // tq
    hp = ATTN_HEADS_PER_STEP
    in_specs = [pl.BlockSpec((tq, hp * dk), lambda b, h, i: (b * nq + i, h)),
                pl.BlockSpec((s, hp * dk), lambda b, h, i: (b, h)),
                pl.BlockSpec((None, hp, dv, s), lambda b, h, i: (b, h, 0, 0))]
    operands = [q, k, vt]
    if selected_t is not None:
        in_specs.append(pl.BlockSpec((None, s, tq), lambda b, h, i: (b, 0, i)))
        operands.append(selected_t)
    return pl.pallas_call(
        functools.partial(_attn_kernel, tq=tq, dk=dk, dv=dv, scale=scale, selected_keys=selected_t is not None),
        grid=(batch, heads // hp, nq),
        in_specs=in_specs,
        out_specs=pl.BlockSpec((tq, hp * dv), lambda b, h, i: (b * nq + i, h)),
        out_shape=jax.ShapeDtypeStruct((t, heads * dv), BF16),
        scratch_shapes=[pltpu.VMEM((hp, 1, tq), F32), pltpu.VMEM((hp, 1, tq), F32), pltpu.VMEM((hp, dv, tq), F32)],
        compiler_params=_params("parallel", "parallel", "parallel"),
        name=name,
    )(*operands)


def _heads_transposed(v, batch, heads):
    t, width = v.shape
    return v.reshape(batch, t // batch, heads, width // heads).transpose(0, 2, 3, 1)


def _head_norm_rope_kernel(x_ref, g_ref, cos_ref, sin_ref, o_ref, *, heads):
    cos, sin = cos_ref[0], sin_ref[0]
    g = g_ref[...]
    for h in range(heads):
        x = x_ref[:, h * LANES:(h + 1) * LANES]
        y = x * _rms_scale(x, LANES) * g
        o_ref[:, h * LANES:(h + 1) * LANES] = _rope_tile(y, cos, sin, DSA_ROT // 2, LANES).astype(o_ref.dtype)


def _head_norm_rope(proj, col_block, g, cos, sin):
    t = proj.shape[0]
    width = DSA_HEADS * DSA_HEAD_DIM
    bm = _block(t, 512)
    table = pl.BlockSpec((1, bm, LANES), lambda i: (1, i, 0))
    return pl.pallas_call(
        functools.partial(_head_norm_rope_kernel, heads=DSA_HEADS),
        grid=(t // bm,),
        in_specs=[pl.BlockSpec((bm, width), lambda i: (i, col_block)),
                  pl.BlockSpec((1, LANES), lambda i: (0, 0)), table, table],
        out_specs=pl.BlockSpec((bm, width), lambda i: (i, 0)),
        out_shape=jax.ShapeDtypeStruct((t, width), BF16),
        compiler_params=_params("parallel"),
        name="head_norm_rope",
    )(proj, g.reshape(1, LANES), cos, sin)


def _rope_only_kernel(x_ref, cos_ref, sin_ref, o_ref):
    cos, sin = cos_ref[0], sin_ref[0]
    for c in range(x_ref.shape[1] // LANES):
        x = x_ref[:, c * LANES:(c + 1) * LANES]
        o_ref[:, c * LANES:(c + 1) * LANES] = _rope_tile(x, cos, sin, IDX_ROT // 2, IDX_DIM).astype(o_ref.dtype)


def _rope_only(proj, width, col_block, cos, sin):
    t = proj.shape[0]
    bm = _block(t, 512)
    table = pl.BlockSpec((1, bm, LANES), lambda i: (2, i, 0))
    return pl.pallas_call(
        _rope_only_kernel,
        grid=(t // bm,),
        in_specs=[pl.BlockSpec((bm, width), lambda i: (i, col_block)), table, table],
        out_specs=pl.BlockSpec((bm, width), lambda i: (i, 0)),
        out_shape=jax.ShapeDtypeStruct((t, width), BF16),
        compiler_params=_params("parallel"),
        name="indexer_rope",
    )(proj, cos, sin)


def _sortable_key(x):
    bits = lax.bitcast_convert_type(x, jnp.int32)
    return bits ^ ((bits >> 31) & 0x7FFFFFFF)


def _indexer_kernel(qi_ref, kit_ref, w_ref, sel_ref, key_ref, wb_ref, *, tq, tk, topk, seq, w_scale):
    i = pl.program_id(1)
    lanes_per_tile = tk // LANES
    n_tiles = (i + 1) * (tq // tk)
    row_chunk = _chunk_of(i * tq + lax.broadcasted_iota(jnp.int32, (tq, 1), 0))

    w = w_ref[...] * w_scale
    for h in range(IDX_HEADS):
        wb_ref[h] = jnp.broadcast_to(w[:, h:h + 1], (tq, LANES))

    def score_tile(j, carry):
        off = pl.multiple_of(j * tk, tk)
        kt = kit_ref[:, pl.ds(off, tk)]
        acc = [jnp.zeros((tq, LANES), F32) for _ in range(lanes_per_tile)]
        for h in range(IDX_HEADS):
            logits = _dot(qi_ref[h], kt)
            wh = wb_ref[h]
            for c in range(lanes_per_tile):
                acc[c] = acc[c] + jnp.maximum(logits[:, c * LANES:(c + 1) * LANES], 0.0) * wh
        for c in range(lanes_per_tile):
            col = off + c * LANES + lax.broadcasted_iota(jnp.int32, (1, LANES), 1)
            score = jnp.where(_chunk_of(col) <= row_chunk, acc[c], NEG_INF)
            key_ref[:, pl.ds(off + c * LANES, LANES)] = _sortable_key(score)
        return carry

    lax.fori_loop(0, n_tiles, score_tile, 0)

    def count(pred):
        def tile(j, acc):
            off = pl.multiple_of(j * tk, tk)
            for c in range(lanes_per_tile):
                kt = key_ref[:, pl.ds(off + c * LANES, LANES)]
                acc = acc + jnp.where(pred(kt, off + c * LANES), 1.0, 0.0)
            return acc
        acc = lax.fori_loop(0, n_tiles, tile, jnp.zeros((tq, LANES), F32))
        return jnp.sum(acc, axis=-1, keepdims=True)

    def value_bit(b, lo):
        cand = lo + lax.shift_left(jnp.int32(1), 31 - b)
        cand_b = jnp.broadcast_to(cand, (tq, LANES))
        n_ge = count(lambda kt, _: kt >= cand_b)
        return jnp.where(n_ge >= topk, cand, lo)

    thr = lax.fori_loop(0, 32, value_bit, jnp.full((tq, 1), INT_MIN, jnp.int32))
    thr_b = jnp.broadcast_to(thr, (tq, LANES))
    n_ge = count(lambda kt, _: kt >= thr_b)
    surplus = jnp.logical_and(n_ge > topk, thr > NEG_INF_KEY)

    def write_tile(j, carry):
        off = pl.multiple_of(j * tk, tk)
        for c in range(lanes_per_tile):
            kt = key_ref[:, pl.ds(off + c * LANES, LANES)]
            chosen = jnp.logical_and(kt >= thr_b, kt > NEG_INF_KEY)
            sel_ref[:, pl.ds(off + c * LANES, LANES)] = jnp.where(chosen, 1.0, 0.0).astype(sel_ref.dtype)
        return carry

    lax.fori_loop(0, n_tiles, write_tile, 0)

    def clear_tile(j, carry):
        off = pl.multiple_of(j * tk, tk)
        sel_ref[:, pl.ds(off, tk)] = jnp.zeros((tq, tk), sel_ref.dtype)
        return carry

    lax.fori_loop(n_tiles, seq // tk, clear_tile, 0)

    @pl.when(jnp.max(jnp.where(surplus, 1.0, 0.0)) > 0.0)
    def _():
        n_gt = count(lambda kt, _: kt > thr_b)
        need = topk - n_gt

        def tied_before(kt, first_col, bound_b):
            col = first_col + lax.broadcasted_iota(jnp.int32, (tq, LANES), 1)
            return jnp.logical_and(kt == thr_b, col < bound_b)

        def position_bit(b, base):
            cand = base + lax.shift_left(jnp.int32(1), (seq - 1).bit_length() - 1 - b)
            cand_b = jnp.broadcast_to(cand, (tq, LANES))
            n_before = count(lambda kt, c0: tied_before(kt, c0, cand_b))
            return jnp.where(n_before < need, cand, base)

        last = lax.fori_loop(0, (seq - 1).bit_length(), position_bit, jnp.zeros((tq, 1), jnp.int32))
        last_b = jnp.broadcast_to(last, (tq, LANES))

        def rewrite_tile(j, carry):
            off = pl.multiple_of(j * tk, tk)
            for c in range(lanes_per_tile):
                c0 = off + c * LANES
                kt = key_ref[:, pl.ds(c0, LANES)]
                col = c0 + lax.broadcasted_iota(jnp.int32, (tq, LANES), 1)
                chosen = jnp.logical_or(kt > thr_b, jnp.logical_and(kt == thr_b, col <= last_b))
                chosen = jnp.logical_and(chosen, kt > NEG_INF_KEY)
                sel_ref[:, pl.ds(c0, LANES)] = jnp.where(chosen, 1.0, 0.0).astype(sel_ref.dtype)
            return carry

        lax.fori_loop(0, n_tiles, rewrite_tile, 0)


def _indexer_select(qi, kit, w, w_col_block, batch, topk, tq=128, tk=256):
    s = qi.shape[2]
    tq = _block(s, tq)
    tk = min(tk, tq)
    nq = s // tq
    w_scale = (IDX_DIM ** -0.5) * (IDX_HEADS ** -0.5)
    return pl.pallas_call(
        functools.partial(_indexer_kernel, tq=tq, tk=tk, topk=topk, seq=s, w_scale=w_scale),
        grid=(batch, nq),
        in_specs=[pl.BlockSpec((None, IDX_HEADS, tq, IDX_DIM), lambda b, i: (b, 0, i, 0)),
                  pl.BlockSpec((None, IDX_DIM, s), lambda b, i: (b, 0, 0)),
                  pl.BlockSpec((tq, LANES), lambda b, i: (b * nq + i, w_col_block))],
        out_specs=pl.BlockSpec((tq, s), lambda b, i: (b * nq + i, 0)),
        out_shape=jax.ShapeDtypeStruct((batch * s, s), BF16),
        scratch_shapes=[pltpu.VMEM((tq, s), jnp.int32), pltpu.VMEM((IDX_HEADS, tq, LANES), F32)],
        compiler_params=_params("parallel", "parallel"),
        name="indexer_select",
    )(qi, kit, w)


def _memory_attn_kernel(q_ref, kv_ref, gq_ref, gk_ref, o_ref, *, scale):
    hd = MEM_HEAD_DIM
    gq, gk = gq_ref[...], gk_ref[...]
    for h in range(MEM_HEADS):
        q = q_ref[:, h * hd:(h + 1) * hd]
        k = kv_ref[:, h * hd:(h + 1) * hd]
        v = kv_ref[:, (MEM_HEADS + h) * hd:(MEM_HEADS + h + 1) * hd].astype(BF16)
        qn = (q * _rms_scale(q, hd) * gq).astype(BF16)
        kn = (k * _rms_scale(k, hd) * gk).astype(BF16)
        s = _dot_nt(qn, kn) * scale
        e = jnp.exp(s - jnp.max(s, axis=-1, keepdims=True))
        p = e / jnp.sum(e, axis=-1, keepdims=True)
        o_ref[:, h * hd:(h + 1) * hd] = _dot(p.astype(BF16), v).astype(o_ref.dtype)


def _memory_attention(proj, col_block, kv, g_q, g_k, batch, tq=512):
    t = proj.shape[0]
    s = t // batch
    m = kv.shape[0] // batch
    width = MEM_HEADS * MEM_HEAD_DIM
    tq = _block(s, tq)
    nq = s // tq
    g_spec = pl.BlockSpec((1, MEM_HEAD_DIM), lambda b, i: (0, 0))
    return pl.pallas_call(
        functools.partial(_memory_attn_kernel, scale=MEM_HEAD_DIM ** -0.5),
        grid=(batch, nq),
        in_specs=[pl.BlockSpec((tq, width), lambda b, i: (b * nq + i, col_block)),
                  pl.BlockSpec((m, 2 * width), lambda b, i: (b, 0)), g_spec, g_spec],
        out_specs=pl.BlockSpec((tq, width), lambda b, i: (b * nq + i, 0)),
        out_shape=jax.ShapeDtypeStruct((t, width), BF16),
        compiler_params=_params("parallel", "parallel"),
        name="memory_attention",
    )(proj, kv, g_q.reshape(1, -1), g_k.reshape(1, -1))


def _projection_layout(q_rank, kv_rank):
    dsa = DSA_HEADS * DSA_HEAD_DIM
    sizes = [("c_q", q_rank), ("c_kv", kv_rank), ("k_r", MLA_ROPE), ("q_d", dsa), ("k_d", dsa), ("v_d", dsa),
             ("q_i", IDX_HEADS * IDX_DIM), ("k_i", IDX_DIM), ("w_i", IDX_HEADS), ("q_m", MEM_HEADS * MEM_HEAD_DIM)]
    src, start = {}, 0
    for name, n in sizes:
        src[name] = (start, n)
        start += n
    return src, start


def _hybrid_mixer(x, h, mem, positions, w_in, g_cq, w_uq, g_ckv, w_ukv, g_q_mla, g_k_mla, g_q_dsa, g_k_dsa,
                  g_mem, w_mem_kv, g_q_mem, g_k_mem, w_o_mla, w_o_dsa, w_o_mem, w_gate, w_out, batch):
    t, d = h.shape
    s = t // batch
    q_rank, kv_rank = w_uq.shape[0], w_ukv.shape[0]
    src, d_in = _projection_layout(q_rank, kv_rank)
    assert d_in == w_in.shape[1]
    dsa = DSA_HEADS * DSA_HEAD_DIM
    assert q_rank == dsa, "the reordered projection layout assumes equal widths for these groups"

    take = lambda name: w_in[:, src[name][0]:src[name][0] + src[name][1]]
    pieces = [("c_q", take("c_q")), ("q_d", take("q_d")), ("k_d", take("k_d")), ("v_d", take("v_d")),
              ("q_i", take("q_i")), ("q_m", take("q_m")), ("c_kv", take("c_kv")),
              ("k_r", jnp.concatenate([take("k_r"), take("k_i")], axis=1)),
              ("w_i", _pad_last(take("w_i"), LANES))]
    col, start = {}, 0
    for name, piece in pieces:
        assert start % piece.shape[1] == 0, name
        col[name] = start
        start += piece.shape[1]
    w_in_r = jnp.concatenate([p for _, p in pieces], axis=1).astype(BF16)

    proj = _matmul(h, w_in_r, F32, bm=1024, bn=768, name="input_projection")
    cos, sin = _rope_tables(positions)

    q_a, k_a, v_a = _mla_prep(proj, col, g_cq, w_uq, g_ckv, w_ukv, g_q_mla, g_k_mla, cos, sin)
    y_mla = _attention(q_a, k_a, _heads_transposed(v_a, batch, MLA_HEADS), None, batch, MLA_HEADS,
                       MLA_QK_PAD, MLA_V, MLA_QK ** -0.5, "chunk_causal_attention")

    q_b = _head_norm_rope(proj, col["q_d"] // dsa, g_q_dsa, cos, sin)
    k_b = _head_norm_rope(proj, col["k_d"] // dsa, g_k_dsa, cos, sin)
    v_b = proj[:, col["v_d"]:col["v_d"] + dsa].astype(BF16)
    qi_width = IDX_HEADS * IDX_DIM
    qi = _rope_only(proj, qi_width, col["q_i"] // qi_width, cos, sin)
    qi = qi.reshape(batch, s, IDX_HEADS, IDX_DIM).transpose(0, 2, 1, 3)
    ki = _rope_only(proj, LANES, col["k_r"] // LANES, cos, sin)[:, IDX_DIM:]
    kit = ki.reshape(batch, s, IDX_DIM).transpose(0, 2, 1)
    topk = min(DSA_MAX_TOPK, s // 4)
    selected = _indexer_select(qi, kit, proj, col["w_i"] // LANES, batch, topk)
    selected_t = selected.reshape(batch, s, s).transpose(0, 2, 1)
    y_dsa = _attention(q_b, k_b, _heads_transposed(v_b, batch, DSA_HEADS), selected_t, batch, DSA_HEADS,
                       DSA_HEAD_DIM, DSA_HEAD_DIM, DSA_HEAD_DIM ** -0.5, "masked_attention")

    m_len = mem.shape[1]
    mem_n = _rmsnorm(mem.reshape(batch * m_len, d), g_mem)
    kv_mem = _matmul(mem_n, w_mem_kv.astype(BF16), F32, bm=512, bn=512, name="memory_kv")
    y_mem = _memory_attention(proj, col["q_m"] // (MEM_HEADS * MEM_HEAD_DIM), kv_mem, g_q_mem, g_k_mem, batch)

    gated = _gate_combine(h, (y_mla, y_dsa, y_mem), w_gate.astype(BF16),
                          (w_o_mla.astype(BF16), w_o_dsa.astype(BF16), w_o_mem.astype(BF16)))
    return _matmul_residual(gated, w_out.astype(BF16), x, 1.0, name="mixer_out")


def kernel(x, mem, positions, g_ff1, w_ff1_gate, w_ff1_up, w_ff1_down, g_mix, w_in, g_cq, w_uq, g_ckv, w_ukv, g_q_mla, g_k_mla, g_q_dsa, g_k_dsa, g_mem, w_mem_kv, g_q_mem, g_k_mem, w_o_mla, w_o_dsa, w_o_mem, w_gate, w_out, g_ff2, w_ff2_gate, w_ff2_up, w_ff2_down):
    batch, seq, d = x.shape
    xt = x.reshape(batch * seq, d)
    for l in range(g_ff1.shape[0]):
        xt = _ffn_half_step(xt, g_ff1[l], w_ff1_gate[l], w_ff1_up[l], w_ff1_down[l])
        h = _rmsnorm(xt, g_mix[l])
        xt = _hybrid_mixer(xt, h, mem, positions, w_in[l], g_cq[l], w_uq[l], g_ckv[l], w_ukv[l],
                           g_q_mla[l], g_k_mla[l], g_q_dsa[l], g_k_dsa[l], g_mem[l], w_mem_kv[l],
                           g_q_mem[l], g_k_mem[l], w_o_mla[l], w_o_dsa[l], w_o_mem[l], w_gate[l], w_out[l], batch)
        xt = _ffn_half_step(xt, g_ff2[l], w_ff2_gate[l], w_ff2_up[l], w_ff2_down[l])
    return xt.reshape(batch, seq, d)
```

```python
import functools

import jax
import jax.numpy as jnp
from jax import lax
from jax.experimental import pallas as pl
from jax.experimental.pallas import tpu as pltpu

CHUNK = 64
ROPE_THETA = 500000.0
EPS = 1e-6

MLA_HEADS = 12
MLA_NOPE = 128
MLA_ROPE = 64
MLA_V = 128
MLA_QK = MLA_ROPE + MLA_NOPE
MLA_QK_PAD = 256

DSA_HEADS = 12
DSA_HEAD_DIM = 128
DSA_ROT = DSA_HEAD_DIM // 4
DSA_MAX_TOPK = 256
IDX_HEADS = 32
IDX_DIM = 64
IDX_ROT = IDX_DIM // 4

MEM_HEADS = 4
MEM_HEAD_DIM = 256

LANES = 128
VMEM_LIMIT_BYTES = 56 * 1024 * 1024

F32 = jnp.float32
BF16 = jnp.bfloat16
NEG_INF = float("-inf")
INT_MIN = -2 ** 31
NEG_INF_KEY = (0xFF800000 - (1 << 32)) ^ 0x7FFFFFFF


def _params(*semantics):
    return pltpu.CompilerParams(dimension_semantics=semantics, vmem_limit_bytes=VMEM_LIMIT_BYTES)


def _block(dim, preferred):
    if dim <= preferred:
        return dim
    b = preferred - preferred % LANES
    while b >= LANES:
        if dim % b == 0:
            return b
        b -= LANES
    return dim


def _rms_scale(x, width):
    return lax.rsqrt(jnp.sum(x * x, axis=-1, keepdims=True) * (1.0 / width) + EPS)


def _dot(a, b):
    return jnp.dot(a, b, preferred_element_type=F32)


def _dot_nt(a, b):
    return lax.dot_general(a, b, (((1,), (1,)), ((), ())), preferred_element_type=F32)


def _sigmoid(x):
    return 1.0 / (1.0 + jnp.exp(-x))


def _chunk_of(frame):
    return lax.shift_right_logical(frame, CHUNK.bit_length() - 1)


def _rmsnorm_kernel(x_ref, g_ref, o_ref):
    x = x_ref[...]
    o_ref[...] = (x * _rms_scale(x, x.shape[-1]) * g_ref[...]).astype(o_ref.dtype)


def _rmsnorm(x, g, out_dtype=BF16):
    m, d = x.shape
    bm = _block(m, 512)
    return pl.pallas_call(
        _rmsnorm_kernel,
        grid=(m // bm,),
        in_specs=[pl.BlockSpec((bm, d), lambda i: (i, 0)), pl.BlockSpec((1, d), lambda i: (0, 0))],
        out_specs=pl.BlockSpec((bm, d), lambda i: (i, 0)),
        out_shape=jax.ShapeDtypeStruct((m, d), out_dtype),
        compiler_params=_params("parallel"),
        name="rmsnorm",
    )(x, g.reshape(1, d))


def _matmul_kernel(a_ref, b_ref, o_ref):
    o_ref[...] = _dot(a_ref[...], b_ref[...]).astype(o_ref.dtype)


def _matmul(a, b, out_dtype, bm=1024, bn=512, name="matmul"):
    m, k = a.shape
    n = b.shape[1]
    bm, bn = _block(m, bm), _block(n, bn)
    return pl.pallas_call(
        _matmul_kernel,
        grid=(m // bm, n // bn),
        in_specs=[pl.BlockSpec((bm, k), lambda i, j: (i, 0)), pl.BlockSpec((k, bn), lambda i, j: (0, j))],
        out_specs=pl.BlockSpec((bm, bn), lambda i, j: (i, j)),
        out_shape=jax.ShapeDtypeStruct((m, n), out_dtype),
        compiler_params=_params("parallel", "parallel"),
        name=name,
    )(a, b)


def _swiglu_up_kernel(a_ref, wg_ref, wu_ref, o_ref):
    a = a_ref[...]
    g = _dot(a, wg_ref[...])
    u = _dot(a, wu_ref[...])
    o_ref[...] = (g * _sigmoid(g) * u).astype(o_ref.dtype)


def _swiglu_up(a, wg, wu, bm=1024, bn=512):
    m, k = a.shape
    n = wg.shape[1]
    bm, bn = _block(m, bm), _block(n, bn)
    return pl.pallas_call(
        _swiglu_up_kernel,
        grid=(m // bm, n // bn),
        in_specs=[pl.BlockSpec((bm, k), lambda i, j: (i, 0)),
                  pl.BlockSpec((k, bn), lambda i, j: (0, j)),
                  pl.BlockSpec((k, bn), lambda i, j: (0, j))],
        out_specs=pl.BlockSpec((bm, bn), lambda i, j: (i, j)),
        out_shape=jax.ShapeDtypeStruct((m, n), BF16),
        compiler_params=_params("parallel", "parallel"),
        name="swiglu_up",
    )(a, wg, wu)


def _matmul_residual_kernel(a_ref, b_ref, x_ref, o_ref, *, scale):
    o_ref[...] = x_ref[...] + scale * _dot(a_ref[...], b_ref[...])


def _matmul_residual(a, b, x, scale, bm=512, bn=512, name="matmul_residual"):
    m, k = a.shape
    n = b.shape[1]
    bm, bn = _block(m, bm), _block(n, bn)
    return pl.pallas_call(
        functools.partial(_matmul_residual_kernel, scale=scale),
        grid=(m // bm, n // bn),
        in_specs=[pl.BlockSpec((bm, k), lambda i, j: (i, 0)),
                  pl.BlockSpec((k, bn), lambda i, j: (0, j)),
                  pl.BlockSpec((bm, bn), lambda i, j: (i, j))],
        out_specs=pl.BlockSpec((bm, bn), lambda i, j: (i, j)),
        out_shape=jax.ShapeDtypeStruct((m, n), F32),
        compiler_params=_params("parallel", "parallel"),
        name=name,
    )(a, b, x)


def _ffn_half_step(x, g, w_gate, w_up, w_down):
    h = _rmsnorm(x, g)
    a = _swiglu_up(h, w_gate.astype(BF16), w_up.astype(BF16))
    return _matmul_residual(a, w_down.astype(BF16), x, 0.5, name="ffn_down")


def _gate_combine_kernel(h_ref, ya_ref, yb_ref, yc_ref, wg_ref, wa_ref, wb_ref, wc_ref, o_ref):
    h = h_ref[...]
    out = _sigmoid(_dot(h, wg_ref[0])) * _dot(ya_ref[...], wa_ref[...])
    out = out + _sigmoid(_dot(h, wg_ref[1])) * _dot(yb_ref[...], wb_ref[...])
    out = out + _sigmoid(_dot(h, wg_ref[2])) * _dot(yc_ref[...], wc_ref[...])
    o_ref[...] = out.astype(o_ref.dtype)


def _gate_combine(h, ys, w_gate, w_os, bm=512, bn=512):
    m, d = h.shape
    n = w_gate.shape[-1]
    bm, bn = _block(m, bm), _block(n, bn)
    row = lambda w: pl.BlockSpec((bm, w), lambda i, j: (i, 0))
    col = lambda k: pl.BlockSpec((k, bn), lambda i, j: (0, j))
    return pl.pallas_call(
        _gate_combine_kernel,
        grid=(m // bm, n // bn),
        in_specs=[row(d)] + [row(y.shape[1]) for y in ys]
                 + [pl.BlockSpec((3, d, bn), lambda i, j: (0, 0, j))] + [col(w.shape[0]) for w in w_os],
        out_specs=pl.BlockSpec((bm, bn), lambda i, j: (i, j)),
        out_shape=jax.ShapeDtypeStruct((m, n), BF16),
        compiler_params=_params("parallel", "parallel"),
        name="gate_combine",
    )(h, *ys, w_gate, *w_os)


def _rope_tables_kernel(pos_ref, invf_ref, sgn_ref, cos_ref, sin_ref):
    pos = pos_ref[...].astype(F32)
    for t in range(invf_ref.shape[0]):
        ang = pos * invf_ref[t:t + 1, :]
        cos_ref[t] = jnp.cos(ang)
        sin_ref[t] = jnp.sin(ang) * sgn_ref[t:t + 1, :]


def _rope_lane_tables(rot, period):
    half = rot // 2
    inv_freq = ROPE_THETA ** (-jnp.arange(half, dtype=F32) / half)
    lane = jnp.arange(LANES) % period
    invf = jnp.where(lane < rot, inv_freq[lane % half], 0.0)
    sgn = jnp.where(lane < half, -1.0, jnp.where(lane < rot, 1.0, 0.0))
    return invf.astype(F32), sgn.astype(F32)


def _rope_tables(positions):
    t = positions.size
    specs = [_rope_lane_tables(MLA_ROPE, LANES), _rope_lane_tables(DSA_ROT, LANES),
             _rope_lane_tables(IDX_ROT, IDX_DIM)]
    invf = jnp.stack([s[0] for s in specs])
    sgn = jnp.stack([s[1] for s in specs])
    bm = _block(t, 1024)
    n = invf.shape[0]
    const = pl.BlockSpec((n, LANES), lambda i: (0, 0))
    out = pl.BlockSpec((n, bm, LANES), lambda i: (0, i, 0))
    return pl.pallas_call(
        _rope_tables_kernel,
        grid=(t // bm,),
        in_specs=[pl.BlockSpec((bm, 1), lambda i: (i, 0)), const, const],
        out_specs=[out, out],
        out_shape=[jax.ShapeDtypeStruct((n, t, LANES), F32)] * 2,
        compiler_params=_params("parallel"),
        name="rope_tables",
    )(positions.reshape(t, 1), invf, sgn)


def _rope_tile(x, cos, sin, half, period):
    lane = lax.broadcasted_iota(jnp.int32, x.shape, 1) & (period - 1)
    partner = jnp.where(lane < half, pltpu.roll(x, LANES - half, 1), pltpu.roll(x, half, 1))
    return x * cos + partner * sin


def _mla_q_kernel(cq_ref, gcq_ref, w_ref, gq_ref, cos_ref, sin_ref, o_ref):
    x = cq_ref[...]
    xn = (x * _rms_scale(x, x.shape[-1]) * gcq_ref[...]).astype(BF16)
    q = _dot(xn, w_ref[...])
    cos, sin = cos_ref[0], sin_ref[0]
    g = gq_ref[...]
    for h in range(MLA_HEADS):
        lo = h * MLA_QK_PAD
        qh = q[:, lo:lo + MLA_QK_PAD]
        y = qh * _rms_scale(qh, MLA_QK) * g
        o_ref[:, lo:lo + LANES] = _rope_tile(y[:, :LANES], cos, sin, MLA_ROPE // 2, LANES).astype(o_ref.dtype)
        o_ref[:, lo + LANES:lo + MLA_QK_PAD] = y[:, LANES:].astype(o_ref.dtype)


def _mla_kv_kernel(ckv_ref, kr_ref, gckv_ref, wk_ref, wv_ref, gk_ref, cos_ref, sin_ref, k_ref, v_ref):
    x = ckv_ref[...]
    xn = (x * _rms_scale(x, x.shape[-1]) * gckv_ref[...]).astype(BF16)
    k_nope = _dot(xn, wk_ref[...])
    v_ref[...] = _dot(xn, wv_ref[...]).astype(v_ref.dtype)
    kr = kr_ref[...]
    lane = lax.broadcasted_iota(jnp.int32, kr.shape, 1)
    kr = jnp.where(lane < MLA_ROPE, kr, 0.0)
    cos, sin = cos_ref[0], sin_ref[0]
    g = gk_ref[...]
    for h in range(MLA_HEADS):
        lo = h * MLA_QK_PAD
        ka = k_nope[:, lo:lo + LANES] + kr
        kb = k_nope[:, lo + LANES:lo + MLA_QK_PAD]
        ms = (jnp.sum(ka * ka, axis=-1, keepdims=True) + jnp.sum(kb * kb, axis=-1, keepdims=True)) * (1.0 / MLA_QK)
        r = lax.rsqrt(ms + EPS)
        ya = ka * r * g[:, :LANES]
        yb = kb * r * g[:, LANES:]
        k_ref[:, lo:lo + LANES] = _rope_tile(ya, cos, sin, MLA_ROPE // 2, LANES).astype(k_ref.dtype)
        k_ref[:, lo + LANES:lo + MLA_QK_PAD] = yb.astype(k_ref.dtype)


def _pad_last(a, width):
    return jnp.pad(a, [(0, 0)] * (a.ndim - 1) + [(0, width - a.shape[-1])])


def _mla_prep(proj, col, g_cq, w_uq, g_ckv, w_ukv, g_q, g_k, cos, sin):
    t = proj.shape[0]
    q_rank, kv_rank = w_uq.shape[0], w_ukv.shape[0]
    hp = MLA_HEADS * MLA_QK_PAD
    w_q = _pad_last(w_uq.reshape(q_rank, MLA_HEADS, MLA_QK), MLA_QK_PAD).reshape(q_rank, hp).astype(BF16)
    w_kv = w_ukv.reshape(kv_rank, MLA_HEADS, MLA_NOPE + MLA_V)
    w_k = jnp.pad(w_kv[..., :MLA_NOPE], ((0, 0), (0, 0), (MLA_ROPE, MLA_QK_PAD - MLA_QK)))
    w_k = w_k.reshape(kv_rank, hp).astype(BF16)
    w_v = w_kv[..., MLA_NOPE:].reshape(kv_rank, MLA_HEADS * MLA_V).astype(BF16)
    g_qp = _pad_last(g_q, MLA_QK_PAD).reshape(1, MLA_QK_PAD)
    g_kp = _pad_last(g_k, MLA_QK_PAD).reshape(1, MLA_QK_PAD)

    bm = _block(t, 512)
    rows = lambda w, c: pl.BlockSpec((bm, w), lambda i, c=c: (i, c))
    full = lambda a: pl.BlockSpec(a.shape, lambda i: (0,) * a.ndim)
    table = pl.BlockSpec((1, bm, LANES), lambda i: (0, i, 0))
    g_cq2, g_ckv2 = g_cq.reshape(1, -1), g_ckv.reshape(1, -1)

    q = pl.pallas_call(
        _mla_q_kernel,
        grid=(t // bm,),
        in_specs=[rows(q_rank, col["c_q"] // q_rank), full(g_cq2), full(w_q), full(g_qp), table, table],
        out_specs=pl.BlockSpec((bm, hp), lambda i: (i, 0)),
        out_shape=jax.ShapeDtypeStruct((t, hp), BF16),
        compiler_params=_params("parallel"),
        name="mla_q_prep",
    )(proj, g_cq2, w_q, g_qp, cos, sin)

    k, v = pl.pallas_call(
        _mla_kv_kernel,
        grid=(t // bm,),
        in_specs=[rows(kv_rank, col["c_kv"] // kv_rank), rows(LANES, col["k_r"] // LANES), full(g_ckv2),
                  full(w_k), full(w_v), full(g_kp), table, table],
        out_specs=[pl.BlockSpec((bm, hp), lambda i: (i, 0)),
                   pl.BlockSpec((bm, MLA_HEADS * MLA_V), lambda i: (i, 0))],
        out_shape=[jax.ShapeDtypeStruct((t, hp), BF16), jax.ShapeDtypeStruct((t, MLA_HEADS * MLA_V), BF16)],
        compiler_params=_params("parallel"),
        name="mla_kv_prep",
    )(proj, proj, g_ckv2, w_k, w_v, g_kp, cos, sin)
    return q, k, v


ATTN_HEADS_PER_STEP = 2


def _attn_scratch(tq, dv):
    per_head = [pltpu.VMEM((1, tq), F32), pltpu.VMEM((1, tq), F32), pltpu.VMEM((dv, tq), F32)]
    per_parity = [pltpu.VMEM((tq, tq), F32), pltpu.VMEM((1, tq), F32),
                  pltpu.VMEM((tq, tq), BF16), pltpu.VMEM((1, tq), F32)]
    return (per_head + 2 * per_parity) * ATTN_HEADS_PER_STEP


def _attn_kernel(q_ref, k_ref, vt_ref, *rest, tq, dk, dv, scale, selected_keys):
    n_scratch = 11 * ATTN_HEADS_PER_STEP
    scratch = rest[-n_scratch:]
    sel_ref = rest[0] if selected_keys else None
    o_ref = rest[-n_scratch - 1]
    i = pl.program_id(2)
    heads = range(ATTN_HEADS_PER_STEP)
    m_of, l_of, acc_of, s_of, mx_of, p_of, al_of = [], [], [], [], [], [], []
    for h in heads:
        refs = scratch[11 * h:11 * (h + 1)]
        m_of.append(refs[0]); l_of.append(refs[1]); acc_of.append(refs[2])
        s_of.append((refs[3], refs[7])); mx_of.append((refs[4], refs[8]))
        p_of.append((refs[5], refs[9])); al_of.append((refs[6], refs[10]))
    qs = [q_ref[:, h * dk:(h + 1) * dk] for h in heads]

    for h in heads:
        m_of[h][...] = jnp.full((1, tq), NEG_INF, F32)
        l_of[h][...] = jnp.zeros((1, tq), F32)
        acc_of[h][...] = jnp.zeros((dv, tq), F32)
        s_of[h][1][...] = jnp.full((tq, tq), NEG_INF, F32)
        mx_of[h][1][...] = jnp.full((1, tq), NEG_INF, F32)
        p_of[h][0][...] = jnp.zeros((tq, tq), BF16)
        al_of[h][0][...] = jnp.zeros((1, tq), F32)

    def scores(block, par, diagonal):
        off = pl.multiple_of(block * tq, tq)
        if selected_keys:
            visible = sel_ref[pl.ds(off, tq), :].astype(F32) > 0.0
        elif diagonal:
            key_chunk = _chunk_of(lax.broadcasted_iota(jnp.int32, (tq, tq), 0))
            query_chunk = _chunk_of(lax.broadcasted_iota(jnp.int32, (tq, tq), 1))
            visible = key_chunk <= query_chunk
        else:
            visible = None
        for h in heads:
            s = _dot_nt(k_ref[pl.ds(off, tq), h * dk:(h + 1) * dk], qs[h]) * scale
            if visible is not None:
                s = jnp.where(visible, s, NEG_INF)
            s_of[h][par][...] = s
            mx_of[h][par][...] = jnp.max(s, axis=0, keepdims=True)

    def softmax(par):
        for h in heads:
            m_prev = m_of[h][...]
            m_new = jnp.maximum(m_prev, mx_of[h][par][...])
            m_safe = jnp.where(m_new == NEG_INF, 0.0, m_new)
            alpha = jnp.exp(m_prev - m_safe)
            p = jnp.exp(s_of[h][par][...] - m_safe)
            l_of[h][...] = alpha * l_of[h][...] + jnp.sum(p, axis=0, keepdims=True)
            p_of[h][par][...] = p.astype(BF16)
            al_of[h][par][...] = alpha
            m_of[h][...] = m_new

    def values(block, par):
        off = pl.multiple_of(jnp.maximum(block, 0) * tq, tq)
        for h in heads:
            acc_of[h][...] = al_of[h][par][...] * acc_of[h][...] + _dot(vt_ref[h, :, pl.ds(off, tq)],
                                                                         p_of[h][par][...])

    def step(t, par, diagonal=False):
        values(t - 2, par)
        softmax(1 - par)
        scores(t, par, diagonal)

    def pair(u, carry):
        step(2 * u, 0)
        step(2 * u + 1, 1)
        return carry

    lax.fori_loop(0, lax.shift_right_logical(i, 1), pair, 0)

    def drain(par):
        step(i, par, diagonal=True)
        values(i - 1, 1 - par)
        softmax(par)
        values(i, par)

    odd = (i & 1) == 1

    @pl.when(odd)
    def _():
        step(i - 1, 0)
        drain(1)

    @pl.when(jnp.logical_not(odd))
    def _():
        drain(0)

    for h in heads:
        o = acc_of[h][...] / l_of[h][...]
        o_ref[:, h * dv:(h + 1) * dv] = o.T.astype(o_ref.dtype)


def _attention(q, k, vt, selected_t, batch, heads, dk, dv, scale, name, tq=256):
    t = q.shape[0]
    s = t // batch
    tq = _block(s, tq)
    nq = s // tq
    hp = ATTN_HEADS_PER_STEP
    in_specs = [pl.BlockSpec((tq, hp * dk), lambda b, h, i: (b * nq + i, h)),
                pl.BlockSpec((s, hp * dk), lambda b, h, i: (b, h)),
                pl.BlockSpec((None, hp, dv, s), lambda b, h, i: (b, h, 0, 0))]
    operands = [q, k, vt]
    if selected_t is not None:
        in_specs.append(pl.BlockSpec((None, s, tq), lambda b, h, i: (b, 0, i)))
        operands.append(selected_t)
    return pl.pallas_call(
        functools.partial(_attn_kernel, tq=tq, dk=dk, dv=dv, scale=scale, selected_keys=selected_t is not None),
        grid=(batch, heads // hp, nq),
        in_specs=in_specs,
        out_specs=pl.BlockSpec((tq, hp * dv), lambda b, h, i: (b * nq + i, h)),
        out_shape=jax.ShapeDtypeStruct((t, heads * dv), BF16),
        scratch_shapes=_attn_scratch(tq, dv),
        compiler_params=_params("parallel", "parallel", "parallel"),
        name=name,
    )(*operands)


def _heads_transposed(v, batch, heads):
    t, width = v.shape
    return v.reshape(batch, t // batch, heads, width // heads).transpose(0, 2, 3, 1)


def _head_norm_rope_kernel(x_ref, g_ref, cos_ref, sin_ref, o_ref, *, heads):
    cos, sin = cos_ref[0], sin_ref[0]
    g = g_ref[...]
    for h in range(heads):
        x = x_ref[:, h * LANES:(h + 1) * LANES]
        y = x * _rms_scale(x, LANES) * g
        o_ref[:, h * LANES:(h + 1) * LANES] = _rope_tile(y, cos, sin, DSA_ROT // 2, LANES).astype(o_ref.dtype)


def _head_norm_rope(proj, col_block, g, cos, sin):
    t = proj.shape[0]
    width = DSA_HEADS * DSA_HEAD_DIM
    bm = _block(t, 512)
    table = pl.BlockSpec((1, bm, LANES), lambda i: (1, i, 0))
    return pl.pallas_call(
        functools.partial(_head_norm_rope_kernel, heads=DSA_HEADS),
        grid=(t // bm,),
        in_specs=[pl.BlockSpec((bm, width), lambda i: (i, col_block)),
                  pl.BlockSpec((1, LANES), lambda i: (0, 0)), table, table],
        out_specs=pl.BlockSpec((bm, width), lambda i: (i, 0)),
        out_shape=jax.ShapeDtypeStruct((t, width), BF16),
        compiler_params=_params("parallel"),
        name="head_norm_rope",
    )(proj, g.reshape(1, LANES), cos, sin)


def _rope_only_kernel(x_ref, cos_ref, sin_ref, o_ref):
    cos, sin = cos_ref[0], sin_ref[0]
    for c in range(x_ref.shape[1] // LANES):
        x = x_ref[:, c * LANES:(c + 1) * LANES]
        o_ref[:, c * LANES:(c + 1) * LANES] = _rope_tile(x, cos, sin, IDX_ROT // 2, IDX_DIM).astype(o_ref.dtype)


def _rope_only(proj, width, col_block, cos, sin):
    t = proj.shape[0]
    bm = _block(t, 512)
    table = pl.BlockSpec((1, bm, LANES), lambda i: (2, i, 0))
    return pl.pallas_call(
        _rope_only_kernel,
        grid=(t // bm,),
        in_specs=[pl.BlockSpec((bm, width), lambda i: (i, col_block)), table, table],
        out_specs=pl.BlockSpec((bm, width), lambda i: (i, 0)),
        out_shape=jax.ShapeDtypeStruct((t, width), BF16),
        compiler_params=_params("parallel"),
        name="indexer_rope",
    )(proj, cos, sin)


def _sortable_key(x):
    bits = lax.bitcast_convert_type(x, jnp.int32)
    return bits ^ ((bits >> 31) & 0x7FFFFFFF)


def _indexer_kernel(qi_ref, kit_ref, w_ref, sel_ref, key_ref, wb_ref, *, tq, tk, topk, seq, w_scale):
    i = pl.program_id(1)
    lanes_per_tile = tk // LANES
    n_tiles = (i + 1) * (tq // tk)
    row_chunk = _chunk_of(i * tq + lax.broadcasted_iota(jnp.int32, (tq, 1), 0))

    w = w_ref[...] * w_scale
    for h in range(IDX_HEADS):
        wb_ref[h] = jnp.broadcast_to(w[:, h:h + 1], (tq, LANES))

    def score_tile(j, carry):
        off = pl.multiple_of(j * tk, tk)
        kt = kit_ref[:, pl.ds(off, tk)]
        acc = [jnp.zeros((tq, LANES), F32) for _ in range(lanes_per_tile)]
        for h in range(IDX_HEADS):
            logits = _dot(qi_ref[h], kt)
            wh = wb_ref[h]
            for c in range(lanes_per_tile):
                acc[c] = acc[c] + jnp.maximum(logits[:, c * LANES:(c + 1) * LANES], 0.0) * wh
        for c in range(lanes_per_tile):
            col = off + c * LANES + lax.broadcasted_iota(jnp.int32, (1, LANES), 1)
            score = jnp.where(_chunk_of(col) <= row_chunk, acc[c], NEG_INF)
            key_ref[:, pl.ds(off + c * LANES, LANES)] = _sortable_key(score)
        return carry

    lax.fori_loop(0, n_tiles, score_tile, 0)

    def count(pred):
        def tile(j, acc):
            off = pl.multiple_of(j * tk, tk)
            for c in range(lanes_per_tile):
                kt = key_ref[:, pl.ds(off + c * LANES, LANES)]
                acc = acc + jnp.where(pred(kt, off + c * LANES), 1.0, 0.0)
            return acc
        acc = lax.fori_loop(0, n_tiles, tile, jnp.zeros((tq, LANES), F32))
        return jnp.sum(acc, axis=-1, keepdims=True)

    def value_bit(b, lo):
        cand = lo + lax.shift_left(jnp.int32(1), 31 - b)
        cand_b = jnp.broadcast_to(cand, (tq, LANES))
        n_ge = count(lambda kt, _: kt >= cand_b)
        return jnp.where(n_ge >= topk, cand, lo)

    thr = lax.fori_loop(0, 32, value_bit, jnp.full((tq, 1), INT_MIN, jnp.int32))
    thr_b = jnp.broadcast_to(thr, (tq, LANES))
    n_ge = count(lambda kt, _: kt >= thr_b)
    surplus = jnp.logical_and(n_ge > topk, thr > NEG_INF_KEY)

    def write_tile(j, carry):
        off = pl.multiple_of(j * tk, tk)
        for c in range(lanes_per_tile):
            kt = key_ref[:, pl.ds(off + c * LANES, LANES)]
            chosen = jnp.logical_and(kt >= thr_b, kt > NEG_INF_KEY)
            sel_ref[:, pl.ds(off + c * LANES, LANES)] = jnp.where(chosen, 1.0, 0.0).astype(sel_ref.dtype)
        return carry

    lax.fori_loop(0, n_tiles, write_tile, 0)

    def clear_tile(j, carry):
        off = pl.multiple_of(j * tk, tk)
        sel_ref[:, pl.ds(off, tk)] = jnp.zeros((tq, tk), sel_ref.dtype)
        return carry

    lax.fori_loop(n_tiles, seq // tk, clear_tile, 0)

    @pl.when(jnp.max(jnp.where(surplus, 1.0, 0.0)) > 0.0)
    def _():
        n_gt = count(lambda kt, _: kt > thr_b)
        need = topk - n_gt

        def tied_before(kt, first_col, bound_b):
            col = first_col + lax.broadcasted_iota(jnp.int32, (tq, LANES), 1)
            return jnp.logical_and(kt == thr_b, col < bound_b)

        def position_bit(b, base):
            cand = base + lax.shift_left(jnp.int32(1), (seq - 1).bit_length() - 1 - b)
            cand_b = jnp.broadcast_to(cand, (tq, LANES))
            n_before = count(lambda kt, c0: tied_before(kt, c0, cand_b))
            return jnp.where(n_before < need, cand, base)

        last = lax.fori_loop(0, (seq - 1).bit_length(), position_bit, jnp.zeros((tq, 1), jnp.int32))
        last_b = jnp.broadcast_to(last, (tq, LANES))

        def rewrite_tile(j, carry):
            off = pl.multiple_of(j * tk, tk)
            for c in range(lanes_per_tile):
                c0 = off + c * LANES
                kt = key_ref[:, pl.ds(c0, LANES)]
                col = c0 + lax.broadcasted_iota(jnp.int32, (tq, LANES), 1)
                chosen = jnp.logical_or(kt > thr_b, jnp.logical_and(kt == thr_b, col <= last_b))
                chosen = jnp.logical_and(chosen, kt > NEG_INF_KEY)
                sel_ref[:, pl.ds(c0, LANES)] = jnp.where(chosen, 1.0, 0.0).astype(sel_ref.dtype)
            return carry

        lax.fori_loop(0, n_tiles, rewrite_tile, 0)


def _indexer_select(qi, kit, w, w_col_block, batch, topk, tq=128, tk=256):
    s = qi.shape[2]
    tq = _block(s, tq)
    tk = min(tk, tq)
    nq = s // tq
    w_scale = (IDX_DIM ** -0.5) * (IDX_HEADS ** -0.5)
    return pl.pallas_call(
        functools.partial(_indexer_kernel, tq=tq, tk=tk, topk=topk, seq=s, w_scale=w_scale),
        grid=(batch, nq),
        in_specs=[pl.BlockSpec((None, IDX_HEADS, tq, IDX_DIM), lambda b, i: (b, 0, i, 0)),
                  pl.BlockSpec((None, IDX_DIM, s), lambda b, i: (b, 0, 0)),
                  pl.BlockSpec((tq, LANES), lambda b, i: (b * nq + i, w_col_block))],
        out_specs=pl.BlockSpec((tq, s), lambda b, i: (b * nq + i, 0)),
        out_shape=jax.ShapeDtypeStruct((batch * s, s), BF16),
        scratch_shapes=[pltpu.VMEM((tq, s), jnp.int32), pltpu.VMEM((IDX_HEADS, tq, LANES), F32)],
        compiler_params=_params("parallel", "parallel"),
        name="indexer_select",
    )(qi, kit, w)


def _memory_attn_kernel(q_ref, kv_ref, gq_ref, gk_ref, o_ref, *, scale):
    hd = MEM_HEAD_DIM
    gq, gk = gq_ref[...], gk_ref[...]
    for h in range(MEM_HEADS):
        q = q_ref[:, h * hd:(h + 1) * hd]
        k = kv_ref[:, h * hd:(h + 1) * hd]
        v = kv_ref[:, (MEM_HEADS + h) * hd:(MEM_HEADS + h + 1) * hd].astype(BF16)
        qn = (q * _rms_scale(q, hd) * gq).astype(BF16)
        kn = (k * _rms_scale(k, hd) * gk).astype(BF16)
        s = _dot_nt(qn, kn) * scale
        e = jnp.exp(s - jnp.max(s, axis=-1, keepdims=True))
        p = e / jnp.sum(e, axis=-1, keepdims=True)
        o_ref[:, h * hd:(h + 1) * hd] = _dot(p.astype(BF16), v).astype(o_ref.dtype)


def _memory_attention(proj, col_block, kv, g_q, g_k, batch, tq=512):
    t = proj.shape[0]
    s = t // batch
    m = kv.shape[0] // batch
    width = MEM_HEADS * MEM_HEAD_DIM
    tq = _block(s, tq)
    nq = s // tq
    g_spec = pl.BlockSpec((1, MEM_HEAD_DIM), lambda b, i: (0, 0))
    return pl.pallas_call(
        functools.partial(_memory_attn_kernel, scale=MEM_HEAD_DIM ** -0.5),
        grid=(batch, nq),
        in_specs=[pl.BlockSpec((tq, width), lambda b, i: (b * nq + i, col_block)),
                  pl.BlockSpec((m, 2 * width), lambda b, i: (b, 0)), g_spec, g_spec],
        out_specs=pl.BlockSpec((tq, width), lambda b, i: (b * nq + i, 0)),
        out_shape=jax.ShapeDtypeStruct((t, width), BF16),
        compiler_params=_params("parallel", "parallel"),
        name="memory_attention",
    )(proj, kv, g_q.reshape(1, -1), g_k.reshape(1, -1))


def _projection_layout(q_rank, kv_rank):
    dsa = DSA_HEADS * DSA_HEAD_DIM
    sizes = [("c_q", q_rank), ("c_kv", kv_rank), ("k_r", MLA_ROPE), ("q_d", dsa), ("k_d", dsa), ("v_d", dsa),
             ("q_i", IDX_HEADS * IDX_DIM), ("k_i", IDX_DIM), ("w_i", IDX_HEADS), ("q_m", MEM_HEADS * MEM_HEAD_DIM)]
    src, start = {}, 0
    for name, n in sizes:
        src[name] = (start, n)
        start += n
    return src, start


def _hybrid_mixer(x, h, mem, positions, w_in, g_cq, w_uq, g_ckv, w_ukv, g_q_mla, g_k_mla, g_q_dsa, g_k_dsa,
                  g_mem, w_mem_kv, g_q_mem, g_k_mem, w_o_mla, w_o_dsa, w_o_mem, w_gate, w_out, batch):
    t, d = h.shape
    s = t // batch
    q_rank, kv_rank = w_uq.shape[0], w_ukv.shape[0]
    src, d_in = _projection_layout(q_rank, kv_rank)
    assert d_in == w_in.shape[1]
    dsa = DSA_HEADS * DSA_HEAD_DIM
    assert q_rank == dsa, "the reordered projection layout assumes equal widths for these groups"

    take = lambda name: w_in[:, src[name][0]:src[name][0] + src[name][1]]
    pieces = [("c_q", take("c_q")), ("q_d", take("q_d")), ("k_d", take("k_d")), ("v_d", take("v_d")),
              ("q_i", take("q_i")), ("q_m", take("q_m")), ("c_kv", take("c_kv")),
              ("k_r", jnp.concatenate([take("k_r"), take("k_i")], axis=1)),
              ("w_i", _pad_last(take("w_i"), LANES))]
    col, start = {}, 0
    for name, piece in pieces:
        assert start % piece.shape[1] == 0, name
        col[name] = start
        start += piece.shape[1]
    w_in_r = jnp.concatenate([p for _, p in pieces], axis=1).astype(BF16)

    proj = _matmul(h, w_in_r, F32, bm=1024, bn=768, name="input_projection")
    cos, sin = _rope_tables(positions)

    q_a, k_a, v_a = _mla_prep(proj, col, g_cq, w_uq, g_ckv, w_ukv, g_q_mla, g_k_mla, cos, sin)
    y_mla = _attention(q_a, k_a, _heads_transposed(v_a, batch, MLA_HEADS), None, batch, MLA_HEADS,
                       MLA_QK_PAD, MLA_V, MLA_QK ** -0.5, "chunk_causal_attention")

    q_b = _head_norm_rope(proj, col["q_d"] // dsa, g_q_dsa, cos, sin)
    k_b = _head_norm_rope(proj, col["k_d"] // dsa, g_k_dsa, cos, sin)
    v_b = proj[:, col["v_d"]:col["v_d"] + dsa].astype(BF16)
    qi_width = IDX_HEADS * IDX_DIM
    qi = _rope_only(proj, qi_width, col["q_i"] // qi_width, cos, sin)
    qi = qi.reshape(batch, s, IDX_HEADS, IDX_DIM).transpose(0, 2, 1, 3)
    ki = _rope_only(proj, LANES, col["k_r"] // LANES, cos, sin)[:, IDX_DIM:]
    kit = ki.reshape(batch, s, IDX_DIM).transpose(0, 2, 1)
    topk = min(DSA_MAX_TOPK, s // 4)
    selected = _indexer_select(qi, kit, proj, col["w_i"] // LANES, batch, topk)
    selected_t = selected.reshape(batch, s, s).transpose(0, 2, 1)
    y_dsa = _attention(q_b, k_b, _heads_transposed(v_b, batch, DSA_HEADS), selected_t, batch, DSA_HEADS,
                       DSA_HEAD_DIM, DSA_HEAD_DIM, DSA_HEAD_DIM ** -0.5, "masked_attention")

    m_len = mem.shape[1]
    mem_n = _rmsnorm(mem.reshape(batch * m_len, d), g_mem)
    kv_mem = _matmul(mem_n, w_mem_kv.astype(BF16), F32, bm=512, bn=512, name="memory_kv")
    y_mem = _memory_attention(proj, col["q_m"] // (MEM_HEADS * MEM_HEAD_DIM), kv_mem, g_q_mem, g_k_mem, batch)

    gated = _gate_combine(h, (y_mla, y_dsa, y_mem), w_gate.astype(BF16),
                          (w_o_mla.astype(BF16), w_o_dsa.astype(BF16), w_o_mem.astype(BF16)))
    return _matmul_residual(gated, w_out.astype(BF16), x, 1.0, name="mixer_out")


def kernel(x, mem, positions, g_ff1, w_ff1_gate, w_ff1_up, w_ff1_down, g_mix, w_in, g_cq, w_uq, g_ckv, w_ukv, g_q_mla, g_k_mla, g_q_dsa, g_k_dsa, g_mem, w_mem_kv, g_q_mem, g_k_mem, w_o_mla, w_o_dsa, w_o_mem, w_gate, w_out, g_ff2, w_ff2_gate, w_ff2_up, w_ff2_down):
    batch, seq, d = x.shape
    xt = x.reshape(batch * seq, d)
    for l in range(g_ff1.shape[0]):
        xt = _ffn_half_step(xt, g_ff1[l], w_ff1_gate[l], w_ff1_up[l], w_ff1_down[l])
        h = _rmsnorm(xt, g_mix[l])
        xt = _hybrid_mixer(xt, h, mem, positions, w_in[l], g_cq[l], w_uq[l], g_ckv[l], w_ukv[l],
                           g_q_mla[l], g_k_mla[l], g_q_dsa[l], g_k_dsa[l], g_mem[l], w_mem_kv[l],
                           g_q_mem[l], g_k_mem[l], w_o_mla[l], w_o_dsa[l], w_o_mem[l], w_gate[l], w_out[l], batch)
        xt = _ffn_half_step(xt, g_ff2[l], w_ff2_gate[l], w_ff2_up[l], w_ff2_down[l])
    return xt.reshape(batch, seq, d)
```

```python
import functools

import jax
import jax.numpy as jnp
from jax import lax
from jax.experimental import pallas as pl
from jax.experimental.pallas import tpu as pltpu

CHUNK = 64
ROPE_THETA = 500000.0
EPS = 1e-6

MLA_HEADS = 12
MLA_NOPE = 128
MLA_ROPE = 64
MLA_V = 128
MLA_QK = MLA_ROPE + MLA_NOPE
MLA_QK_PAD = 256

DSA_HEADS = 12
DSA_HEAD_DIM = 128
DSA_ROT = DSA_HEAD_DIM // 4
DSA_MAX_TOPK = 256
IDX_HEADS = 32
IDX_DIM = 64
IDX_ROT = IDX_DIM // 4

MEM_HEADS = 4
MEM_HEAD_DIM = 256

LANES = 128
SUBLANES = 8
VMEM_LIMIT_BYTES = 56 * 1024 * 1024

F32 = jnp.float32
BF16 = jnp.bfloat16
NEG_INF = float("-inf")
INT_MIN = -2 ** 31
NEG_INF_KEY = (0xFF800000 - (1 << 32)) ^ 0x7FFFFFFF


def _params(*semantics):
    return pltpu.CompilerParams(dimension_semantics=semantics, vmem_limit_bytes=VMEM_LIMIT_BYTES)


def _block(dim, preferred):
    if dim <= preferred:
        return dim
    b = preferred - preferred % LANES
    while b >= LANES:
        if dim % b == 0:
            return b
        b -= LANES
    return dim


def _rms_scale(x, width):
    return lax.rsqrt(jnp.sum(x * x, axis=-1, keepdims=True) * (1.0 / width) + EPS)


def _dot(a, b):
    return jnp.dot(a, b, preferred_element_type=F32)


def _dot_nt(a, b):
    return lax.dot_general(a, b, (((1,), (1,)), ((), ())), preferred_element_type=F32)


def _sigmoid(x):
    return 1.0 / (1.0 + jnp.exp(-x))


def _chunk_of(frame):
    return lax.shift_right_logical(frame, CHUNK.bit_length() - 1)


def _rmsnorm_kernel(x_ref, g_ref, o_ref):
    x = x_ref[...]
    o_ref[...] = (x * _rms_scale(x, x.shape[-1]) * g_ref[...]).astype(o_ref.dtype)


def _rmsnorm(x, g, out_dtype=BF16):
    m, d = x.shape
    bm = _block(m, 512)
    return pl.pallas_call(
        _rmsnorm_kernel,
        grid=(m // bm,),
        in_specs=[pl.BlockSpec((bm, d), lambda i: (i, 0)), pl.BlockSpec((1, d), lambda i: (0, 0))],
        out_specs=pl.BlockSpec((bm, d), lambda i: (i, 0)),
        out_shape=jax.ShapeDtypeStruct((m, d), out_dtype),
        compiler_params=_params("parallel"),
        name="rmsnorm",
    )(x, g.reshape(1, d))


def _matmul_kernel(a_ref, b_ref, o_ref):
    o_ref[...] = _dot(a_ref[...], b_ref[...]).astype(o_ref.dtype)


def _matmul(a, b, out_dtype, bm=1024, bn=512, name="matmul"):
    m, k = a.shape
    n = b.shape[1]
    bm, bn = _block(m, bm), _block(n, bn)
    return pl.pallas_call(
        _matmul_kernel,
        grid=(m // bm, n // bn),
        in_specs=[pl.BlockSpec((bm, k), lambda i, j: (i, 0)), pl.BlockSpec((k, bn), lambda i, j: (0, j))],
        out_specs=pl.BlockSpec((bm, bn), lambda i, j: (i, j)),
        out_shape=jax.ShapeDtypeStruct((m, n), out_dtype),
        compiler_params=_params("parallel", "parallel"),
        name=name,
    )(a, b)


def _swiglu_up_kernel(a_ref, wg_ref, wu_ref, o_ref):
    a = a_ref[...]
    g = _dot(a, wg_ref[...])
    u = _dot(a, wu_ref[...])
    o_ref[...] = (g * _sigmoid(g) * u).astype(o_ref.dtype)


def _swiglu_up(a, wg, wu, bm=1024, bn=512):
    m, k = a.shape
    n = wg.shape[1]
    bm, bn = _block(m, bm), _block(n, bn)
    return pl.pallas_call(
        _swiglu_up_kernel,
        grid=(m // bm, n // bn),
        in_specs=[pl.BlockSpec((bm, k), lambda i, j: (i, 0)),
                  pl.BlockSpec((k, bn), lambda i, j: (0, j)),
                  pl.BlockSpec((k, bn), lambda i, j: (0, j))],
        out_specs=pl.BlockSpec((bm, bn), lambda i, j: (i, j)),
        out_shape=jax.ShapeDtypeStruct((m, n), BF16),
        compiler_params=_params("parallel", "parallel"),
        name="swiglu_up",
    )(a, wg, wu)


def _matmul_residual_kernel(a_ref, b_ref, x_ref, o_ref, *, scale):
    o_ref[...] = x_ref[...] + scale * _dot(a_ref[...], b_ref[...])


def _matmul_residual(a, b, x, scale, bm=512, bn=512, name="matmul_residual"):
    m, k = a.shape
    n = b.shape[1]
    bm, bn = _block(m, bm), _block(n, bn)
    return pl.pallas_call(
        functools.partial(_matmul_residual_kernel, scale=scale),
        grid=(m // bm, n // bn),
        in_specs=[pl.BlockSpec((bm, k), lambda i, j: (i, 0)),
                  pl.BlockSpec((k, bn), lambda i, j: (0, j)),
                  pl.BlockSpec((bm, bn), lambda i, j: (i, j))],
        out_specs=pl.BlockSpec((bm, bn), lambda i, j: (i, j)),
        out_shape=jax.ShapeDtypeStruct((m, n), F32),
        compiler_params=_params("parallel", "parallel"),
        name=name,
    )(a, b, x)


def _ffn_half_step(x, g, w_gate, w_up, w_down):
    h = _rmsnorm(x, g)
    a = _swiglu_up(h, w_gate.astype(BF16), w_up.astype(BF16))
    return _matmul_residual(a, w_down.astype(BF16), x, 0.5, name="ffn_down")


def _gate_combine_kernel(h_ref, ya_ref, yb_ref, yc_ref, wg_ref, wa_ref, wb_ref, wc_ref, o_ref):
    h = h_ref[...]
    out = _sigmoid(_dot(h, wg_ref[0])) * _dot(ya_ref[...], wa_ref[...])
    out = out + _sigmoid(_dot(h, wg_ref[1])) * _dot(yb_ref[...], wb_ref[...])
    out = out + _sigmoid(_dot(h, wg_ref[2])) * _dot(yc_ref[...], wc_ref[...])
    o_ref[...] = out.astype(o_ref.dtype)


def _gate_combine(h, ys, w_gate, w_os, bm=512, bn=512):
    m, d = h.shape
    n = w_gate.shape[-1]
    bm, bn = _block(m, bm), _block(n, bn)
    row = lambda w: pl.BlockSpec((bm, w), lambda i, j: (i, 0))
    col = lambda k: pl.BlockSpec((k, bn), lambda i, j: (0, j))
    return pl.pallas_call(
        _gate_combine_kernel,
        grid=(m // bm, n // bn),
        in_specs=[row(d)] + [row(y.shape[1]) for y in ys]
                 + [pl.BlockSpec((3, d, bn), lambda i, j: (0, 0, j))] + [col(w.shape[0]) for w in w_os],
        out_specs=pl.BlockSpec((bm, bn), lambda i, j: (i, j)),
        out_shape=jax.ShapeDtypeStruct((m, n), BF16),
        compiler_params=_params("parallel", "parallel"),
        name="gate_combine",
    )(h, *ys, w_gate, *w_os)


def _rope_tables_kernel(pos_ref, invf_ref, sgn_ref, cos_ref, sin_ref):
    pos = pos_ref[...].astype(F32)
    for t in range(invf_ref.shape[0]):
        ang = pos * invf_ref[t:t + 1, :]
        cos_ref[t] = jnp.cos(ang)
        sin_ref[t] = jnp.sin(ang) * sgn_ref[t:t + 1, :]


def _rope_lane_tables(rot, period):
    half = rot // 2
    inv_freq = ROPE_THETA ** (-jnp.arange(half, dtype=F32) / half)
    lane = jnp.arange(LANES) % period
    invf = jnp.where(lane < rot, inv_freq[lane % half], 0.0)
    sgn = jnp.where(lane < half, -1.0, jnp.where(lane < rot, 1.0, 0.0))
    return invf.astype(F32), sgn.astype(F32)


def _rope_tables(positions):
    t = positions.size
    specs = [_rope_lane_tables(MLA_ROPE, LANES), _rope_lane_tables(DSA_ROT, LANES),
             _rope_lane_tables(IDX_ROT, IDX_DIM)]
    invf = jnp.stack([s[0] for s in specs])
    sgn = jnp.stack([s[1] for s in specs])
    bm = _block(t, 1024)
    n = invf.shape[0]
    const = pl.BlockSpec((n, LANES), lambda i: (0, 0))
    out = pl.BlockSpec((n, bm, LANES), lambda i: (0, i, 0))
    return pl.pallas_call(
        _rope_tables_kernel,
        grid=(t // bm,),
        in_specs=[pl.BlockSpec((bm, 1), lambda i: (i, 0)), const, const],
        out_specs=[out, out],
        out_shape=[jax.ShapeDtypeStruct((n, t, LANES), F32)] * 2,
        compiler_params=_params("parallel"),
        name="rope_tables",
    )(positions.reshape(t, 1), invf, sgn)


def _rope_tile(x, cos, sin, half, period):
    lane = lax.broadcasted_iota(jnp.int32, x.shape, 1) & (period - 1)
    partner = jnp.where(lane < half, pltpu.roll(x, LANES - half, 1), pltpu.roll(x, half, 1))
    return x * cos + partner * sin


def _mla_q_kernel(cq_ref, gcq_ref, w_ref, gq_ref, cos_ref, sin_ref, o_ref):
    x = cq_ref[...]
    xn = (x * _rms_scale(x, x.shape[-1]) * gcq_ref[...]).astype(BF16)
    q = _dot(xn, w_ref[...])
    cos, sin = cos_ref[0], sin_ref[0]
    g = gq_ref[...]
    for h in range(MLA_HEADS):
        lo = h * MLA_QK_PAD
        qh = q[:, lo:lo + MLA_QK_PAD]
        y = qh * _rms_scale(qh, MLA_QK) * g
        o_ref[:, lo:lo + LANES] = _rope_tile(y[:, :LANES], cos, sin, MLA_ROPE // 2, LANES).astype(o_ref.dtype)
        o_ref[:, lo + LANES:lo + MLA_QK_PAD] = y[:, LANES:].astype(o_ref.dtype)


def _mla_kv_kernel(ckv_ref, kr_ref, gckv_ref, wk_ref, wv_ref, gk_ref, cos_ref, sin_ref, k_ref, v_ref):
    x = ckv_ref[...]
    xn = (x * _rms_scale(x, x.shape[-1]) * gckv_ref[...]).astype(BF16)
    k_nope = _dot(xn, wk_ref[...])
    v_ref[...] = _dot(xn, wv_ref[...]).astype(v_ref.dtype)
    kr = kr_ref[...]
    lane = lax.broadcasted_iota(jnp.int32, kr.shape, 1)
    kr = jnp.where(lane < MLA_ROPE, kr, 0.0)
    cos, sin = cos_ref[0], sin_ref[0]
    g = gk_ref[...]
    for h in range(MLA_HEADS):
        lo = h * MLA_QK_PAD
        ka = k_nope[:, lo:lo + LANES] + kr
        kb = k_nope[:, lo + LANES:lo + MLA_QK_PAD]
        ms = (jnp.sum(ka * ka, axis=-1, keepdims=True) + jnp.sum(kb * kb, axis=-1, keepdims=True)) * (1.0 / MLA_QK)
        r = lax.rsqrt(ms + EPS)
        ya = ka * r * g[:, :LANES]
        yb = kb * r * g[:, LANES:]
        k_ref[:, lo:lo + LANES] = _rope_tile(ya, cos, sin, MLA_ROPE // 2, LANES).astype(k_ref.dtype)
        k_ref[:, lo + LANES:lo + MLA_QK_PAD] = yb.astype(k_ref.dtype)


def _pad_last(a, width):
    return jnp.pad(a, [(0, 0)] * (a.ndim - 1) + [(0, width - a.shape[-1])])


def _mla_prep(proj, col, g_cq, w_uq, g_ckv, w_ukv, g_q, g_k, cos, sin):
    t = proj.shape[0]
    q_rank, kv_rank = w_uq.shape[0], w_ukv.shape[0]
    hp = MLA_HEADS * MLA_QK_PAD
    w_q = _pad_last(w_uq.reshape(q_rank, MLA_HEADS, MLA_QK), MLA_QK_PAD).reshape(q_rank, hp).astype(BF16)
    w_kv = w_ukv.reshape(kv_rank, MLA_HEADS, MLA_NOPE + MLA_V)
    w_k = jnp.pad(w_kv[..., :MLA_NOPE], ((0, 0), (0, 0), (MLA_ROPE, MLA_QK_PAD - MLA_QK)))
    w_k = w_k.reshape(kv_rank, hp).astype(BF16)
    w_v = w_kv[..., MLA_NOPE:].reshape(kv_rank, MLA_HEADS * MLA_V).astype(BF16)
    g_qp = _pad_last(g_q, MLA_QK_PAD).reshape(1, MLA_QK_PAD)
    g_kp = _pad_last(g_k, MLA_QK_PAD).reshape(1, MLA_QK_PAD)

    bm = _block(t, 512)
    rows = lambda w, c: pl.BlockSpec((bm, w), lambda i, c=c: (i, c))
    full = lambda a: pl.BlockSpec(a.shape, lambda i: (0,) * a.ndim)
    table = pl.BlockSpec((1, bm, LANES), lambda i: (0, i, 0))
    g_cq2, g_ckv2 = g_cq.reshape(1, -1), g_ckv.reshape(1, -1)

    q = pl.pallas_call(
        _mla_q_kernel,
        grid=(t // bm,),
        in_specs=[rows(q_rank, col["c_q"] // q_rank), full(g_cq2), full(w_q), full(g_qp), table, table],
        out_specs=pl.BlockSpec((bm, hp), lambda i: (i, 0)),
        out_shape=jax.ShapeDtypeStruct((t, hp), BF16),
        compiler_params=_params("parallel"),
        name="mla_q_prep",
    )(proj, g_cq2, w_q, g_qp, cos, sin)

    k, v = pl.pallas_call(
        _mla_kv_kernel,
        grid=(t // bm,),
        in_specs=[rows(kv_rank, col["c_kv"] // kv_rank), rows(LANES, col["k_r"] // LANES), full(g_ckv2),
                  full(w_k), full(w_v), full(g_kp), table, table],
        out_specs=[pl.BlockSpec((bm, hp), lambda i: (i, 0)),
                   pl.BlockSpec((bm, MLA_HEADS * MLA_V), lambda i: (i, 0))],
        out_shape=[jax.ShapeDtypeStruct((t, hp), BF16), jax.ShapeDtypeStruct((t, MLA_HEADS * MLA_V), BF16)],
        compiler_params=_params("parallel"),
        name="mla_kv_prep",
    )(proj, proj, g_ckv2, w_k, w_v, g_kp, cos, sin)
    return q, k, v


ATTN_HEADS_PER_STEP = 2


def _attn_scratch(tq, dv):
    per_head = [pltpu.VMEM((1, tq), F32), pltpu.VMEM((1, tq), F32), pltpu.VMEM((dv, tq), F32)]
    per_parity = [pltpu.VMEM((tq, tq), F32), pltpu.VMEM((1, tq), F32),
                  pltpu.VMEM((tq, tq), BF16), pltpu.VMEM((1, tq), F32)]
    return (per_head + 2 * per_parity) * ATTN_HEADS_PER_STEP


def _attn_kernel(q_ref, k_ref, vt_ref, *rest, tq, dk, dv, scale, selected_keys):
    n_scratch = 11 * ATTN_HEADS_PER_STEP
    scratch = rest[-n_scratch:]
    sel_ref = rest[0] if selected_keys else None
    o_ref = rest[-n_scratch - 1]
    i = pl.program_id(2)
    heads = range(ATTN_HEADS_PER_STEP)
    m_of, l_of, acc_of, s_of, mx_of, p_of, al_of = [], [], [], [], [], [], []
    for h in heads:
        refs = scratch[11 * h:11 * (h + 1)]
        m_of.append(refs[0]); l_of.append(refs[1]); acc_of.append(refs[2])
        s_of.append((refs[3], refs[7])); mx_of.append((refs[4], refs[8]))
        p_of.append((refs[5], refs[9])); al_of.append((refs[6], refs[10]))
    qs = [q_ref[:, h * dk:(h + 1) * dk] for h in heads]

    for h in heads:
        m_of[h][...] = jnp.full((1, tq), NEG_INF, F32)
        l_of[h][...] = jnp.zeros((1, tq), F32)
        acc_of[h][...] = jnp.zeros((dv, tq), F32)
        s_of[h][1][...] = jnp.full((tq, tq), NEG_INF, F32)
        mx_of[h][1][...] = jnp.full((1, tq), NEG_INF, F32)
        p_of[h][0][...] = jnp.zeros((tq, tq), BF16)
        al_of[h][0][...] = jnp.zeros((1, tq), F32)

    def scores(block, par, diagonal):
        off = pl.multiple_of(block * tq, tq)
        if selected_keys:
            visible = sel_ref[pl.ds(off, tq), :].astype(F32) > 0.0
        elif diagonal:
            key_chunk = _chunk_of(lax.broadcasted_iota(jnp.int32, (tq, tq), 0))
            query_chunk = _chunk_of(lax.broadcasted_iota(jnp.int32, (tq, tq), 1))
            visible = key_chunk <= query_chunk
        else:
            visible = None
        for h in heads:
            s = _dot_nt(k_ref[pl.ds(off, tq), h * dk:(h + 1) * dk], qs[h]) * scale
            if visible is not None:
                s = jnp.where(visible, s, NEG_INF)
            s_of[h][par][...] = s
            mx_of[h][par][...] = jnp.max(s, axis=0, keepdims=True)

    def softmax(par):
        for h in heads:
            m_prev = m_of[h][...]
            m_new = jnp.maximum(m_prev, mx_of[h][par][...])
            m_safe = jnp.where(m_new == NEG_INF, 0.0, m_new)
            alpha = jnp.exp(m_prev - m_safe)
            p = jnp.exp(s_of[h][par][...] - m_safe)
            l_of[h][...] = alpha * l_of[h][...] + jnp.sum(p, axis=0, keepdims=True)
            p_of[h][par][...] = p.astype(BF16)
            al_of[h][par][...] = alpha
            m_of[h][...] = m_new

    def values(block, par):
        off = pl.multiple_of(jnp.maximum(block, 0) * tq, tq)
        for h in heads:
            acc_of[h][...] = al_of[h][par][...] * acc_of[h][...] + _dot(vt_ref[h, :, pl.ds(off, tq)],
                                                                         p_of[h][par][...])

    def step(t, par, diagonal=False):
        values(t - 2, par)
        softmax(1 - par)
        scores(t, par, diagonal)

    def pair(u, carry):
        step(2 * u, 0)
        step(2 * u + 1, 1)
        return carry

    lax.fori_loop(0, lax.shift_right_logical(i, 1), pair, 0)

    def drain(par):
        step(i, par, diagonal=True)
        values(i - 1, 1 - par)
        softmax(par)
        values(i, par)

    odd = (i & 1) == 1

    @pl.when(odd)
    def _():
        step(i - 1, 0)
        drain(1)

    @pl.when(jnp.logical_not(odd))
    def _():
        drain(0)

    for h in heads:
        o = acc_of[h][...] / l_of[h][...]
        o_ref[:, h * dv:(h + 1) * dv] = o.T.astype(o_ref.dtype)


def _attention(q, k, vt, selected_t, batch, heads, dk, dv, scale, name, tq=256):
    t = q.shape[0]
    s = t // batch
    tq = _block(s, tq)
    nq = s // tq
    hp = ATTN_HEADS_PER_STEP
    in_specs = [pl.BlockSpec((tq, hp * dk), lambda b, h, i: (b * nq + i, h)),
                pl.BlockSpec((s, hp * dk), lambda b, h, i: (b, h)),
                pl.BlockSpec((None, hp, dv, s), lambda b, h, i: (b, h, 0, 0))]
    operands = [q, k, vt]
    if selected_t is not None:
        in_specs.append(pl.BlockSpec((None, s, tq), lambda b, h, i: (b, 0, i)))
        operands.append(selected_t)
    return pl.pallas_call(
        functools.partial(_attn_kernel, tq=tq, dk=dk, dv=dv, scale=scale, selected_keys=selected_t is not None),
        grid=(batch, heads // hp, nq),
        in_specs=in_specs,
        out_specs=pl.BlockSpec((tq, hp * dv), lambda b, h, i: (b * nq + i, h)),
        out_shape=jax.ShapeDtypeStruct((t, heads * dv), BF16),
        scratch_shapes=_attn_scratch(tq, dv),
        compiler_params=_params("parallel", "parallel", "parallel"),
        name=name,
    )(*operands)


def _heads_transposed(v, batch, heads):
    t, width = v.shape
    return v.reshape(batch, t // batch, heads, width // heads).transpose(0, 2, 3, 1)


def _head_norm_rope_kernel(x_ref, g_ref, cos_ref, sin_ref, o_ref, *, heads):
    cos, sin = cos_ref[0], sin_ref[0]
    g = g_ref[...]
    for h in range(heads):
        x = x_ref[:, h * LANES:(h + 1) * LANES]
        y = x * _rms_scale(x, LANES) * g
        o_ref[:, h * LANES:(h + 1) * LANES] = _rope_tile(y, cos, sin, DSA_ROT // 2, LANES).astype(o_ref.dtype)


def _head_norm_rope(proj, col_block, g, cos, sin):
    t = proj.shape[0]
    width = DSA_HEADS * DSA_HEAD_DIM
    bm = _block(t, 512)
    table = pl.BlockSpec((1, bm, LANES), lambda i: (1, i, 0))
    return pl.pallas_call(
        functools.partial(_head_norm_rope_kernel, heads=DSA_HEADS),
        grid=(t // bm,),
        in_specs=[pl.BlockSpec((bm, width), lambda i: (i, col_block)),
                  pl.BlockSpec((1, LANES), lambda i: (0, 0)), table, table],
        out_specs=pl.BlockSpec((bm, width), lambda i: (i, 0)),
        out_shape=jax.ShapeDtypeStruct((t, width), BF16),
        compiler_params=_params("parallel"),
        name="head_norm_rope",
    )(proj, g.reshape(1, LANES), cos, sin)


def _rope_only_kernel(x_ref, cos_ref, sin_ref, o_ref):
    cos, sin = cos_ref[0], sin_ref[0]
    for c in range(x_ref.shape[1] // LANES):
        x = x_ref[:, c * LANES:(c + 1) * LANES]
        o_ref[:, c * LANES:(c + 1) * LANES] = _rope_tile(x, cos, sin, IDX_ROT // 2, IDX_DIM).astype(o_ref.dtype)


def _rope_only(proj, width, col_block, cos, sin):
    t = proj.shape[0]
    bm = _block(t, 512)
    table = pl.BlockSpec((1, bm, LANES), lambda i: (2, i, 0))
    return pl.pallas_call(
        _rope_only_kernel,
        grid=(t // bm,),
        in_specs=[pl.BlockSpec((bm, width), lambda i: (i, col_block)), table, table],
        out_specs=pl.BlockSpec((bm, width), lambda i: (i, 0)),
        out_shape=jax.ShapeDtypeStruct((t, width), BF16),
        compiler_params=_params("parallel"),
        name="indexer_rope",
    )(proj, cos, sin)


COUNT_ACC_ROWS = 64


def _sortable_key(x):
    bits = lax.bitcast_convert_type(x, jnp.int32)
    return bits ^ ((bits >> 31) & 0x7FFFFFFF)


def _indexer_kernel(qi_ref, kit_ref, w_ref, sel_ref, key_ref, wb_ref, *, tq, tk, topk, seq, w_scale):
    i = pl.program_id(1)
    lanes_per_tile = tk // LANES
    n_tiles = pl.cdiv((i + 1) * tq, tk)
    row_chunk = _chunk_of(i * tq + lax.broadcasted_iota(jnp.int32, (tq, 1), 0))

    w = w_ref[...] * w_scale
    for h in range(IDX_HEADS):
        wb_ref[h] = jnp.broadcast_to(w[:, h:h + 1], (tq, LANES))

    def score_tile(j, carry):
        off = pl.multiple_of(j * tk, tk)
        kt = kit_ref[:, pl.ds(off, tk)]
        acc = [jnp.zeros((tq, LANES), F32) for _ in range(lanes_per_tile)]
        for h in range(IDX_HEADS):
            logits = _dot(qi_ref[h], kt)
            wh = wb_ref[h]
            for c in range(lanes_per_tile):
                acc[c] = acc[c] + jnp.maximum(logits[:, c * LANES:(c + 1) * LANES], 0.0) * wh
        for c in range(lanes_per_tile):
            col = off + c * LANES + lax.broadcasted_iota(jnp.int32, (1, LANES), 1)
            score = jnp.where(_chunk_of(col) <= row_chunk, acc[c], NEG_INF)
            key_ref[pl.ds(off + c * LANES, LANES), :] = _sortable_key(score.T)
        return carry

    lax.fori_loop(0, n_tiles, score_tile, 0)

    def count(pred):
        rows = min(tk, COUNT_ACC_ROWS)

        def tile(j, acc):
            off = pl.multiple_of(j * tk, tk)
            hit = jnp.where(pred(key_ref[pl.ds(off, tk), :], off), 1.0, 0.0)
            return acc + jnp.sum(hit.reshape(tk // rows, rows, tq), axis=0)
        acc = lax.fori_loop(0, n_tiles, tile, jnp.zeros((rows, tq), F32))
        return jnp.sum(acc, axis=0, keepdims=True)

    def value_bit(b, lo):
        cand = lo + lax.shift_left(jnp.int32(1), 31 - b)
        n_ge = count(lambda kt, _: kt >= cand)
        return jnp.where(n_ge >= topk, cand, lo)

    thr = lax.fori_loop(0, 32, value_bit, jnp.full((1, tq), INT_MIN, jnp.int32))
    n_ge = count(lambda kt, _: kt >= thr)
    surplus = jnp.logical_and(n_ge > topk, thr > NEG_INF_KEY)

    def write_tile(j, carry):
        off = pl.multiple_of(j * tk, tk)
        kt = key_ref[pl.ds(off, tk), :]
        chosen = jnp.logical_and(kt >= thr, kt > NEG_INF_KEY)
        sel_ref[pl.ds(off, tk), :] = jnp.where(chosen, 1.0, 0.0).astype(sel_ref.dtype)
        return carry

    lax.fori_loop(0, n_tiles, write_tile, 0)

    def clear_tile(j, carry):
        off = pl.multiple_of(j * tk, tk)
        sel_ref[pl.ds(off, tk), :] = jnp.zeros((tk, tq), sel_ref.dtype)
        return carry

    lax.fori_loop(n_tiles, seq // tk, clear_tile, 0)

    @pl.when(jnp.max(jnp.where(surplus, 1.0, 0.0)) > 0.0)
    def _():
        n_gt = count(lambda kt, _: kt > thr)
        need = topk - n_gt

        def position(first_key):
            return first_key + lax.broadcasted_iota(jnp.int32, (tk, tq), 0)

        def position_bit(b, base):
            cand = base + lax.shift_left(jnp.int32(1), (seq - 1).bit_length() - 1 - b)
            n_before = count(lambda kt, k0: jnp.logical_and(kt == thr, position(k0) < cand))
            return jnp.where(n_before < need, cand, base)

        last = lax.fori_loop(0, (seq - 1).bit_length(), position_bit, jnp.zeros((1, tq), jnp.int32))

        def rewrite_tile(j, carry):
            off = pl.multiple_of(j * tk, tk)
            kt = key_ref[pl.ds(off, tk), :]
            chosen = jnp.logical_or(kt > thr, jnp.logical_and(kt == thr, position(off) <= last))
            chosen = jnp.logical_and(chosen, kt > NEG_INF_KEY)
            sel_ref[pl.ds(off, tk), :] = jnp.where(chosen, 1.0, 0.0).astype(sel_ref.dtype)
            return carry

        lax.fori_loop(0, n_tiles, rewrite_tile, 0)


def _indexer_select(qi, kit, w, w_col_block, batch, topk, tq=128, tk=256):
    s = qi.shape[2]
    tq = _block(s, tq)
    tk = _block(s, tk)
    assert tq == LANES, "score tiles are transposed one (LANES, LANES) tile at a time"
    nq = s // tq
    w_scale = (IDX_DIM ** -0.5) * (IDX_HEADS ** -0.5)
    return pl.pallas_call(
        functools.partial(_indexer_kernel, tq=tq, tk=tk, topk=topk, seq=s, w_scale=w_scale),
        grid=(batch, nq),
        in_specs=[pl.BlockSpec((None, IDX_HEADS, tq, IDX_DIM), lambda b, i: (b, 0, i, 0)),
                  pl.BlockSpec((None, IDX_DIM, s), lambda b, i: (b, 0, 0)),
                  pl.BlockSpec((tq, LANES), lambda b, i: (b * nq + i, w_col_block))],
        out_specs=pl.BlockSpec((None, s, tq), lambda b, i: (b, 0, i)),
        out_shape=jax.ShapeDtypeStruct((batch, s, s), BF16),
        scratch_shapes=[pltpu.VMEM((s, tq), jnp.int32), pltpu.VMEM((IDX_HEADS, tq, LANES), F32)],
        compiler_params=_params("parallel", "parallel"),
        name="indexer_select",
    )(qi, kit, w)


def _memory_attn_kernel(q_ref, kv_ref, gq_ref, gk_ref, o_ref, *, scale):
    hd = MEM_HEAD_DIM
    gq, gk = gq_ref[...], gk_ref[...]
    for h in range(MEM_HEADS):
        q = q_ref[:, h * hd:(h + 1) * hd]
        k = kv_ref[:, h * hd:(h + 1) * hd]
        v = kv_ref[:, (MEM_HEADS + h) * hd:(MEM_HEADS + h + 1) * hd].astype(BF16)
        qn = (q * _rms_scale(q, hd) * gq).astype(BF16)
        kn = (k * _rms_scale(k, hd) * gk).astype(BF16)
        s = _dot_nt(qn, kn) * scale
        e = jnp.exp(s - jnp.max(s, axis=-1, keepdims=True))
        p = e / jnp.sum(e, axis=-1, keepdims=True)
        o_ref[:, h * hd:(h + 1) * hd] = _dot(p.astype(BF16), v).astype(o_ref.dtype)


def _memory_attention(proj, col_block, kv, g_q, g_k, batch, tq=512):
    t = proj.shape[0]
    s = t // batch
    m = kv.shape[0] // batch
    width = MEM_HEADS * MEM_HEAD_DIM
    tq = _block(s, tq)
    nq = s // tq
    g_spec = pl.BlockSpec((1, MEM_HEAD_DIM), lambda b, i: (0, 0))
    return pl.pallas_call(
        functools.partial(_memory_attn_kernel, scale=MEM_HEAD_DIM ** -0.5),
        grid=(batch, nq),
        in_specs=[pl.BlockSpec((tq, width), lambda b, i: (b * nq + i, col_block)),
                  pl.BlockSpec((m, 2 * width), lambda b, i: (b, 0)), g_spec, g_spec],
        out_specs=pl.BlockSpec((tq, width), lambda b, i: (b * nq + i, 0)),
        out_shape=jax.ShapeDtypeStruct((t, width), BF16),
        compiler_params=_params("parallel", "parallel"),
        name="memory_attention",
    )(proj, kv, g_q.reshape(1, -1), g_k.reshape(1, -1))


def _projection_layout(q_rank, kv_rank):
    dsa = DSA_HEADS * DSA_HEAD_DIM
    sizes = [("c_q", q_rank), ("c_kv", kv_rank), ("k_r", MLA_ROPE), ("q_d", dsa), ("k_d", dsa), ("v_d", dsa),
             ("q_i", IDX_HEADS * IDX_DIM), ("k_i", IDX_DIM), ("w_i", IDX_HEADS), ("q_m", MEM_HEADS * MEM_HEAD_DIM)]
    src, start = {}, 0
    for name, n in sizes:
        src[name] = (start, n)
        start += n
    return src, start


def _hybrid_mixer(x, h, mem, positions, w_in, g_cq, w_uq, g_ckv, w_ukv, g_q_mla, g_k_mla, g_q_dsa, g_k_dsa,
                  g_mem, w_mem_kv, g_q_mem, g_k_mem, w_o_mla, w_o_dsa, w_o_mem, w_gate, w_out, batch):
    t, d = h.shape
    s = t // batch
    q_rank, kv_rank = w_uq.shape[0], w_ukv.shape[0]
    src, d_in = _projection_layout(q_rank, kv_rank)
    assert d_in == w_in.shape[1]
    dsa = DSA_HEADS * DSA_HEAD_DIM
    assert q_rank == dsa, "the reordered projection layout assumes equal widths for these groups"

    take = lambda name: w_in[:, src[name][0]:src[name][0] + src[name][1]]
    pieces = [("c_q", take("c_q")), ("q_d", take("q_d")), ("k_d", take("k_d")), ("v_d", take("v_d")),
              ("q_i", take("q_i")), ("q_m", take("q_m")), ("c_kv", take("c_kv")),
              ("k_r", jnp.concatenate([take("k_r"), take("k_i")], axis=1)),
              ("w_i", _pad_last(take("w_i"), LANES))]
    col, start = {}, 0
    for name, piece in pieces:
        assert start % piece.shape[1] == 0, name
        col[name] = start
        start += piece.shape[1]
    w_in_r = jnp.concatenate([p for _, p in pieces], axis=1).astype(BF16)

    proj = _matmul(h, w_in_r, F32, bm=1024, bn=768, name="input_projection")
    cos, sin = _rope_tables(positions)

    q_a, k_a, v_a = _mla_prep(proj, col, g_cq, w_uq, g_ckv, w_ukv, g_q_mla, g_k_mla, cos, sin)
    y_mla = _attention(q_a, k_a, _heads_transposed(v_a, batch, MLA_HEADS), None, batch, MLA_HEADS,
                       MLA_QK_PAD, MLA_V, MLA_QK ** -0.5, "chunk_causal_attention")

    q_b = _head_norm_rope(proj, col["q_d"] // dsa, g_q_dsa, cos, sin)
    k_b = _head_norm_rope(proj, col["k_d"] // dsa, g_k_dsa, cos, sin)
    v_b = proj[:, col["v_d"]:col["v_d"] + dsa].astype(BF16)
    qi_width = IDX_HEADS * IDX_DIM
    qi = _rope_only(proj, qi_width, col["q_i"] // qi_width, cos, sin)
    qi = qi.reshape(batch, s, IDX_HEADS, IDX_DIM).transpose(0, 2, 1, 3)
    ki = _rope_only(proj, LANES, col["k_r"] // LANES, cos, sin)[:, IDX_DIM:]
    kit = ki.reshape(batch, s, IDX_DIM).transpose(0, 2, 1)
    topk = min(DSA_MAX_TOPK, s // 4)
    selected_t = _indexer_select(qi, kit, proj, col["w_i"] // LANES, batch, topk)
    y_dsa = _attention(q_b, k_b, _heads_transposed(v_b, batch, DSA_HEADS), selected_t, batch, DSA_HEADS,
                       DSA_HEAD_DIM, DSA_HEAD_DIM, DSA_HEAD_DIM ** -0.5, "masked_attention")

    m_len = mem.shape[1]
    mem_n = _rmsnorm(mem.reshape(batch * m_len, d), g_mem)
    kv_mem = _matmul(mem_n, w_mem_kv.astype(BF16), F32, bm=512, bn=512, name="memory_kv")
    y_mem = _memory_attention(proj, col["q_m"] // (MEM_HEADS * MEM_HEAD_DIM), kv_mem, g_q_mem, g_k_mem, batch)

    gated = _gate_combine(h, (y_mla, y_dsa, y_mem), w_gate.astype(BF16),
                          (w_o_mla.astype(BF16), w_o_dsa.astype(BF16), w_o_mem.astype(BF16)))
    return _matmul_residual(gated, w_out.astype(BF16), x, 1.0, name="mixer_out")


def kernel(x, mem, positions, g_ff1, w_ff1_gate, w_ff1_up, w_ff1_down, g_mix, w_in, g_cq, w_uq, g_ckv, w_ukv, g_q_mla, g_k_mla, g_q_dsa, g_k_dsa, g_mem, w_mem_kv, g_q_mem, g_k_mem, w_o_mla, w_o_dsa, w_o_mem, w_gate, w_out, g_ff2, w_ff2_gate, w_ff2_up, w_ff2_down):
    batch, seq, d = x.shape
    xt = x.reshape(batch * seq, d)
    for l in range(g_ff1.shape[0]):
        xt = _ffn_half_step(xt, g_ff1[l], w_ff1_gate[l], w_ff1_up[l], w_ff1_down[l])
        h = _rmsnorm(xt, g_mix[l])
        xt = _hybrid_mixer(xt, h, mem, positions, w_in[l], g_cq[l], w_uq[l], g_ckv[l], w_ukv[l],
                           g_q_mla[l], g_k_mla[l], g_q_dsa[l], g_k_dsa[l], g_mem[l], w_mem_kv[l],
                           g_q_mem[l], g_k_mem[l], w_o_mla[l], w_o_dsa[l], w_o_mem[l], w_gate[l], w_out[l], batch)
        xt = _ffn_half_step(xt, g_ff2[l], w_ff2_gate[l], w_ff2_up[l], w_ff2_down[l])
    return xt.reshape(batch, seq, d)
```

```python
import functools

import jax
import jax.numpy as jnp
from jax import lax
from jax.experimental import pallas as pl
from jax.experimental.pallas import tpu as pltpu

CHUNK = 64
ROPE_THETA = 500000.0
EPS = 1e-6

MLA_HEADS = 12
MLA_NOPE = 128
MLA_ROPE = 64
MLA_V = 128
MLA_QK = MLA_ROPE + MLA_NOPE
MLA_QK_PAD = 256

DSA_HEADS = 12
DSA_HEAD_DIM = 128
DSA_ROT = DSA_HEAD_DIM // 4
DSA_MAX_TOPK = 256
IDX_HEADS = 32
IDX_DIM = 64
IDX_ROT = IDX_DIM // 4

MEM_HEADS = 4
MEM_HEAD_DIM = 256

LANES = 128
SUBLANES = 8
VMEM_LIMIT_BYTES = 56 * 1024 * 1024

F32 = jnp.float32
BF16 = jnp.bfloat16
NEG_INF = float("-inf")
INT_MIN = -2 ** 31
LOG2_E = 1.4426950408889634
NEG_INF_KEY = (0xFF800000 - (1 << 32)) ^ 0x7FFFFFFF


def _params(*semantics):
    return pltpu.CompilerParams(dimension_semantics=semantics, vmem_limit_bytes=VMEM_LIMIT_BYTES)


def _block(dim, preferred):
    if dim <= preferred:
        return dim
    b = preferred - preferred % LANES
    while b >= LANES:
        if dim % b == 0:
            return b
        b -= LANES
    return dim


def _rms_scale(x, width):
    return lax.rsqrt(jnp.sum(x * x, axis=-1, keepdims=True) * (1.0 / width) + EPS)


def _dot(a, b):
    return jnp.dot(a, b, preferred_element_type=F32)


def _dot_nt(a, b):
    return lax.dot_general(a, b, (((1,), (1,)), ((), ())), preferred_element_type=F32)


def _sigmoid(x):
    return 1.0 / (1.0 + jnp.exp(-x))


def _query_premultiplier(head_dim):
    return head_dim ** -0.5 * LOG2_E


def _chunk_of(frame):
    return lax.shift_right_logical(frame, CHUNK.bit_length() - 1)


def _rmsnorm_kernel(x_ref, g_ref, o_ref):
    x = x_ref[...]
    o_ref[...] = (x * _rms_scale(x, x.shape[-1]) * g_ref[...]).astype(o_ref.dtype)


def _rmsnorm(x, g, out_dtype=BF16):
    m, d = x.shape
    bm = _block(m, 512)
    return pl.pallas_call(
        _rmsnorm_kernel,
        grid=(m // bm,),
        in_specs=[pl.BlockSpec((bm, d), lambda i: (i, 0)), pl.BlockSpec((1, d), lambda i: (0, 0))],
        out_specs=pl.BlockSpec((bm, d), lambda i: (i, 0)),
        out_shape=jax.ShapeDtypeStruct((m, d), out_dtype),
        compiler_params=_params("parallel"),
        name="rmsnorm",
    )(x, g.reshape(1, d))


def _matmul_kernel(a_ref, b_ref, o_ref):
    o_ref[...] = _dot(a_ref[...], b_ref[...]).astype(o_ref.dtype)


def _matmul(a, b, out_dtype, bm=1024, bn=512, name="matmul"):
    m, k = a.shape
    n = b.shape[1]
    bm, bn = _block(m, bm), _block(n, bn)
    return pl.pallas_call(
        _matmul_kernel,
        grid=(m // bm, n // bn),
        in_specs=[pl.BlockSpec((bm, k), lambda i, j: (i, 0)), pl.BlockSpec((k, bn), lambda i, j: (0, j))],
        out_specs=pl.BlockSpec((bm, bn), lambda i, j: (i, j)),
        out_shape=jax.ShapeDtypeStruct((m, n), out_dtype),
        compiler_params=_params("parallel", "parallel"),
        name=name,
    )(a, b)


def _swiglu_up_kernel(a_ref, wg_ref, wu_ref, o_ref):
    a = a_ref[...]
    g = _dot(a, wg_ref[...])
    u = _dot(a, wu_ref[...])
    o_ref[...] = (g * _sigmoid(g) * u).astype(o_ref.dtype)


def _swiglu_up(a, wg, wu, bm=1024, bn=512):
    m, k = a.shape
    n = wg.shape[1]
    bm, bn = _block(m, bm), _block(n, bn)
    return pl.pallas_call(
        _swiglu_up_kernel,
        grid=(m // bm, n // bn),
        in_specs=[pl.BlockSpec((bm, k), lambda i, j: (i, 0)),
                  pl.BlockSpec((k, bn), lambda i, j: (0, j)),
                  pl.BlockSpec((k, bn), lambda i, j: (0, j))],
        out_specs=pl.BlockSpec((bm, bn), lambda i, j: (i, j)),
        out_shape=jax.ShapeDtypeStruct((m, n), BF16),
        compiler_params=_params("parallel", "parallel"),
        name="swiglu_up",
    )(a, wg, wu)


def _matmul_residual_kernel(a_ref, b_ref, x_ref, o_ref, *, scale):
    o_ref[...] = x_ref[...] + scale * _dot(a_ref[...], b_ref[...])


def _matmul_residual(a, b, x, scale, bm=512, bn=512, name="matmul_residual"):
    m, k = a.shape
    n = b.shape[1]
    bm, bn = _block(m, bm), _block(n, bn)
    return pl.pallas_call(
        functools.partial(_matmul_residual_kernel, scale=scale),
        grid=(m // bm, n // bn),
        in_specs=[pl.BlockSpec((bm, k), lambda i, j: (i, 0)),
                  pl.BlockSpec((k, bn), lambda i, j: (0, j)),
                  pl.BlockSpec((bm, bn), lambda i, j: (i, j))],
        out_specs=pl.BlockSpec((bm, bn), lambda i, j: (i, j)),
        out_shape=jax.ShapeDtypeStruct((m, n), F32),
        compiler_params=_params("parallel", "parallel"),
        name=name,
    )(a, b, x)


def _ffn_half_step(x, g, w_gate, w_up, w_down):
    h = _rmsnorm(x, g)
    a = _swiglu_up(h, w_gate.astype(BF16), w_up.astype(BF16))
    return _matmul_residual(a, w_down.astype(BF16), x, 0.5, name="ffn_down")


def _gate_combine_kernel(h_ref, ya_ref, yb_ref, yc_ref, wg_ref, wa_ref, wb_ref, wc_ref, o_ref):
    h = h_ref[...]
    out = _sigmoid(_dot(h, wg_ref[0])) * _dot(ya_ref[...], wa_ref[...])
    out = out + _sigmoid(_dot(h, wg_ref[1])) * _dot(yb_ref[...], wb_ref[...])
    out = out + _sigmoid(_dot(h, wg_ref[2])) * _dot(yc_ref[...], wc_ref[...])
    o_ref[...] = out.astype(o_ref.dtype)


def _gate_combine(h, ys, w_gate, w_os, bm=512, bn=512):
    m, d = h.shape
    n = w_gate.shape[-1]
    bm, bn = _block(m, bm), _block(n, bn)
    row = lambda w: pl.BlockSpec((bm, w), lambda i, j: (i, 0))
    col = lambda k: pl.BlockSpec((k, bn), lambda i, j: (0, j))
    return pl.pallas_call(
        _gate_combine_kernel,
        grid=(m // bm, n // bn),
        in_specs=[row(d)] + [row(y.shape[1]) for y in ys]
                 + [pl.BlockSpec((3, d, bn), lambda i, j: (0, 0, j))] + [col(w.shape[0]) for w in w_os],
        out_specs=pl.BlockSpec((bm, bn), lambda i, j: (i, j)),
        out_shape=jax.ShapeDtypeStruct((m, n), BF16),
        compiler_params=_params("parallel", "parallel"),
        name="gate_combine",
    )(h, *ys, w_gate, *w_os)


def _rope_tables_kernel(pos_ref, invf_ref, sgn_ref, cos_ref, sin_ref):
    pos = pos_ref[...].astype(F32)
    for t in range(invf_ref.shape[0]):
        ang = pos * invf_ref[t:t + 1, :]
        cos_ref[t] = jnp.cos(ang)
        sin_ref[t] = jnp.sin(ang) * sgn_ref[t:t + 1, :]


def _rope_lane_tables(rot, period):
    half = rot // 2
    inv_freq = ROPE_THETA ** (-jnp.arange(half, dtype=F32) / half)
    lane = jnp.arange(LANES) % period
    invf = jnp.where(lane < rot, inv_freq[lane % half], 0.0)
    sgn = jnp.where(lane < half, -1.0, jnp.where(lane < rot, 1.0, 0.0))
    return invf.astype(F32), sgn.astype(F32)


def _rope_tables(positions):
    t = positions.size
    specs = [_rope_lane_tables(MLA_ROPE, LANES), _rope_lane_tables(DSA_ROT, LANES),
             _rope_lane_tables(IDX_ROT, IDX_DIM)]
    invf = jnp.stack([s[0] for s in specs])
    sgn = jnp.stack([s[1] for s in specs])
    bm = _block(t, 1024)
    n = invf.shape[0]
    const = pl.BlockSpec((n, LANES), lambda i: (0, 0))
    out = pl.BlockSpec((n, bm, LANES), lambda i: (0, i, 0))
    return pl.pallas_call(
        _rope_tables_kernel,
        grid=(t // bm,),
        in_specs=[pl.BlockSpec((bm, 1), lambda i: (i, 0)), const, const],
        out_specs=[out, out],
        out_shape=[jax.ShapeDtypeStruct((n, t, LANES), F32)] * 2,
        compiler_params=_params("parallel"),
        name="rope_tables",
    )(positions.reshape(t, 1), invf, sgn)


def _rope_tile(x, cos, sin, half, period):
    lane = lax.broadcasted_iota(jnp.int32, x.shape, 1) & (period - 1)
    partner = jnp.where(lane < half, pltpu.roll(x, LANES - half, 1), pltpu.roll(x, half, 1))
    return x * cos + partner * sin


def _mla_q_kernel(cq_ref, gcq_ref, w_ref, gq_ref, cos_ref, sin_ref, o_ref):
    x = cq_ref[...]
    xn = (x * _rms_scale(x, x.shape[-1]) * gcq_ref[...]).astype(BF16)
    q = _dot(xn, w_ref[...])
    cos, sin = cos_ref[0], sin_ref[0]
    g = gq_ref[...] * _query_premultiplier(MLA_QK)
    for h in range(MLA_HEADS):
        lo = h * MLA_QK_PAD
        qh = q[:, lo:lo + MLA_QK_PAD]
        y = qh * _rms_scale(qh, MLA_QK) * g
        o_ref[:, lo:lo + LANES] = _rope_tile(y[:, :LANES], cos, sin, MLA_ROPE // 2, LANES).astype(o_ref.dtype)
        o_ref[:, lo + LANES:lo + MLA_QK_PAD] = y[:, LANES:].astype(o_ref.dtype)


def _store_heads_transposed(x, vt_ref):
    heads, hd, _ = vt_ref.shape
    for h in range(heads):
        vt_ref[h] = x[:, h * hd:(h + 1) * hd].T.astype(vt_ref.dtype)


def _mla_kv_kernel(ckv_ref, kr_ref, gckv_ref, wk_ref, wv_ref, gk_ref, cos_ref, sin_ref, k_ref, vt_ref):
    x = ckv_ref[...]
    xn = (x * _rms_scale(x, x.shape[-1]) * gckv_ref[...]).astype(BF16)
    k_nope = _dot(xn, wk_ref[...])
    _store_heads_transposed(_dot(xn, wv_ref[...]), vt_ref)
    kr = kr_ref[...]
    lane = lax.broadcasted_iota(jnp.int32, kr.shape, 1)
    kr = jnp.where(lane < MLA_ROPE, kr, 0.0)
    cos, sin = cos_ref[0], sin_ref[0]
    g = gk_ref[...]
    for h in range(MLA_HEADS):
        lo = h * MLA_QK_PAD
        ka = k_nope[:, lo:lo + LANES] + kr
        kb = k_nope[:, lo + LANES:lo + MLA_QK_PAD]
        ms = (jnp.sum(ka * ka, axis=-1, keepdims=True) + jnp.sum(kb * kb, axis=-1, keepdims=True)) * (1.0 / MLA_QK)
        r = lax.rsqrt(ms + EPS)
        ya = ka * r * g[:, :LANES]
        yb = kb * r * g[:, LANES:]
        k_ref[:, lo:lo + LANES] = _rope_tile(ya, cos, sin, MLA_ROPE // 2, LANES).astype(k_ref.dtype)
        k_ref[:, lo + LANES:lo + MLA_QK_PAD] = yb.astype(k_ref.dtype)


def _pad_last(a, width):
    return jnp.pad(a, [(0, 0)] * (a.ndim - 1) + [(0, width - a.shape[-1])])


def _mla_prep(proj, col, g_cq, w_uq, g_ckv, w_ukv, g_q, g_k, cos, sin, batch):
    t = proj.shape[0]
    q_rank, kv_rank = w_uq.shape[0], w_ukv.shape[0]
    hp = MLA_HEADS * MLA_QK_PAD
    w_q = _pad_last(w_uq.reshape(q_rank, MLA_HEADS, MLA_QK), MLA_QK_PAD).reshape(q_rank, hp).astype(BF16)
    w_kv = w_ukv.reshape(kv_rank, MLA_HEADS, MLA_NOPE + MLA_V)
    w_k = jnp.pad(w_kv[..., :MLA_NOPE], ((0, 0), (0, 0), (MLA_ROPE, MLA_QK_PAD - MLA_QK)))
    w_k = w_k.reshape(kv_rank, hp).astype(BF16)
    w_v = w_kv[..., MLA_NOPE:].reshape(kv_rank, MLA_HEADS * MLA_V).astype(BF16)
    g_qp = _pad_last(g_q, MLA_QK_PAD).reshape(1, MLA_QK_PAD)
    g_kp = _pad_last(g_k, MLA_QK_PAD).reshape(1, MLA_QK_PAD)

    bm = _block(t, 512)
    rows = lambda w, c: pl.BlockSpec((bm, w), lambda i, c=c: (i, c))
    full = lambda a: pl.BlockSpec(a.shape, lambda i: (0,) * a.ndim)
    table = pl.BlockSpec((1, bm, LANES), lambda i: (0, i, 0))
    g_cq2, g_ckv2 = g_cq.reshape(1, -1), g_ckv.reshape(1, -1)

    q = pl.pallas_call(
        _mla_q_kernel,
        grid=(t // bm,),
        in_specs=[rows(q_rank, col["c_q"] // q_rank), full(g_cq2), full(w_q), full(g_qp), table, table],
        out_specs=pl.BlockSpec((bm, hp), lambda i: (i, 0)),
        out_shape=jax.ShapeDtypeStruct((t, hp), BF16),
        compiler_params=_params("parallel"),
        name="mla_q_prep",
    )(proj, g_cq2, w_q, g_qp, cos, sin)

    k, v = pl.pallas_call(
        _mla_kv_kernel,
        grid=(t // bm,),
        in_specs=[rows(kv_rank, col["c_kv"] // kv_rank), rows(LANES, col["k_r"] // LANES), full(g_ckv2),
                  full(w_k), full(w_v), full(g_kp), table, table],
        out_specs=[pl.BlockSpec((bm, hp), lambda i: (i, 0)),
                   _heads_transposed_spec(MLA_HEADS, MLA_V, bm, t // batch)],
        out_shape=[jax.ShapeDtypeStruct((t, hp), BF16),
                   jax.ShapeDtypeStruct((batch, MLA_HEADS, MLA_V, t // batch), BF16)],
        compiler_params=_params("parallel"),
        name="mla_kv_prep",
    )(proj, proj, g_ckv2, w_k, w_v, g_kp, cos, sin)
    return q, k, v


ATTN_HEADS_PER_STEP = 2


def _attn_scratch(tq, dv):
    per_head = [pltpu.VMEM((1, tq), F32), pltpu.VMEM((1, tq), F32), pltpu.VMEM((dv, tq), F32)]
    per_parity = [pltpu.VMEM((tq, tq), F32), pltpu.VMEM((1, tq), F32),
                  pltpu.VMEM((tq, tq), BF16), pltpu.VMEM((1, tq), F32)]
    return (per_head + 2 * per_parity) * ATTN_HEADS_PER_STEP


def _attn_kernel(q_ref, k_ref, vt_ref, *rest, tq, dk, dv, selected_keys):
    n_scratch = 11 * ATTN_HEADS_PER_STEP
    scratch = rest[-n_scratch:]
    sel_ref = rest[0] if selected_keys else None
    o_ref = rest[-n_scratch - 1]
    i = pl.program_id(2)
    heads = range(ATTN_HEADS_PER_STEP)
    m_of, l_of, acc_of, s_of, mx_of, p_of, al_of = [], [], [], [], [], [], []
    for h in heads:
        refs = scratch[11 * h:11 * (h + 1)]
        m_of.append(refs[0]); l_of.append(refs[1]); acc_of.append(refs[2])
        s_of.append((refs[3], refs[7])); mx_of.append((refs[4], refs[8]))
        p_of.append((refs[5], refs[9])); al_of.append((refs[6], refs[10]))
    qs = [q_ref[:, h * dk:(h + 1) * dk] for h in heads]

    for h in heads:
        m_of[h][...] = jnp.full((1, tq), NEG_INF, F32)
        l_of[h][...] = jnp.zeros((1, tq), F32)
        acc_of[h][...] = jnp.zeros((dv, tq), F32)
        s_of[h][1][...] = jnp.full((tq, tq), NEG_INF, F32)
        mx_of[h][1][...] = jnp.full((1, tq), NEG_INF, F32)
        p_of[h][0][...] = jnp.zeros((tq, tq), BF16)
        al_of[h][0][...] = jnp.zeros((1, tq), F32)

    def scores(block, par, diagonal):
        off = pl.multiple_of(block * tq, tq)
        if selected_keys:
            visible = sel_ref[pl.ds(off, tq), :].astype(F32) > 0.0
        elif diagonal:
            key_chunk = _chunk_of(lax.broadcasted_iota(jnp.int32, (tq, tq), 0))
            query_chunk = _chunk_of(lax.broadcasted_iota(jnp.int32, (tq, tq), 1))
            visible = key_chunk <= query_chunk
        else:
            visible = None
        for h in heads:
            s = _dot_nt(k_ref[pl.ds(off, tq), h * dk:(h + 1) * dk], qs[h])
            if visible is not None:
                s = jnp.where(visible, s, NEG_INF)
            s_of[h][par][...] = s
            mx_of[h][par][...] = jnp.max(s, axis=0, keepdims=True)

    def softmax(par):
        for h in heads:
            m_prev = m_of[h][...]
            m_new = jnp.maximum(m_prev, mx_of[h][par][...])
            m_safe = jnp.where(m_new == NEG_INF, 0.0, m_new)
            alpha = jnp.exp2(m_prev - m_safe)
            p = jnp.exp2(s_of[h][par][...] - m_safe)
            l_of[h][...] = alpha * l_of[h][...] + jnp.sum(p, axis=0, keepdims=True)
            p_of[h][par][...] = p.astype(BF16)
            al_of[h][par][...] = alpha
            m_of[h][...] = m_new

    def values(block, par):
        off = pl.multiple_of(jnp.maximum(block, 0) * tq, tq)
        for h in heads:
            acc_of[h][...] = al_of[h][par][...] * acc_of[h][...] + _dot(vt_ref[h, :, pl.ds(off, tq)],
                                                                         p_of[h][par][...])

    def step(t, par, diagonal=False):
        values(t - 2, par)
        softmax(1 - par)
        scores(t, par, diagonal)

    def pair(u, carry):
        step(2 * u, 0)
        step(2 * u + 1, 1)
        return carry

    lax.fori_loop(0, lax.shift_right_logical(i, 1), pair, 0)

    def drain(par):
        step(i, par, diagonal=True)
        values(i - 1, 1 - par)
        softmax(par)
        values(i, par)

    odd = (i & 1) == 1

    @pl.when(odd)
    def _():
        step(i - 1, 0)
        drain(1)

    @pl.when(jnp.logical_not(odd))
    def _():
        drain(0)

    for h in heads:
        o = acc_of[h][...] / l_of[h][...]
        o_ref[:, h * dv:(h + 1) * dv] = o.T.astype(o_ref.dtype)


def _attention(q, k, vt, selected_t, batch, heads, dk, dv, name, tq=256):
    t = q.shape[0]
    s = t // batch
    tq = _block(s, tq)
    nq = s // tq
    hp = ATTN_HEADS_PER_STEP
    in_specs = [pl.BlockSpec((tq, hp * dk), lambda b, h, i: (b * nq + i, h)),
                pl.BlockSpec((s, hp * dk), lambda b, h, i: (b, h)),
                pl.BlockSpec((None, hp, dv, s), lambda b, h, i: (b, h, 0, 0))]
    operands = [q, k, vt]
    if selected_t is not None:
        in_specs.append(pl.BlockSpec((None, s, tq), lambda b, h, i: (b, 0, i)))
        operands.append(selected_t)
    return pl.pallas_call(
        functools.partial(_attn_kernel, tq=tq, dk=dk, dv=dv, selected_keys=selected_t is not None),
        grid=(batch, heads // hp, nq),
        in_specs=in_specs,
        out_specs=pl.BlockSpec((tq, hp * dv), lambda b, h, i: (b * nq + i, h)),
        out_shape=jax.ShapeDtypeStruct((t, heads * dv), BF16),
        scratch_shapes=_attn_scratch(tq, dv),
        compiler_params=_params("parallel", "parallel", "parallel"),
        name=name,
    )(*operands)


def _heads_transposed_spec(heads, hd, bm, seq):
    blocks_per_batch = seq // bm
    return pl.BlockSpec((None, heads, hd, bm), lambda i: (i // blocks_per_batch, 0, 0, i % blocks_per_batch))


def _heads_transposed_kernel(x_ref, vt_ref):
    _store_heads_transposed(x_ref[...], vt_ref)


def _heads_transposed(proj, col_block, batch, heads, hd):
    t = proj.shape[0]
    bm = _block(t // batch, 512)
    return pl.pallas_call(
        _heads_transposed_kernel,
        grid=(t // bm,),
        in_specs=[pl.BlockSpec((bm, heads * hd), lambda i: (i, col_block))],
        out_specs=_heads_transposed_spec(heads, hd, bm, t // batch),
        out_shape=jax.ShapeDtypeStruct((batch, heads, hd, t // batch), BF16),
        compiler_params=_params("parallel"),
        name="heads_transposed",
    )(proj)


def _head_norm_rope_kernel(x_ref, g_ref, cos_ref, sin_ref, o_ref, *, heads, premultiplier):
    cos, sin = cos_ref[0], sin_ref[0]
    g = g_ref[...] * premultiplier
    for h in range(heads):
        x = x_ref[:, h * LANES:(h + 1) * LANES]
        y = x * _rms_scale(x, LANES) * g
        o_ref[:, h * LANES:(h + 1) * LANES] = _rope_tile(y, cos, sin, DSA_ROT // 2, LANES).astype(o_ref.dtype)


def _head_norm_rope(proj, col_block, g, cos, sin, premultiplier=1.0):
    t = proj.shape[0]
    width = DSA_HEADS * DSA_HEAD_DIM
    bm = _block(t, 512)
    table = pl.BlockSpec((1, bm, LANES), lambda i: (1, i, 0))
    return pl.pallas_call(
        functools.partial(_head_norm_rope_kernel, heads=DSA_HEADS, premultiplier=premultiplier),
        grid=(t // bm,),
        in_specs=[pl.BlockSpec((bm, width), lambda i: (i, col_block)),
                  pl.BlockSpec((1, LANES), lambda i: (0, 0)), table, table],
        out_specs=pl.BlockSpec((bm, width), lambda i: (i, 0)),
        out_shape=jax.ShapeDtypeStruct((t, width), BF16),
        compiler_params=_params("parallel"),
        name="head_norm_rope",
    )(proj, g.reshape(1, LANES), cos, sin)


def _rope_only_kernel(x_ref, cos_ref, sin_ref, o_ref):
    cos, sin = cos_ref[0], sin_ref[0]
    for c in range(x_ref.shape[1] // LANES):
        x = x_ref[:, c * LANES:(c + 1) * LANES]
        o_ref[:, c * LANES:(c + 1) * LANES] = _rope_tile(x, cos, sin, IDX_ROT // 2, IDX_DIM).astype(o_ref.dtype)


def _rope_only(proj, width, col_block, cos, sin):
    t = proj.shape[0]
    bm = _block(t, 512)
    table = pl.BlockSpec((1, bm, LANES), lambda i: (2, i, 0))
    return pl.pallas_call(
        _rope_only_kernel,
        grid=(t // bm,),
        in_specs=[pl.BlockSpec((bm, width), lambda i: (i, col_block)), table, table],
        out_specs=pl.BlockSpec((bm, width), lambda i: (i, 0)),
        out_shape=jax.ShapeDtypeStruct((t, width), BF16),
        compiler_params=_params("parallel"),
        name="indexer_rope",
    )(proj, cos, sin)


COUNT_ACC_ROWS = 64


def _sortable_key(x):
    bits = lax.bitcast_convert_type(x, jnp.int32)
    return bits ^ ((bits >> 31) & 0x7FFFFFFF)


def _indexer_kernel(qi_ref, kit_ref, w_ref, sel_ref, key_ref, wb_ref, qh_ref, *, tq, tk, topk, seq, w_scale):
    i = pl.program_id(1)
    lanes_per_tile = tk // LANES
    n_tiles = pl.cdiv((i + 1) * tq, tk)
    row_chunk = _chunk_of(i * tq + lax.broadcasted_iota(jnp.int32, (tq, 1), 0))

    w = w_ref[...] * w_scale
    for h in range(IDX_HEADS):
        wb_ref[h] = jnp.broadcast_to(w[:, h:h + 1], (tq, LANES))
        qh_ref[h] = qi_ref[:, h * IDX_DIM:(h + 1) * IDX_DIM]

    def score_tile(j, carry):
        off = pl.multiple_of(j * tk, tk)
        kt = kit_ref[:, pl.ds(off, tk)]
        acc = [jnp.zeros((tq, LANES), F32) for _ in range(lanes_per_tile)]
        for h in range(IDX_HEADS):
            logits = _dot(qh_ref[h], kt)
            wh = wb_ref[h]
            for c in range(lanes_per_tile):
                acc[c] = acc[c] + jnp.maximum(logits[:, c * LANES:(c + 1) * LANES], 0.0) * wh
        for c in range(lanes_per_tile):
            col = off + c * LANES + lax.broadcasted_iota(jnp.int32, (1, LANES), 1)
            score = jnp.where(_chunk_of(col) <= row_chunk, acc[c], NEG_INF)
            key_ref[pl.ds(off + c * LANES, LANES), :] = _sortable_key(score.T)
        return carry

    lax.fori_loop(0, n_tiles, score_tile, 0)

    def count(pred):
        rows = min(tk, COUNT_ACC_ROWS)

        def tile(j, acc):
            off = pl.multiple_of(j * tk, tk)
            hit = jnp.where(pred(key_ref[pl.ds(off, tk), :], off), 1.0, 0.0)
            return acc + jnp.sum(hit.reshape(tk // rows, rows, tq), axis=0)
        acc = lax.fori_loop(0, n_tiles, tile, jnp.zeros((rows, tq), F32))
        return jnp.sum(acc, axis=0, keepdims=True)

    def value_bit(b, lo):
        cand = lo + lax.shift_left(jnp.int32(1), 31 - b)
        n_ge = count(lambda kt, _: kt >= cand)
        return jnp.where(n_ge >= topk, cand, lo)

    thr = lax.fori_loop(0, 32, value_bit, jnp.full((1, tq), INT_MIN, jnp.int32))
    n_ge = count(lambda kt, _: kt >= thr)
    surplus = jnp.logical_and(n_ge > topk, thr > NEG_INF_KEY)

    def write_tile(j, carry):
        off = pl.multiple_of(j * tk, tk)
        kt = key_ref[pl.ds(off, tk), :]
        chosen = jnp.logical_and(kt >= thr, kt > NEG_INF_KEY)
        sel_ref[pl.ds(off, tk), :] = jnp.where(chosen, 1.0, 0.0).astype(sel_ref.dtype)
        return carry

    lax.fori_loop(0, n_tiles, write_tile, 0)

    def clear_tile(j, carry):
        off = pl.multiple_of(j * tk, tk)
        sel_ref[pl.ds(off, tk), :] = jnp.zeros((tk, tq), sel_ref.dtype)
        return carry

    lax.fori_loop(n_tiles, seq // tk, clear_tile, 0)

    @pl.when(jnp.max(jnp.where(surplus, 1.0, 0.0)) > 0.0)
    def _():
        n_gt = count(lambda kt, _: kt > thr)
        need = topk - n_gt

        def position(first_key):
            return first_key + lax.broadcasted_iota(jnp.int32, (tk, tq), 0)

        def position_bit(b, base):
            cand = base + lax.shift_left(jnp.int32(1), (seq - 1).bit_length() - 1 - b)
            n_before = count(lambda kt, k0: jnp.logical_and(kt == thr, position(k0) < cand))
            return jnp.where(n_before < need, cand, base)

        last = lax.fori_loop(0, (seq - 1).bit_length(), position_bit, jnp.zeros((1, tq), jnp.int32))

        def rewrite_tile(j, carry):
            off = pl.multiple_of(j * tk, tk)
            kt = key_ref[pl.ds(off, tk), :]
            chosen = jnp.logical_or(kt > thr, jnp.logical_and(kt == thr, position(off) <= last))
            chosen = jnp.logical_and(chosen, kt > NEG_INF_KEY)
            sel_ref[pl.ds(off, tk), :] = jnp.where(chosen, 1.0, 0.0).astype(sel_ref.dtype)
            return carry

        lax.fori_loop(0, n_tiles, rewrite_tile, 0)


def _indexer_select(qi, kit, w, w_col_block, batch, topk, tq=128, tk=256):
    s = kit.shape[2]
    tq = _block(s, tq)
    tk = _block(s, tk)
    assert tq == LANES, "score tiles are transposed one (LANES, LANES) tile at a time"
    nq = s // tq
    w_scale = (IDX_DIM ** -0.5) * (IDX_HEADS ** -0.5)
    return pl.pallas_call(
        functools.partial(_indexer_kernel, tq=tq, tk=tk, topk=topk, seq=s, w_scale=w_scale),
        grid=(batch, nq),
        in_specs=[pl.BlockSpec((tq, IDX_HEADS * IDX_DIM), lambda b, i: (b * nq + i, 0)),
                  pl.BlockSpec((None, IDX_DIM, s), lambda b, i: (b, 0, 0)),
                  pl.BlockSpec((tq, LANES), lambda b, i: (b * nq + i, w_col_block))],
        out_specs=pl.BlockSpec((None, s, tq), lambda b, i: (b, 0, i)),
        out_shape=jax.ShapeDtypeStruct((batch, s, s), BF16),
        scratch_shapes=[pltpu.VMEM((s, tq), jnp.int32), pltpu.VMEM((IDX_HEADS, tq, LANES), F32),
                        pltpu.VMEM((IDX_HEADS, tq, IDX_DIM), BF16)],
        compiler_params=_params("parallel", "parallel"),
        name="indexer_select",
    )(qi, kit, w)


def _memory_attn_kernel(q_ref, kv_ref, gq_ref, gk_ref, o_ref, *, scale):
    hd = MEM_HEAD_DIM
    gq, gk = gq_ref[...], gk_ref[...]
    for h in range(MEM_HEADS):
        q = q_ref[:, h * hd:(h + 1) * hd]
        k = kv_ref[:, h * hd:(h + 1) * hd]
        v = kv_ref[:, (MEM_HEADS + h) * hd:(MEM_HEADS + h + 1) * hd].astype(BF16)
        qn = (q * _rms_scale(q, hd) * gq).astype(BF16)
        kn = (k * _rms_scale(k, hd) * gk).astype(BF16)
        s = _dot_nt(qn, kn) * scale
        e = jnp.exp(s - jnp.max(s, axis=-1, keepdims=True))
        p = e / jnp.sum(e, axis=-1, keepdims=True)
        o_ref[:, h * hd:(h + 1) * hd] = _dot(p.astype(BF16), v).astype(o_ref.dtype)


def _memory_attention(proj, col_block, kv, g_q, g_k, batch, tq=512):
    t = proj.shape[0]
    s = t // batch
    m = kv.shape[0] // batch
    width = MEM_HEADS * MEM_HEAD_DIM
    tq = _block(s, tq)
    nq = s // tq
    g_spec = pl.BlockSpec((1, MEM_HEAD_DIM), lambda b, i: (0, 0))
    return pl.pallas_call(
        functools.partial(_memory_attn_kernel, scale=MEM_HEAD_DIM ** -0.5),
        grid=(batch, nq),
        in_specs=[pl.BlockSpec((tq, width), lambda b, i: (b * nq + i, col_block)),
                  pl.BlockSpec((m, 2 * width), lambda b, i: (b, 0)), g_spec, g_spec],
        out_specs=pl.BlockSpec((tq, width), lambda b, i: (b * nq + i, 0)),
        out_shape=jax.ShapeDtypeStruct((t, width), BF16),
        compiler_params=_params("parallel", "parallel"),
        name="memory_attention",
    )(proj, kv, g_q.reshape(1, -1), g_k.reshape(1, -1))


def _projection_layout(q_rank, kv_rank):
    dsa = DSA_HEADS * DSA_HEAD_DIM
    sizes = [("c_q", q_rank), ("c_kv", kv_rank), ("k_r", MLA_ROPE), ("q_d", dsa), ("k_d", dsa), ("v_d", dsa),
             ("q_i", IDX_HEADS * IDX_DIM), ("k_i", IDX_DIM), ("w_i", IDX_HEADS), ("q_m", MEM_HEADS * MEM_HEAD_DIM)]
    src, start = {}, 0
    for name, n in sizes:
        src[name] = (start, n)
        start += n
    return src, start


def _hybrid_mixer(x, h, mem, positions, w_in, g_cq, w_uq, g_ckv, w_ukv, g_q_mla, g_k_mla, g_q_dsa, g_k_dsa,
                  g_mem, w_mem_kv, g_q_mem, g_k_mem, w_o_mla, w_o_dsa, w_o_mem, w_gate, w_out, batch):
    t, d = h.shape
    s = t // batch
    q_rank, kv_rank = w_uq.shape[0], w_ukv.shape[0]
    src, d_in = _projection_layout(q_rank, kv_rank)
    assert d_in == w_in.shape[1]
    dsa = DSA_HEADS * DSA_HEAD_DIM
    assert q_rank == dsa, "the reordered projection layout assumes equal widths for these groups"

    take = lambda name: w_in[:, src[name][0]:src[name][0] + src[name][1]]
    pieces = [("c_q", take("c_q")), ("q_d", take("q_d")), ("k_d", take("k_d")), ("v_d", take("v_d")),
              ("q_i", take("q_i")), ("q_m", take("q_m")), ("c_kv", take("c_kv")),
              ("k_r", jnp.concatenate([take("k_r"), take("k_i")], axis=1)),
              ("w_i", _pad_last(take("w_i"), LANES))]
    col, start = {}, 0
    for name, piece in pieces:
        assert start % piece.shape[1] == 0, name
        col[name] = start
        start += piece.shape[1]
    w_in_r = jnp.concatenate([p for _, p in pieces], axis=1).astype(BF16)

    proj = _matmul(h, w_in_r, F32, bm=1024, bn=768, name="input_projection")
    cos, sin = _rope_tables(positions)

    q_a, k_a, vt_a = _mla_prep(proj, col, g_cq, w_uq, g_ckv, w_ukv, g_q_mla, g_k_mla, cos, sin, batch)
    y_mla = _attention(q_a, k_a, vt_a, None, batch, MLA_HEADS,
                       MLA_QK_PAD, MLA_V, "chunk_causal_attention")

    q_b = _head_norm_rope(proj, col["q_d"] // dsa, g_q_dsa, cos, sin, _query_premultiplier(DSA_HEAD_DIM))
    k_b = _head_norm_rope(proj, col["k_d"] // dsa, g_k_dsa, cos, sin)
    vt_b = _heads_transposed(proj, col["v_d"] // dsa, batch, DSA_HEADS, DSA_HEAD_DIM)
    qi_width = IDX_HEADS * IDX_DIM
    qi = _rope_only(proj, qi_width, col["q_i"] // qi_width, cos, sin)
    ki = _rope_only(proj, LANES, col["k_r"] // LANES, cos, sin)[:, IDX_DIM:]
    kit = ki.reshape(batch, s, IDX_DIM).transpose(0, 2, 1)
    topk = min(DSA_MAX_TOPK, s // 4)
    selected_t = _indexer_select(qi, kit, proj, col["w_i"] // LANES, batch, topk)
    y_dsa = _attention(q_b, k_b, vt_b, selected_t, batch, DSA_HEADS,
                       DSA_HEAD_DIM, DSA_HEAD_DIM, "masked_attention")

    m_len = mem.shape[1]
    mem_n = _rmsnorm(mem.reshape(batch * m_len, d), g_mem)
    kv_mem = _matmul(mem_n, w_mem_kv.astype(BF16), F32, bm=512, bn=512, name="memory_kv")
    y_mem = _memory_attention(proj, col["q_m"] // (MEM_HEADS * MEM_HEAD_DIM), kv_mem, g_q_mem, g_k_mem, batch)

    gated = _gate_combine(h, (y_mla, y_dsa, y_mem), w_gate.astype(BF16),
                          (w_o_mla.astype(BF16), w_o_dsa.astype(BF16), w_o_mem.astype(BF16)))
    return _matmul_residual(gated, w_out.astype(BF16), x, 1.0, name="mixer_out")


def kernel(x, mem, positions, g_ff1, w_ff1_gate, w_ff1_up, w_ff1_down, g_mix, w_in, g_cq, w_uq, g_ckv, w_ukv, g_q_mla, g_k_mla, g_q_dsa, g_k_dsa, g_mem, w_mem_kv, g_q_mem, g_k_mem, w_o_mla, w_o_dsa, w_o_mem, w_gate, w_out, g_ff2, w_ff2_gate, w_ff2_up, w_ff2_down):
    batch, seq, d = x.shape
    xt = x.reshape(batch * seq, d)
    for l in range(g_ff1.shape[0]):
        xt = _ffn_half_step(xt, g_ff1[l], w_ff1_gate[l], w_ff1_up[l], w_ff1_down[l])
        h = _rmsnorm(xt, g_mix[l])
        xt = _hybrid_mixer(xt, h, mem, positions, w_in[l], g_cq[l], w_uq[l], g_ckv[l], w_ukv[l],
                           g_q_mla[l], g_k_mla[l], g_q_dsa[l], g_k_dsa[l], g_mem[l], w_mem_kv[l],
                           g_q_mem[l], g_k_mem[l], w_o_mla[l], w_o_dsa[l], w_o_mem[l], w_gate[l], w_out[l], batch)
        xt = _ffn_half_step(xt, g_ff2[l], w_ff2_gate[l], w_ff2_up[l], w_ff2_down[l])
    return xt.reshape(batch, seq, d)
```

```python
import functools

import jax
import jax.numpy as jnp
from jax import lax
from jax.experimental import pallas as pl
from jax.experimental.pallas import tpu as pltpu

CHUNK = 64
ROPE_THETA = 500000.0
EPS = 1e-6

MLA_HEADS = 12
MLA_NOPE = 128
MLA_ROPE = 64
MLA_V = 128
MLA_QK = MLA_ROPE + MLA_NOPE
MLA_QK_PAD = 256

DSA_HEADS = 12
DSA_HEAD_DIM = 128
DSA_ROT = DSA_HEAD_DIM // 4
DSA_MAX_TOPK = 256
IDX_HEADS = 32
IDX_DIM = 64
IDX_ROT = IDX_DIM // 4

MEM_HEADS = 4
MEM_HEAD_DIM = 256

LANES = 128
SUBLANES = 8
VMEM_LIMIT_BYTES = 56 * 1024 * 1024

F32 = jnp.float32
BF16 = jnp.bfloat16
NEG_INF = float("-inf")
INT_MIN = -2 ** 31
LOG2_E = 1.4426950408889634
NEG_INF_KEY = (0xFF800000 - (1 << 32)) ^ 0x7FFFFFFF


def _params(*semantics):
    return pltpu.CompilerParams(dimension_semantics=semantics, vmem_limit_bytes=VMEM_LIMIT_BYTES)


def _block(dim, preferred):
    if dim <= preferred:
        return dim
    b = preferred - preferred % LANES
    while b >= LANES:
        if dim % b == 0:
            return b
        b -= LANES
    return dim


def _rms_scale(x, width):
    return lax.rsqrt(jnp.sum(x * x, axis=-1, keepdims=True) * (1.0 / width) + EPS)


def _dot(a, b):
    return jnp.dot(a, b, preferred_element_type=F32)


def _dot_nt(a, b):
    return lax.dot_general(a, b, (((1,), (1,)), ((), ())), preferred_element_type=F32)


def _sigmoid(x):
    return 1.0 / (1.0 + jnp.exp(-x))


def _query_premultiplier(head_dim):
    return head_dim ** -0.5 * LOG2_E


def _chunk_of(frame):
    return lax.shift_right_logical(frame, CHUNK.bit_length() - 1)


def _rmsnorm_kernel(x_ref, g_ref, o_ref):
    x = x_ref[...]
    o_ref[...] = (x * _rms_scale(x, x.shape[-1]) * g_ref[...]).astype(o_ref.dtype)


def _rmsnorm(x, g, out_dtype=BF16):
    m, d = x.shape
    bm = _block(m, 512)
    return pl.pallas_call(
        _rmsnorm_kernel,
        grid=(m // bm,),
        in_specs=[pl.BlockSpec((bm, d), lambda i: (i, 0)), pl.BlockSpec((1, d), lambda i: (0, 0))],
        out_specs=pl.BlockSpec((bm, d), lambda i: (i, 0)),
        out_shape=jax.ShapeDtypeStruct((m, d), out_dtype),
        compiler_params=_params("parallel"),
        name="rmsnorm",
    )(x, g.reshape(1, d))


def _matmul_kernel(a_ref, b_ref, o_ref):
    o_ref[...] = _dot(a_ref[...], b_ref[...]).astype(o_ref.dtype)


def _matmul(a, b, out_dtype, bm=1024, bn=512, name="matmul"):
    m, k = a.shape
    n = b.shape[1]
    bm, bn = _block(m, bm), _block(n, bn)
    return pl.pallas_call(
        _matmul_kernel,
        grid=(m // bm, n // bn),
        in_specs=[pl.BlockSpec((bm, k), lambda i, j: (i, 0)), pl.BlockSpec((k, bn), lambda i, j: (0, j))],
        out_specs=pl.BlockSpec((bm, bn), lambda i, j: (i, j)),
        out_shape=jax.ShapeDtypeStruct((m, n), out_dtype),
        compiler_params=_params("parallel", "parallel"),
        name=name,
    )(a, b)


def _swiglu_up_kernel(a_ref, wg_ref, wu_ref, o_ref):
    a = a_ref[...]
    g = _dot(a, wg_ref[...])
    u = _dot(a, wu_ref[...])
    o_ref[...] = (g * _sigmoid(g) * u).astype(o_ref.dtype)


def _swiglu_up(a, wg, wu, bm=1024, bn=512):
    m, k = a.shape
    n = wg.shape[1]
    bm, bn = _block(m, bm), _block(n, bn)
    return pl.pallas_call(
        _swiglu_up_kernel,
        grid=(m // bm, n // bn),
        in_specs=[pl.BlockSpec((bm, k), lambda i, j: (i, 0)),
                  pl.BlockSpec((k, bn), lambda i, j: (0, j)),
                  pl.BlockSpec((k, bn), lambda i, j: (0, j))],
        out_specs=pl.BlockSpec((bm, bn), lambda i, j: (i, j)),
        out_shape=jax.ShapeDtypeStruct((m, n), BF16),
        compiler_params=_params("parallel", "parallel"),
        name="swiglu_up",
    )(a, wg, wu)


def _matmul_residual_kernel(a_ref, b_ref, x_ref, o_ref, *, scale):
    o_ref[...] = x_ref[...] + scale * _dot(a_ref[...], b_ref[...])


def _matmul_residual(a, b, x, scale, bm=512, bn=512, name="matmul_residual"):
    m, k = a.shape
    n = b.shape[1]
    bm, bn = _block(m, bm), _block(n, bn)
    return pl.pallas_call(
        functools.partial(_matmul_residual_kernel, scale=scale),
        grid=(m // bm, n // bn),
        in_specs=[pl.BlockSpec((bm, k), lambda i, j: (i, 0)),
                  pl.BlockSpec((k, bn), lambda i, j: (0, j)),
                  pl.BlockSpec((bm, bn), lambda i, j: (i, j))],
        out_specs=pl.BlockSpec((bm, bn), lambda i, j: (i, j)),
        out_shape=jax.ShapeDtypeStruct((m, n), F32),
        compiler_params=_params("parallel", "parallel"),
        name=name,
    )(a, b, x)


def _ffn_half_step(x, g, w_gate, w_up, w_down):
    h = _rmsnorm(x, g)
    a = _swiglu_up(h, w_gate.astype(BF16), w_up.astype(BF16))
    return _matmul_residual(a, w_down.astype(BF16), x, 0.5, name="ffn_down")


def _gate_combine_kernel(h_ref, ya_ref, yb_ref, yc_ref, wg_ref, wa_ref, wb_ref, wc_ref, o_ref):
    h = h_ref[...]
    out = _sigmoid(_dot(h, wg_ref[0])) * _dot(ya_ref[...], wa_ref[...])
    out = out + _sigmoid(_dot(h, wg_ref[1])) * _dot(yb_ref[...], wb_ref[...])
    out = out + _sigmoid(_dot(h, wg_ref[2])) * _dot(yc_ref[...], wc_ref[...])
    o_ref[...] = out.astype(o_ref.dtype)


def _gate_combine(h, ys, w_gate, w_os, bm=512, bn=512):
    m, d = h.shape
    n = w_gate.shape[-1]
    bm, bn = _block(m, bm), _block(n, bn)
    row = lambda w: pl.BlockSpec((bm, w), lambda i, j: (i, 0))
    col = lambda k: pl.BlockSpec((k, bn), lambda i, j: (0, j))
    return pl.pallas_call(
        _gate_combine_kernel,
        grid=(m // bm, n // bn),
        in_specs=[row(d)] + [row(y.shape[1]) for y in ys]
                 + [pl.BlockSpec((3, d, bn), lambda i, j: (0, 0, j))] + [col(w.shape[0]) for w in w_os],
        out_specs=pl.BlockSpec((bm, bn), lambda i, j: (i, j)),
        out_shape=jax.ShapeDtypeStruct((m, n), BF16),
        compiler_params=_params("parallel", "parallel"),
        name="gate_combine",
    )(h, *ys, w_gate, *w_os)


def _rope_tables_kernel(pos_ref, invf_ref, sgn_ref, cos_ref, sin_ref):
    pos = pos_ref[...].astype(F32)
    for t in range(invf_ref.shape[0]):
        ang = pos * invf_ref[t:t + 1, :]
        cos_ref[t] = jnp.cos(ang)
        sin_ref[t] = jnp.sin(ang) * sgn_ref[t:t + 1, :]


def _rope_lane_tables(rot, period):
    half = rot // 2
    inv_freq = ROPE_THETA ** (-jnp.arange(half, dtype=F32) / half)
    lane = jnp.arange(LANES) % period
    invf = jnp.where(lane < rot, inv_freq[lane % half], 0.0)
    sgn = jnp.where(lane < half, -1.0, jnp.where(lane < rot, 1.0, 0.0))
    return invf.astype(F32), sgn.astype(F32)


def _rope_tables(positions):
    t = positions.size
    specs = [_rope_lane_tables(MLA_ROPE, LANES), _rope_lane_tables(DSA_ROT, LANES),
             _rope_lane_tables(IDX_ROT, IDX_DIM)]
    invf = jnp.stack([s[0] for s in specs])
    sgn = jnp.stack([s[1] for s in specs])
    bm = _block(t, 1024)
    n = invf.shape[0]
    const = pl.BlockSpec((n, LANES), lambda i: (0, 0))
    out = pl.BlockSpec((n, bm, LANES), lambda i: (0, i, 0))
    return pl.pallas_call(
        _rope_tables_kernel,
        grid=(t // bm,),
        in_specs=[pl.BlockSpec((bm, 1), lambda i: (i, 0)), const, const],
        out_specs=[out, out],
        out_shape=[jax.ShapeDtypeStruct((n, t, LANES), F32)] * 2,
        compiler_params=_params("parallel"),
        name="rope_tables",
    )(positions.reshape(t, 1), invf, sgn)


def _rope_tile(x, cos, sin, half, period):
    lane = lax.broadcasted_iota(jnp.int32, x.shape, 1) & (period - 1)
    partner = jnp.where(lane < half, pltpu.roll(x, LANES - half, 1), pltpu.roll(x, half, 1))
    return x * cos + partner * sin


def _mla_q_kernel(cq_ref, gcq_ref, w_ref, gq_ref, cos_ref, sin_ref, o_ref):
    x = cq_ref[...]
    xn = (x * _rms_scale(x, x.shape[-1]) * gcq_ref[...]).astype(BF16)
    q = _dot(xn, w_ref[...])
    cos, sin = cos_ref[0], sin_ref[0]
    g = gq_ref[...] * _query_premultiplier(MLA_QK)
    for h in range(MLA_HEADS):
        lo = h * MLA_QK_PAD
        qh = q[:, lo:lo + MLA_QK_PAD]
        y = qh * _rms_scale(qh, MLA_QK) * g
        o_ref[:, lo:lo + LANES] = _rope_tile(y[:, :LANES], cos, sin, MLA_ROPE // 2, LANES).astype(o_ref.dtype)
        o_ref[:, lo + LANES:lo + MLA_QK_PAD] = y[:, LANES:].astype(o_ref.dtype)


def _store_heads_transposed(x, vt_ref):
    heads, hd, _ = vt_ref.shape
    for h in range(heads):
        vt_ref[h] = x[:, h * hd:(h + 1) * hd].T.astype(vt_ref.dtype)


def _mla_kv_kernel(ckv_ref, kr_ref, gckv_ref, wk_ref, wv_ref, gk_ref, cos_ref, sin_ref, k_ref, vt_ref):
    x = ckv_ref[...]
    xn = (x * _rms_scale(x, x.shape[-1]) * gckv_ref[...]).astype(BF16)
    k_nope = _dot(xn, wk_ref[...])
    _store_heads_transposed(_dot(xn, wv_ref[...]), vt_ref)
    kr = kr_ref[...]
    lane = lax.broadcasted_iota(jnp.int32, kr.shape, 1)
    kr = jnp.where(lane < MLA_ROPE, kr, 0.0)
    cos, sin = cos_ref[0], sin_ref[0]
    g = gk_ref[...]
    for h in range(MLA_HEADS):
        lo = h * MLA_QK_PAD
        ka = k_nope[:, lo:lo + LANES] + kr
        kb = k_nope[:, lo + LANES:lo + MLA_QK_PAD]
        ms = (jnp.sum(ka * ka, axis=-1, keepdims=True) + jnp.sum(kb * kb, axis=-1, keepdims=True)) * (1.0 / MLA_QK)
        r = lax.rsqrt(ms + EPS)
        ya = ka * r * g[:, :LANES]
        yb = kb * r * g[:, LANES:]
        k_ref[:, lo:lo + LANES] = _rope_tile(ya, cos, sin, MLA_ROPE // 2, LANES).astype(k_ref.dtype)
        k_ref[:, lo + LANES:lo + MLA_QK_PAD] = yb.astype(k_ref.dtype)


def _pad_last(a, width):
    return jnp.pad(a, [(0, 0)] * (a.ndim - 1) + [(0, width - a.shape[-1])])


def _mla_prep(proj, col, g_cq, w_uq, g_ckv, w_ukv, g_q, g_k, cos, sin, batch):
    t = proj.shape[0]
    q_rank, kv_rank = w_uq.shape[0], w_ukv.shape[0]
    hp = MLA_HEADS * MLA_QK_PAD
    w_q = _pad_last(w_uq.reshape(q_rank, MLA_HEADS, MLA_QK), MLA_QK_PAD).reshape(q_rank, hp).astype(BF16)
    w_kv = w_ukv.reshape(kv_rank, MLA_HEADS, MLA_NOPE + MLA_V)
    w_k = jnp.pad(w_kv[..., :MLA_NOPE], ((0, 0), (0, 0), (MLA_ROPE, MLA_QK_PAD - MLA_QK)))
    w_k = w_k.reshape(kv_rank, hp).astype(BF16)
    w_v = w_kv[..., MLA_NOPE:].reshape(kv_rank, MLA_HEADS * MLA_V).astype(BF16)
    g_qp = _pad_last(g_q, MLA_QK_PAD).reshape(1, MLA_QK_PAD)
    g_kp = _pad_last(g_k, MLA_QK_PAD).reshape(1, MLA_QK_PAD)

    bm = _block(t, 512)
    rows = lambda w, c: pl.BlockSpec((bm, w), lambda i, c=c: (i, c))
    full = lambda a: pl.BlockSpec(a.shape, lambda i: (0,) * a.ndim)
    table = pl.BlockSpec((1, bm, LANES), lambda i: (0, i, 0))
    g_cq2, g_ckv2 = g_cq.reshape(1, -1), g_ckv.reshape(1, -1)

    q = pl.pallas_call(
        _mla_q_kernel,
        grid=(t // bm,),
        in_specs=[rows(q_rank, col["c_q"] // q_rank), full(g_cq2), full(w_q), full(g_qp), table, table],
        out_specs=pl.BlockSpec((bm, hp), lambda i: (i, 0)),
        out_shape=jax.ShapeDtypeStruct((t, hp), BF16),
        compiler_params=_params("parallel"),
        name="mla_q_prep",
    )(proj, g_cq2, w_q, g_qp, cos, sin)

    k, v = pl.pallas_call(
        _mla_kv_kernel,
        grid=(t // bm,),
        in_specs=[rows(kv_rank, col["c_kv"] // kv_rank), rows(LANES, col["k_r"] // LANES), full(g_ckv2),
                  full(w_k), full(w_v), full(g_kp), table, table],
        out_specs=[pl.BlockSpec((bm, hp), lambda i: (i, 0)),
                   _heads_transposed_spec(MLA_HEADS, MLA_V, bm, t // batch)],
        out_shape=[jax.ShapeDtypeStruct((t, hp), BF16),
                   jax.ShapeDtypeStruct((batch, MLA_HEADS, MLA_V, t // batch), BF16)],
        compiler_params=_params("parallel"),
        name="mla_kv_prep",
    )(proj, proj, g_ckv2, w_k, w_v, g_kp, cos, sin)
    return q, k, v


ATTN_HEADS_PER_STEP = 3


def _attn_scratch(tq, dv):
    per_head = [pltpu.VMEM((1, tq), F32), pltpu.VMEM((1, tq), F32), pltpu.VMEM((dv, tq), F32)]
    per_parity = [pltpu.VMEM((tq, tq), F32), pltpu.VMEM((1, tq), F32),
                  pltpu.VMEM((tq, tq), BF16), pltpu.VMEM((1, tq), F32)]
    return (per_head + 2 * per_parity) * ATTN_HEADS_PER_STEP


def _attn_kernel(q_ref, k_ref, vt_ref, *rest, tq, dk, dv, selected_keys):
    n_scratch = 11 * ATTN_HEADS_PER_STEP
    scratch = rest[-n_scratch:]
    sel_ref = rest[0] if selected_keys else None
    o_ref = rest[-n_scratch - 1]
    i = pl.program_id(2)
    heads = range(ATTN_HEADS_PER_STEP)
    m_of, l_of, acc_of, s_of, mx_of, p_of, al_of = [], [], [], [], [], [], []
    for h in heads:
        refs = scratch[11 * h:11 * (h + 1)]
        m_of.append(refs[0]); l_of.append(refs[1]); acc_of.append(refs[2])
        s_of.append((refs[3], refs[7])); mx_of.append((refs[4], refs[8]))
        p_of.append((refs[5], refs[9])); al_of.append((refs[6], refs[10]))
    qs = [q_ref[:, h * dk:(h + 1) * dk] for h in heads]

    for h in heads:
        m_of[h][...] = jnp.full((1, tq), NEG_INF, F32)
        l_of[h][...] = jnp.zeros((1, tq), F32)
        acc_of[h][...] = jnp.zeros((dv, tq), F32)
        s_of[h][1][...] = jnp.full((tq, tq), NEG_INF, F32)
        mx_of[h][1][...] = jnp.full((1, tq), NEG_INF, F32)
        p_of[h][0][...] = jnp.zeros((tq, tq), BF16)
        al_of[h][0][...] = jnp.zeros((1, tq), F32)

    def scores(block, par, diagonal):
        off = pl.multiple_of(block * tq, tq)
        if selected_keys:
            visible = sel_ref[pl.ds(off, tq), :].astype(F32) > 0.0
        elif diagonal:
            key_chunk = _chunk_of(lax.broadcasted_iota(jnp.int32, (tq, tq), 0))
            query_chunk = _chunk_of(lax.broadcasted_iota(jnp.int32, (tq, tq), 1))
            visible = key_chunk <= query_chunk
        else:
            visible = None
        for h in heads:
            s = _dot_nt(k_ref[pl.ds(off, tq), h * dk:(h + 1) * dk], qs[h])
            if visible is not None:
                s = jnp.where(visible, s, NEG_INF)
            s_of[h][par][...] = s
            mx_of[h][par][...] = jnp.max(s, axis=0, keepdims=True)

    def softmax(par):
        for h in heads:
            m_prev = m_of[h][...]
            m_new = jnp.maximum(m_prev, mx_of[h][par][...])
            m_safe = jnp.where(m_new == NEG_INF, 0.0, m_new)
            alpha = jnp.exp2(m_prev - m_safe)
            p = jnp.exp2(s_of[h][par][...] - m_safe)
            l_of[h][...] = alpha * l_of[h][...] + jnp.sum(p, axis=0, keepdims=True)
            p_of[h][par][...] = p.astype(BF16)
            al_of[h][par][...] = alpha
            m_of[h][...] = m_new

    def values(block, par):
        off = pl.multiple_of(jnp.maximum(block, 0) * tq, tq)
        for h in heads:
            acc_of[h][...] = al_of[h][par][...] * acc_of[h][...] + _dot(vt_ref[h, :, pl.ds(off, tq)],
                                                                         p_of[h][par][...])

    def step(t, par, diagonal=False):
        values(t - 2, par)
        softmax(1 - par)
        scores(t, par, diagonal)

    def pair(u, carry):
        step(2 * u, 0)
        step(2 * u + 1, 1)
        return carry

    lax.fori_loop(0, lax.shift_right_logical(i, 1), pair, 0)

    def drain(par):
        step(i, par, diagonal=True)
        values(i - 1, 1 - par)
        softmax(par)
        values(i, par)

    odd = (i & 1) == 1

    @pl.when(odd)
    def _():
        step(i - 1, 0)
        drain(1)

    @pl.when(jnp.logical_not(odd))
    def _():
        drain(0)

    for h in heads:
        o = acc_of[h][...] / l_of[h][...]
        o_ref[:, h * dv:(h + 1) * dv] = o.T.astype(o_ref.dtype)


def _attention(q, k, vt, selected_t, batch, heads, dk, dv, name, tq=256):
    t = q.shape[0]
    s = t // batch
    tq = _block(s, tq)
    nq = s // tq
    hp = ATTN_HEADS_PER_STEP
    in_specs = [pl.BlockSpec((tq, hp * dk), lambda b, h, i: (b * nq + i, h)),
                pl.BlockSpec((s, hp * dk), lambda b, h, i: (b, h)),
                pl.BlockSpec((None, hp, dv, s), lambda b, h, i: (b, h, 0, 0))]
    operands = [q, k, vt]
    if selected_t is not None:
        in_specs.append(pl.BlockSpec((None, s, tq), lambda b, h, i: (b, 0, i)))
        operands.append(selected_t)
    return pl.pallas_call(
        functools.partial(_attn_kernel, tq=tq, dk=dk, dv=dv, selected_keys=selected_t is not None),
        grid=(batch, heads // hp, nq),
        in_specs=in_specs,
        out_specs=pl.BlockSpec((tq, hp * dv), lambda b, h, i: (b * nq + i, h)),
        out_shape=jax.ShapeDtypeStruct((t, heads * dv), BF16),
        scratch_shapes=_attn_scratch(tq, dv),
        compiler_params=_params("parallel", "parallel", "parallel"),
        name=name,
    )(*operands)


def _heads_transposed_spec(heads, hd, bm, seq):
    blocks_per_batch = seq // bm
    return pl.BlockSpec((None, heads, hd, bm), lambda i: (i // blocks_per_batch, 0, 0, i % blocks_per_batch))


def _heads_transposed_kernel(x_ref, vt_ref):
    _store_heads_transposed(x_ref[...], vt_ref)


def _heads_transposed(proj, col_block, batch, heads, hd):
    t = proj.shape[0]
    bm = _block(t // batch, 512)
    return pl.pallas_call(
        _heads_transposed_kernel,
        grid=(t // bm,),
        in_specs=[pl.BlockSpec((bm, heads * hd), lambda i: (i, col_block))],
        out_specs=_heads_transposed_spec(heads, hd, bm, t // batch),
        out_shape=jax.ShapeDtypeStruct((batch, heads, hd, t // batch), BF16),
        compiler_params=_params("parallel"),
        name="heads_transposed",
    )(proj)


def _head_norm_rope_kernel(x_ref, g_ref, cos_ref, sin_ref, o_ref, *, heads, premultiplier):
    cos, sin = cos_ref[0], sin_ref[0]
    g = g_ref[...] * premultiplier
    for h in range(heads):
        x = x_ref[:, h * LANES:(h + 1) * LANES]
        y = x * _rms_scale(x, LANES) * g
        o_ref[:, h * LANES:(h + 1) * LANES] = _rope_tile(y, cos, sin, DSA_ROT // 2, LANES).astype(o_ref.dtype)


def _head_norm_rope(proj, col_block, g, cos, sin, premultiplier=1.0):
    t = proj.shape[0]
    width = DSA_HEADS * DSA_HEAD_DIM
    bm = _block(t, 512)
    table = pl.BlockSpec((1, bm, LANES), lambda i: (1, i, 0))
    return pl.pallas_call(
        functools.partial(_head_norm_rope_kernel, heads=DSA_HEADS, premultiplier=premultiplier),
        grid=(t // bm,),
        in_specs=[pl.BlockSpec((bm, width), lambda i: (i, col_block)),
                  pl.BlockSpec((1, LANES), lambda i: (0, 0)), table, table],
        out_specs=pl.BlockSpec((bm, width), lambda i: (i, 0)),
        out_shape=jax.ShapeDtypeStruct((t, width), BF16),
        compiler_params=_params("parallel"),
        name="head_norm_rope",
    )(proj, g.reshape(1, LANES), cos, sin)


def _rope_only_kernel(x_ref, cos_ref, sin_ref, o_ref):
    cos, sin = cos_ref[0], sin_ref[0]
    for c in range(x_ref.shape[1] // LANES):
        x = x_ref[:, c * LANES:(c + 1) * LANES]
        o_ref[:, c * LANES:(c + 1) * LANES] = _rope_tile(x, cos, sin, IDX_ROT // 2, IDX_DIM).astype(o_ref.dtype)


def _rope_only(proj, width, col_block, cos, sin):
    t = proj.shape[0]
    bm = _block(t, 512)
    table = pl.BlockSpec((1, bm, LANES), lambda i: (2, i, 0))
    return pl.pallas_call(
        _rope_only_kernel,
        grid=(t // bm,),
        in_specs=[pl.BlockSpec((bm, width), lambda i: (i, col_block)), table, table],
        out_specs=pl.BlockSpec((bm, width), lambda i: (i, 0)),
        out_shape=jax.ShapeDtypeStruct((t, width), BF16),
        compiler_params=_params("parallel"),
        name="indexer_rope",
    )(proj, cos, sin)


SCORE_COLS = 256
COUNT_ACC_ROWS = 64


def _sortable_key(x):
    bits = lax.bitcast_convert_type(x, jnp.int32)
    return bits ^ ((bits >> 31) & 0x7FFFFFFF)


def _indexer_kernel(qi_ref, kit_ref, w_ref, sel_ref, key_ref, wb_ref, qh_ref, *, tq, tk, topk, seq, w_scale):
    i = pl.program_id(1)
    n_tiles = pl.cdiv((i + 1) * tq, tk)
    row_chunk = _chunk_of(i * tq + lax.broadcasted_iota(jnp.int32, (tq, 1), 0))

    w = w_ref[...] * w_scale
    for h in range(IDX_HEADS):
        wb_ref[h] = jnp.broadcast_to(w[:, h:h + 1], (tq, LANES))
        qh_ref[h] = qi_ref[:, h * IDX_DIM:(h + 1) * IDX_DIM]

    def score_tile(j, carry):
        for part in range(tk // SCORE_COLS):
            off = pl.multiple_of(j * tk + part * SCORE_COLS, SCORE_COLS)
            kt = kit_ref[:, pl.ds(off, SCORE_COLS)]
            acc = [jnp.zeros((tq, LANES), F32) for _ in range(SCORE_COLS // LANES)]
            for h in range(IDX_HEADS):
                logits = _dot(qh_ref[h], kt)
                wh = wb_ref[h]
                for c in range(SCORE_COLS // LANES):
                    acc[c] = acc[c] + jnp.maximum(logits[:, c * LANES:(c + 1) * LANES], 0.0) * wh
            for c in range(SCORE_COLS // LANES):
                col = off + c * LANES + lax.broadcasted_iota(jnp.int32, (1, LANES), 1)
                score = jnp.where(_chunk_of(col) <= row_chunk, acc[c], NEG_INF)
                key_ref[pl.ds(off + c * LANES, LANES), :] = _sortable_key(score.T)
        return carry

    lax.fori_loop(0, n_tiles, score_tile, 0)

    def count(pred):
        rows = min(tk, COUNT_ACC_ROWS)

        def tile(j, acc):
            off = pl.multiple_of(j * tk, tk)
            hit = jnp.where(pred(key_ref[pl.ds(off, tk), :], off), 1.0, 0.0)
            return acc + jnp.sum(hit.reshape(tk // rows, rows, tq), axis=0)
        acc = lax.fori_loop(0, n_tiles, tile, jnp.zeros((rows, tq), F32))
        return jnp.sum(acc, axis=0, keepdims=True)

    def value_bit(b, lo):
        cand = lo + lax.shift_left(jnp.int32(1), 31 - b)
        n_ge = count(lambda kt, _: kt >= cand)
        return jnp.where(n_ge >= topk, cand, lo)

    thr = lax.fori_loop(0, 32, value_bit, jnp.full((1, tq), INT_MIN, jnp.int32))
    n_ge = count(lambda kt, _: kt >= thr)
    surplus = jnp.logical_and(n_ge > topk, thr > NEG_INF_KEY)

    def write_tile(j, carry):
        off = pl.multiple_of(j * tk, tk)
        kt = key_ref[pl.ds(off, tk), :]
        chosen = jnp.logical_and(kt >= thr, kt > NEG_INF_KEY)
        sel_ref[pl.ds(off, tk), :] = jnp.where(chosen, 1.0, 0.0).astype(sel_ref.dtype)
        return carry

    lax.fori_loop(0, n_tiles, write_tile, 0)

    def clear_tile(j, carry):
        off = pl.multiple_of(j * tk, tk)
        sel_ref[pl.ds(off, tk), :] = jnp.zeros((tk, tq), sel_ref.dtype)
        return carry

    lax.fori_loop(n_tiles, seq // tk, clear_tile, 0)

    @pl.when(jnp.max(jnp.where(surplus, 1.0, 0.0)) > 0.0)
    def _():
        n_gt = count(lambda kt, _: kt > thr)
        need = topk - n_gt

        def position(first_key):
            return first_key + lax.broadcasted_iota(jnp.int32, (tk, tq), 0)

        def position_bit(b, base):
            cand = base + lax.shift_left(jnp.int32(1), (seq - 1).bit_length() - 1 - b)
            n_before = count(lambda kt, k0: jnp.logical_and(kt == thr, position(k0) < cand))
            return jnp.where(n_before < need, cand, base)

        last = lax.fori_loop(0, (seq - 1).bit_length(), position_bit, jnp.zeros((1, tq), jnp.int32))

        def rewrite_tile(j, carry):
            off = pl.multiple_of(j * tk, tk)
            kt = key_ref[pl.ds(off, tk), :]
            chosen = jnp.logical_or(kt > thr, jnp.logical_and(kt == thr, position(off) <= last))
            chosen = jnp.logical_and(chosen, kt > NEG_INF_KEY)
            sel_ref[pl.ds(off, tk), :] = jnp.where(chosen, 1.0, 0.0).astype(sel_ref.dtype)
            return carry

        lax.fori_loop(0, n_tiles, rewrite_tile, 0)


def _indexer_select(qi, kit, w, w_col_block, batch, topk, tq=128, tk=512):
    s = kit.shape[2]
    tq = _block(s, tq)
    tk = _block(s, tk)
    assert tk % SCORE_COLS == 0
    assert tq == LANES, "score tiles are transposed one (LANES, LANES) tile at a time"
    nq = s // tq
    w_scale = (IDX_DIM ** -0.5) * (IDX_HEADS ** -0.5)
    return pl.pallas_call(
        functools.partial(_indexer_kernel, tq=tq, tk=tk, topk=topk, seq=s, w_scale=w_scale),
        grid=(batch, nq),
        in_specs=[pl.BlockSpec((tq, IDX_HEADS * IDX_DIM), lambda b, i: (b * nq + i, 0)),
                  pl.BlockSpec((None, IDX_DIM, s), lambda b, i: (b, 0, 0)),
                  pl.BlockSpec((tq, LANES), lambda b, i: (b * nq + i, w_col_block))],
        out_specs=pl.BlockSpec((None, s, tq), lambda b, i: (b, 0, i)),
        out_shape=jax.ShapeDtypeStruct((batch, s, s), BF16),
        scratch_shapes=[pltpu.VMEM((s, tq), jnp.int32), pltpu.VMEM((IDX_HEADS, tq, LANES), F32),
                        pltpu.VMEM((IDX_HEADS, tq, IDX_DIM), BF16)],
        compiler_params=_params("parallel", "parallel"),
        name="indexer_select",
    )(qi, kit, w)


def _memory_attn_kernel(q_ref, kv_ref, gq_ref, gk_ref, o_ref, *, scale):
    hd = MEM_HEAD_DIM
    gq, gk = gq_ref[...], gk_ref[...]
    for h in range(MEM_HEADS):
        q = q_ref[:, h * hd:(h + 1) * hd]
        k = kv_ref[:, h * hd:(h + 1) * hd]
        v = kv_ref[:, (MEM_HEADS + h) * hd:(MEM_HEADS + h + 1) * hd].astype(BF16)
        qn = (q * _rms_scale(q, hd) * gq).astype(BF16)
        kn = (k * _rms_scale(k, hd) * gk).astype(BF16)
        s = _dot_nt(qn, kn) * scale
        e = jnp.exp(s - jnp.max(s, axis=-1, keepdims=True))
        p = e / jnp.sum(e, axis=-1, keepdims=True)
        o_ref[:, h * hd:(h + 1) * hd] = _dot(p.astype(BF16), v).astype(o_ref.dtype)


def _memory_attention(proj, col_block, kv, g_q, g_k, batch, tq=512):
    t = proj.shape[0]
    s = t // batch
    m = kv.shape[0] // batch
    width = MEM_HEADS * MEM_HEAD_DIM
    tq = _block(s, tq)
    nq = s // tq
    g_spec = pl.BlockSpec((1, MEM_HEAD_DIM), lambda b, i: (0, 0))
    return pl.pallas_call(
        functools.partial(_memory_attn_kernel, scale=MEM_HEAD_DIM ** -0.5),
        grid=(batch, nq),
        in_specs=[pl.BlockSpec((tq, width), lambda b, i: (b * nq + i, col_block)),
                  pl.BlockSpec((m, 2 * width), lambda b, i: (b, 0)), g_spec, g_spec],
        out_specs=pl.BlockSpec((tq, width), lambda b, i: (b * nq + i, 0)),
        out_shape=jax.ShapeDtypeStruct((t, width), BF16),
        compiler_params=_params("parallel", "parallel"),
        name="memory_attention",
    )(proj, kv, g_q.reshape(1, -1), g_k.reshape(1, -1))


def _projection_layout(q_rank, kv_rank):
    dsa = DSA_HEADS * DSA_HEAD_DIM
    sizes = [("c_q", q_rank), ("c_kv", kv_rank), ("k_r", MLA_ROPE), ("q_d", dsa), ("k_d", dsa), ("v_d", dsa),
             ("q_i", IDX_HEADS * IDX_DIM), ("k_i", IDX_DIM), ("w_i", IDX_HEADS), ("q_m", MEM_HEADS * MEM_HEAD_DIM)]
    src, start = {}, 0
    for name, n in sizes:
        src[name] = (start, n)
        start += n
    return src, start


def _hybrid_mixer(x, h, mem, positions, w_in, g_cq, w_uq, g_ckv, w_ukv, g_q_mla, g_k_mla, g_q_dsa, g_k_dsa,
                  g_mem, w_mem_kv, g_q_mem, g_k_mem, w_o_mla, w_o_dsa, w_o_mem, w_gate, w_out, batch):
    t, d = h.shape
    s = t // batch
    q_rank, kv_rank = w_uq.shape[0], w_ukv.shape[0]
    src, d_in = _projection_layout(q_rank, kv_rank)
    assert d_in == w_in.shape[1]
    dsa = DSA_HEADS * DSA_HEAD_DIM
    assert q_rank == dsa, "the reordered projection layout assumes equal widths for these groups"

    take = lambda name: w_in[:, src[name][0]:src[name][0] + src[name][1]]
    pieces = [("c_q", take("c_q")), ("q_d", take("q_d")), ("k_d", take("k_d")), ("v_d", take("v_d")),
              ("q_i", take("q_i")), ("q_m", take("q_m")), ("c_kv", take("c_kv")),
              ("k_r", jnp.concatenate([take("k_r"), take("k_i")], axis=1)),
              ("w_i", _pad_last(take("w_i"), LANES))]
    col, start = {}, 0
    for name, piece in pieces:
        assert start % piece.shape[1] == 0, name
        col[name] = start
        start += piece.shape[1]
    w_in_r = jnp.concatenate([p for _, p in pieces], axis=1).astype(BF16)

    proj = _matmul(h, w_in_r, F32, bm=1024, bn=768, name="input_projection")
    cos, sin = _rope_tables(positions)

    q_a, k_a, vt_a = _mla_prep(proj, col, g_cq, w_uq, g_ckv, w_ukv, g_q_mla, g_k_mla, cos, sin, batch)
    y_mla = _attention(q_a, k_a, vt_a, None, batch, MLA_HEADS,
                       MLA_QK_PAD, MLA_V, "chunk_causal_attention")

    q_b = _head_norm_rope(proj, col["q_d"] // dsa, g_q_dsa, cos, sin, _query_premultiplier(DSA_HEAD_DIM))
    k_b = _head_norm_rope(proj, col["k_d"] // dsa, g_k_dsa, cos, sin)
    vt_b = _heads_transposed(proj, col["v_d"] // dsa, batch, DSA_HEADS, DSA_HEAD_DIM)
    qi_width = IDX_HEADS * IDX_DIM
    qi = _rope_only(proj, qi_width, col["q_i"] // qi_width, cos, sin)
    ki = _rope_only(proj, LANES, col["k_r"] // LANES, cos, sin)[:, IDX_DIM:]
    kit = ki.reshape(batch, s, IDX_DIM).transpose(0, 2, 1)
    topk = min(DSA_MAX_TOPK, s // 4)
    selected_t = _indexer_select(qi, kit, proj, col["w_i"] // LANES, batch, topk)
    y_dsa = _attention(q_b, k_b, vt_b, selected_t, batch, DSA_HEADS,
                       DSA_HEAD_DIM, DSA_HEAD_DIM, "masked_attention")

    m_len = mem.shape[1]
    mem_n = _rmsnorm(mem.reshape(batch * m_len, d), g_mem)
    kv_mem = _matmul(mem_n, w_mem_kv.astype(BF16), F32, bm=512, bn=512, name="memory_kv")
    y_mem = _memory_attention(proj, col["q_m"] // (MEM_HEADS * MEM_HEAD_DIM), kv_mem, g_q_mem, g_k_mem, batch)

    gated = _gate_combine(h, (y_mla, y_dsa, y_mem), w_gate.astype(BF16),
                          (w_o_mla.astype(BF16), w_o_dsa.astype(BF16), w_o_mem.astype(BF16)))
    return _matmul_residual(gated, w_out.astype(BF16), x, 1.0, name="mixer_out")


def kernel(x, mem, positions, g_ff1, w_ff1_gate, w_ff1_up, w_ff1_down, g_mix, w_in, g_cq, w_uq, g_ckv, w_ukv, g_q_mla, g_k_mla, g_q_dsa, g_k_dsa, g_mem, w_mem_kv, g_q_mem, g_k_mem, w_o_mla, w_o_dsa, w_o_mem, w_gate, w_out, g_ff2, w_ff2_gate, w_ff2_up, w_ff2_down):
    batch, seq, d = x.shape
    xt = x.reshape(batch * seq, d)
    for l in range(g_ff1.shape[0]):
        xt = _ffn_half_step(xt, g_ff1[l], w_ff1_gate[l], w_ff1_up[l], w_ff1_down[l])
        h = _rmsnorm(xt, g_mix[l])
        xt = _hybrid_mixer(xt, h, mem, positions, w_in[l], g_cq[l], w_uq[l], g_ckv[l], w_ukv[l],
                           g_q_mla[l], g_k_mla[l], g_q_dsa[l], g_k_dsa[l], g_mem[l], w_mem_kv[l],
                           g_q_mem[l], g_k_mem[l], w_o_mla[l], w_o_dsa[l], w_o_mem[l], w_gate[l], w_out[l], batch)
        xt = _ffn_half_step(xt, g_ff2[l], w_ff2_gate[l], w_ff2_up[l], w_ff2_down[l])
    return xt.reshape(batch, seq, d)
```

```python
import functools

import jax
import jax.numpy as jnp
from jax import lax
from jax.experimental import pallas as pl
from jax.experimental.pallas import tpu as pltpu

CHUNK = 64
ROPE_THETA = 500000.0
EPS = 1e-6

MLA_HEADS = 12
MLA_NOPE = 128
MLA_ROPE = 64
MLA_V = 128
MLA_QK = MLA_ROPE + MLA_NOPE
MLA_QK_PAD = 256

DSA_HEADS = 12
DSA_HEAD_DIM = 128
DSA_ROT = DSA_HEAD_DIM // 4
DSA_MAX_TOPK = 256
IDX_HEADS = 32
IDX_DIM = 64
IDX_ROT = IDX_DIM // 4

MEM_HEADS = 4
MEM_HEAD_DIM = 256

LANES = 128
SUBLANES = 8
VMEM_LIMIT_BYTES = 56 * 1024 * 1024

F32 = jnp.float32
BF16 = jnp.bfloat16
NEG_INF = float("-inf")
INT_MIN = -2 ** 31
LOG2_E = 1.4426950408889634
NEG_INF_KEY = (0xFF800000 - (1 << 32)) ^ 0x7FFFFFFF


def _params(*semantics):
    return pltpu.CompilerParams(dimension_semantics=semantics, vmem_limit_bytes=VMEM_LIMIT_BYTES)


def _block(dim, preferred):
    if dim <= preferred:
        return dim
    b = preferred - preferred % LANES
    while b >= LANES:
        if dim % b == 0:
            return b
        b -= LANES
    return dim


def _rms_scale(x, width):
    return lax.rsqrt(jnp.sum(x * x, axis=-1, keepdims=True) * (1.0 / width) + EPS)


def _dot(a, b):
    return jnp.dot(a, b, preferred_element_type=F32)


def _dot_nt(a, b):
    return lax.dot_general(a, b, (((1,), (1,)), ((), ())), preferred_element_type=F32)


def _sigmoid(x):
    return 1.0 / (1.0 + jnp.exp(-x))


def _query_premultiplier(head_dim):
    return head_dim ** -0.5 * LOG2_E


def _chunk_of(frame):
    return lax.shift_right_logical(frame, CHUNK.bit_length() - 1)


def _rmsnorm_kernel(x_ref, g_ref, o_ref):
    x = x_ref[...]
    o_ref[...] = (x * _rms_scale(x, x.shape[-1]) * g_ref[...]).astype(o_ref.dtype)


def _rmsnorm(x, g, out_dtype=BF16):
    m, d = x.shape
    bm = _block(m, 512)
    return pl.pallas_call(
        _rmsnorm_kernel,
        grid=(m // bm,),
        in_specs=[pl.BlockSpec((bm, d), lambda i: (i, 0)), pl.BlockSpec((1, d), lambda i: (0, 0))],
        out_specs=pl.BlockSpec((bm, d), lambda i: (i, 0)),
        out_shape=jax.ShapeDtypeStruct((m, d), out_dtype),
        compiler_params=_params("parallel"),
        name="rmsnorm",
    )(x, g.reshape(1, d))


def _matmul_kernel(a_ref, b_ref, o_ref):
    o_ref[...] = _dot(a_ref[...], b_ref[...]).astype(o_ref.dtype)


def _matmul(a, b, out_dtype, bm=1024, bn=512, name="matmul"):
    m, k = a.shape
    n = b.shape[1]
    bm, bn = _block(m, bm), _block(n, bn)
    return pl.pallas_call(
        _matmul_kernel,
        grid=(m // bm, n // bn),
        in_specs=[pl.BlockSpec((bm, k), lambda i, j: (i, 0)), pl.BlockSpec((k, bn), lambda i, j: (0, j))],
        out_specs=pl.BlockSpec((bm, bn), lambda i, j: (i, j)),
        out_shape=jax.ShapeDtypeStruct((m, n), out_dtype),
        compiler_params=_params("parallel", "parallel"),
        name=name,
    )(a, b)


def _swiglu_up_kernel(a_ref, wg_ref, wu_ref, o_ref):
    a = a_ref[...]
    g = _dot(a, wg_ref[...])
    u = _dot(a, wu_ref[...])
    o_ref[...] = (g * _sigmoid(g) * u).astype(o_ref.dtype)


def _swiglu_up(a, wg, wu, bm=1024, bn=512):
    m, k = a.shape
    n = wg.shape[1]
    bm, bn = _block(m, bm), _block(n, bn)
    return pl.pallas_call(
        _swiglu_up_kernel,
        grid=(m // bm, n // bn),
        in_specs=[pl.BlockSpec((bm, k), lambda i, j: (i, 0)),
                  pl.BlockSpec((k, bn), lambda i, j: (0, j)),
                  pl.BlockSpec((k, bn), lambda i, j: (0, j))],
        out_specs=pl.BlockSpec((bm, bn), lambda i, j: (i, j)),
        out_shape=jax.ShapeDtypeStruct((m, n), BF16),
        compiler_params=_params("parallel", "parallel"),
        name="swiglu_up",
    )(a, wg, wu)


def _matmul_residual_kernel(a_ref, b_ref, x_ref, o_ref, *, scale):
    o_ref[...] = x_ref[...] + scale * _dot(a_ref[...], b_ref[...])


def _matmul_residual(a, b, x, scale, bm=512, bn=512, name="matmul_residual"):
    m, k = a.shape
    n = b.shape[1]
    bm, bn = _block(m, bm), _block(n, bn)
    return pl.pallas_call(
        functools.partial(_matmul_residual_kernel, scale=scale),
        grid=(m // bm, n // bn),
        in_specs=[pl.BlockSpec((bm, k), lambda i, j: (i, 0)),
                  pl.BlockSpec((k, bn), lambda i, j: (0, j)),
                  pl.BlockSpec((bm, bn), lambda i, j: (i, j))],
        out_specs=pl.BlockSpec((bm, bn), lambda i, j: (i, j)),
        out_shape=jax.ShapeDtypeStruct((m, n), F32),
        compiler_params=_params("parallel", "parallel"),
        name=name,
    )(a, b, x)


def _ffn_half_step(x, g, w_gate, w_up, w_down):
    h = _rmsnorm(x, g)
    a = _swiglu_up(h, w_gate.astype(BF16), w_up.astype(BF16))
    return _matmul_residual(a, w_down.astype(BF16), x, 0.5, name="ffn_down")


def _gate_combine_kernel(h_ref, ya_ref, yb_ref, yc_ref, wg_ref, wa_ref, wb_ref, wc_ref, o_ref):
    h = h_ref[...]
    out = _sigmoid(_dot(h, wg_ref[0])) * _dot(ya_ref[...], wa_ref[...])
    out = out + _sigmoid(_dot(h, wg_ref[1])) * _dot(yb_ref[...], wb_ref[...])
    out = out + _sigmoid(_dot(h, wg_ref[2])) * _dot(yc_ref[...], wc_ref[...])
    o_ref[...] = out.astype(o_ref.dtype)


def _gate_combine(h, ys, w_gate, w_os, bm=512, bn=512):
    m, d = h.shape
    n = w_gate.shape[-1]
    bm, bn = _block(m, bm), _block(n, bn)
    row = lambda w: pl.BlockSpec((bm, w), lambda i, j: (i, 0))
    col = lambda k: pl.BlockSpec((k, bn), lambda i, j: (0, j))
    return pl.pallas_call(
        _gate_combine_kernel,
        grid=(m // bm, n // bn),
        in_specs=[row(d)] + [row(y.shape[1]) for y in ys]
                 + [pl.BlockSpec((3, d, bn), lambda i, j: (0, 0, j))] + [col(w.shape[0]) for w in w_os],
        out_specs=pl.BlockSpec((bm, bn), lambda i, j: (i, j)),
        out_shape=jax.ShapeDtypeStruct((m, n), BF16),
        compiler_params=_params("parallel", "parallel"),
        name="gate_combine",
    )(h, *ys, w_gate, *w_os)


def _rope_tables_kernel(pos_ref, invf_ref, sgn_ref, cos_ref, sin_ref):
    pos = pos_ref[...].astype(F32)
    for t in range(invf_ref.shape[0]):
        ang = pos * invf_ref[t:t + 1, :]
        cos_ref[t] = jnp.cos(ang)
        sin_ref[t] = jnp.sin(ang) * sgn_ref[t:t + 1, :]


def _rope_lane_tables(rot, period):
    half = rot // 2
    inv_freq = ROPE_THETA ** (-jnp.arange(half, dtype=F32) / half)
    lane = jnp.arange(LANES) % period
    invf = jnp.where(lane < rot, inv_freq[lane % half], 0.0)
    sgn = jnp.where(lane < half, -1.0, jnp.where(lane < rot, 1.0, 0.0))
    return invf.astype(F32), sgn.astype(F32)


def _rope_tables(positions):
    t = positions.size
    specs = [_rope_lane_tables(MLA_ROPE, LANES), _rope_lane_tables(DSA_ROT, LANES),
             _rope_lane_tables(IDX_ROT, IDX_DIM)]
    invf = jnp.stack([s[0] for s in specs])
    sgn = jnp.stack([s[1] for s in specs])
    bm = _block(t, 1024)
    n = invf.shape[0]
    const = pl.BlockSpec((n, LANES), lambda i: (0, 0))
    out = pl.BlockSpec((n, bm, LANES), lambda i: (0, i, 0))
    return pl.pallas_call(
        _rope_tables_kernel,
        grid=(t // bm,),
        in_specs=[pl.BlockSpec((bm, 1), lambda i: (i, 0)), const, const],
        out_specs=[out, out],
        out_shape=[jax.ShapeDtypeStruct((n, t, LANES), F32)] * 2,
        compiler_params=_params("parallel"),
        name="rope_tables",
    )(positions.reshape(t, 1), invf, sgn)


def _rope_tile(x, cos, sin, half, period):
    lane = lax.broadcasted_iota(jnp.int32, x.shape, 1) & (period - 1)
    partner = jnp.where(lane < half, pltpu.roll(x, LANES - half, 1), pltpu.roll(x, half, 1))
    return x * cos + partner * sin


def _mla_q_kernel(cq_ref, gcq_ref, w_ref, gq_ref, cos_ref, sin_ref, o_ref):
    x = cq_ref[...]
    xn = (x * _rms_scale(x, x.shape[-1]) * gcq_ref[...]).astype(BF16)
    q = _dot(xn, w_ref[...])
    cos, sin = cos_ref[0], sin_ref[0]
    g = gq_ref[...] * _query_premultiplier(MLA_QK)
    for h in range(MLA_HEADS):
        lo = h * MLA_QK_PAD
        qh = q[:, lo:lo + MLA_QK_PAD]
        y = qh * _rms_scale(qh, MLA_QK) * g
        o_ref[:, lo:lo + LANES] = _rope_tile(y[:, :LANES], cos, sin, MLA_ROPE // 2, LANES).astype(o_ref.dtype)
        o_ref[:, lo + LANES:lo + MLA_QK_PAD] = y[:, LANES:].astype(o_ref.dtype)


def _store_heads_transposed(x, vt_ref):
    heads, hd, _ = vt_ref.shape
    for h in range(heads):
        vt_ref[h] = x[:, h * hd:(h + 1) * hd].T.astype(vt_ref.dtype)


def _mla_kv_kernel(ckv_ref, kr_ref, gckv_ref, wk_ref, wv_ref, gk_ref, cos_ref, sin_ref, k_ref, vt_ref):
    x = ckv_ref[...]
    xn = (x * _rms_scale(x, x.shape[-1]) * gckv_ref[...]).astype(BF16)
    k_nope = _dot(xn, wk_ref[...])
    _store_heads_transposed(_dot(xn, wv_ref[...]), vt_ref)
    kr = kr_ref[...]
    lane = lax.broadcasted_iota(jnp.int32, kr.shape, 1)
    kr = jnp.where(lane < MLA_ROPE, kr, 0.0)
    cos, sin = cos_ref[0], sin_ref[0]
    g = gk_ref[...]
    for h in range(MLA_HEADS):
        lo = h * MLA_QK_PAD
        ka = k_nope[:, lo:lo + LANES] + kr
        kb = k_nope[:, lo + LANES:lo + MLA_QK_PAD]
        ms = (jnp.sum(ka * ka, axis=-1, keepdims=True) + jnp.sum(kb * kb, axis=-1, keepdims=True)) * (1.0 / MLA_QK)
        r = lax.rsqrt(ms + EPS)
        ya = ka * r * g[:, :LANES]
        yb = kb * r * g[:, LANES:]
        k_ref[:, lo:lo + LANES] = _rope_tile(ya, cos, sin, MLA_ROPE // 2, LANES).astype(k_ref.dtype)
        k_ref[:, lo + LANES:lo + MLA_QK_PAD] = yb.astype(k_ref.dtype)


def _pad_last(a, width):
    return jnp.pad(a, [(0, 0)] * (a.ndim - 1) + [(0, width - a.shape[-1])])


def _mla_prep(proj, col, g_cq, w_uq, g_ckv, w_ukv, g_q, g_k, cos, sin, batch):
    t = proj.shape[0]
    q_rank, kv_rank = w_uq.shape[0], w_ukv.shape[0]
    hp = MLA_HEADS * MLA_QK_PAD
    w_q = _pad_last(w_uq.reshape(q_rank, MLA_HEADS, MLA_QK), MLA_QK_PAD).reshape(q_rank, hp).astype(BF16)
    w_kv = w_ukv.reshape(kv_rank, MLA_HEADS, MLA_NOPE + MLA_V)
    w_k = jnp.pad(w_kv[..., :MLA_NOPE], ((0, 0), (0, 0), (MLA_ROPE, MLA_QK_PAD - MLA_QK)))
    w_k = w_k.reshape(kv_rank, hp).astype(BF16)
    w_v = w_kv[..., MLA_NOPE:].reshape(kv_rank, MLA_HEADS * MLA_V).astype(BF16)
    g_qp = _pad_last(g_q, MLA_QK_PAD).reshape(1, MLA_QK_PAD)
    g_kp = _pad_last(g_k, MLA_QK_PAD).reshape(1, MLA_QK_PAD)

    bm = _block(t, 512)
    rows = lambda w, c: pl.BlockSpec((bm, w), lambda i, c=c: (i, c))
    full = lambda a: pl.BlockSpec(a.shape, lambda i: (0,) * a.ndim)
    table = pl.BlockSpec((1, bm, LANES), lambda i: (0, i, 0))
    g_cq2, g_ckv2 = g_cq.reshape(1, -1), g_ckv.reshape(1, -1)

    q = pl.pallas_call(
        _mla_q_kernel,
        grid=(t // bm,),
        in_specs=[rows(q_rank, col["c_q"] // q_rank), full(g_cq2), full(w_q), full(g_qp), table, table],
        out_specs=pl.BlockSpec((bm, hp), lambda i: (i, 0)),
        out_shape=jax.ShapeDtypeStruct((t, hp), BF16),
        compiler_params=_params("parallel"),
        name="mla_q_prep",
    )(proj, g_cq2, w_q, g_qp, cos, sin)

    k, v = pl.pallas_call(
        _mla_kv_kernel,
        grid=(t // bm,),
        in_specs=[rows(kv_rank, col["c_kv"] // kv_rank), rows(LANES, col["k_r"] // LANES), full(g_ckv2),
                  full(w_k), full(w_v), full(g_kp), table, table],
        out_specs=[pl.BlockSpec((bm, hp), lambda i: (i, 0)),
                   _heads_transposed_spec(MLA_HEADS, MLA_V, bm, t // batch)],
        out_shape=[jax.ShapeDtypeStruct((t, hp), BF16),
                   jax.ShapeDtypeStruct((batch, MLA_HEADS, MLA_V, t // batch), BF16)],
        compiler_params=_params("parallel"),
        name="mla_kv_prep",
    )(proj, proj, g_ckv2, w_k, w_v, g_kp, cos, sin)
    return q, k, v


ATTN_HEADS_PER_STEP = 3


def _attn_scratch(tq, dk, dv):
    per_head = [pltpu.VMEM((1, tq), F32), pltpu.VMEM((1, tq), F32), pltpu.VMEM((dv, tq), F32)]
    per_parity = [pltpu.VMEM((tq, tq), F32), pltpu.VMEM((1, tq), F32),
                  pltpu.VMEM((tq, tq), BF16), pltpu.VMEM((1, tq), F32)]
    return (per_head + 2 * per_parity) * ATTN_HEADS_PER_STEP + [pltpu.VMEM((ATTN_HEADS_PER_STEP, dk, tq), BF16)]


def _attn_kernel(q_ref, k_ref, vt_ref, *rest, tq, dk, dv, selected_keys):
    n_scratch = 11 * ATTN_HEADS_PER_STEP + 1
    scratch = rest[-n_scratch:]
    qt_ref = scratch[-1]
    sel_ref = rest[0] if selected_keys else None
    o_ref = rest[-n_scratch - 1]
    i = pl.program_id(2)
    heads = range(ATTN_HEADS_PER_STEP)
    m_of, l_of, acc_of, s_of, mx_of, p_of, al_of = [], [], [], [], [], [], []
    for h in heads:
        refs = scratch[11 * h:11 * (h + 1)]
        m_of.append(refs[0]); l_of.append(refs[1]); acc_of.append(refs[2])
        s_of.append((refs[3], refs[7])); mx_of.append((refs[4], refs[8]))
        p_of.append((refs[5], refs[9])); al_of.append((refs[6], refs[10]))
    for h in heads:
        qt_ref[h] = q_ref[:, h * dk:(h + 1) * dk].T

    for h in heads:
        m_of[h][...] = jnp.full((1, tq), NEG_INF, F32)
        l_of[h][...] = jnp.zeros((1, tq), F32)
        acc_of[h][...] = jnp.zeros((dv, tq), F32)
        s_of[h][1][...] = jnp.full((tq, tq), NEG_INF, F32)
        mx_of[h][1][...] = jnp.full((1, tq), NEG_INF, F32)
        p_of[h][0][...] = jnp.zeros((tq, tq), BF16)
        al_of[h][0][...] = jnp.zeros((1, tq), F32)

    def scores(block, par, diagonal):
        off = pl.multiple_of(block * tq, tq)
        if selected_keys:
            visible = sel_ref[pl.ds(off, tq), :].astype(F32) > 0.0
        elif diagonal:
            key_chunk = _chunk_of(lax.broadcasted_iota(jnp.int32, (tq, tq), 0))
            query_chunk = _chunk_of(lax.broadcasted_iota(jnp.int32, (tq, tq), 1))
            visible = key_chunk <= query_chunk
        else:
            visible = None
        for h in heads:
            s = _dot(k_ref[pl.ds(off, tq), h * dk:(h + 1) * dk], qt_ref[h])
            if visible is not None:
                s = jnp.where(visible, s, NEG_INF)
            s_of[h][par][...] = s
            mx_of[h][par][...] = jnp.max(s, axis=0, keepdims=True)

    def softmax(par):
        for h in heads:
            m_prev = m_of[h][...]
            m_new = jnp.maximum(m_prev, mx_of[h][par][...])
            m_safe = jnp.where(m_new == NEG_INF, 0.0, m_new)
            alpha = jnp.exp2(m_prev - m_safe)
            p = jnp.exp2(s_of[h][par][...] - m_safe)
            l_of[h][...] = alpha * l_of[h][...] + jnp.sum(p, axis=0, keepdims=True)
            p_of[h][par][...] = p.astype(BF16)
            al_of[h][par][...] = alpha
            m_of[h][...] = m_new

    def values(block, par):
        off = pl.multiple_of(jnp.maximum(block, 0) * tq, tq)
        for h in heads:
            acc_of[h][...] = al_of[h][par][...] * acc_of[h][...] + _dot(vt_ref[h, :, pl.ds(off, tq)],
                                                                         p_of[h][par][...])

    def step(t, par, diagonal=False):
        values(t - 2, par)
        softmax(1 - par)
        scores(t, par, diagonal)

    def pair(u, carry):
        step(2 * u, 0)
        step(2 * u + 1, 1)
        return carry

    lax.fori_loop(0, lax.shift_right_logical(i, 1), pair, 0)

    def drain(par):
        step(i, par, diagonal=True)
        values(i - 1, 1 - par)
        softmax(par)
        values(i, par)

    odd = (i & 1) == 1

    @pl.when(odd)
    def _():
        step(i - 1, 0)
        drain(1)

    @pl.when(jnp.logical_not(odd))
    def _():
        drain(0)

    for h in heads:
        o = acc_of[h][...] / l_of[h][...]
        o_ref[:, h * dv:(h + 1) * dv] = o.T.astype(o_ref.dtype)


def _attention(q, k, vt, selected_t, batch, heads, dk, dv, name, tq=256):
    t = q.shape[0]
    s = t // batch
    tq = _block(s, tq)
    nq = s // tq
    hp = ATTN_HEADS_PER_STEP
    in_specs = [pl.BlockSpec((tq, hp * dk), lambda b, h, i: (b * nq + i, h)),
                pl.BlockSpec((s, hp * dk), lambda b, h, i: (b, h)),
                pl.BlockSpec((None, hp, dv, s), lambda b, h, i: (b, h, 0, 0))]
    operands = [q, k, vt]
    if selected_t is not None:
        in_specs.append(pl.BlockSpec((None, s, tq), lambda b, h, i: (b, 0, i)))
        operands.append(selected_t)
    return pl.pallas_call(
        functools.partial(_attn_kernel, tq=tq, dk=dk, dv=dv, selected_keys=selected_t is not None),
        grid=(batch, heads // hp, nq),
        in_specs=in_specs,
        out_specs=pl.BlockSpec((tq, hp * dv), lambda b, h, i: (b * nq + i, h)),
        out_shape=jax.ShapeDtypeStruct((t, heads * dv), BF16),
        scratch_shapes=_attn_scratch(tq, dk, dv),
        compiler_params=_params("parallel", "parallel", "parallel"),
        name=name,
    )(*operands)


def _heads_transposed_spec(heads, hd, bm, seq):
    blocks_per_batch = seq // bm
    return pl.BlockSpec((None, heads, hd, bm), lambda i: (i // blocks_per_batch, 0, 0, i % blocks_per_batch))


def _heads_transposed_kernel(x_ref, vt_ref):
    _store_heads_transposed(x_ref[...], vt_ref)


def _heads_transposed(proj, col_block, batch, heads, hd):
    t = proj.shape[0]
    bm = _block(t // batch, 512)
    return pl.pallas_call(
        _heads_transposed_kernel,
        grid=(t // bm,),
        in_specs=[pl.BlockSpec((bm, heads * hd), lambda i: (i, col_block))],
        out_specs=_heads_transposed_spec(heads, hd, bm, t // batch),
        out_shape=jax.ShapeDtypeStruct((batch, heads, hd, t // batch), BF16),
        compiler_params=_params("parallel"),
        name="heads_transposed",
    )(proj)


def _head_norm_rope_kernel(x_ref, g_ref, cos_ref, sin_ref, o_ref, *, heads, premultiplier):
    cos, sin = cos_ref[0], sin_ref[0]
    g = g_ref[...] * premultiplier
    for h in range(heads):
        x = x_ref[:, h * LANES:(h + 1) * LANES]
        y = x * _rms_scale(x, LANES) * g
        o_ref[:, h * LANES:(h + 1) * LANES] = _rope_tile(y, cos, sin, DSA_ROT // 2, LANES).astype(o_ref.dtype)


def _head_norm_rope(proj, col_block, g, cos, sin, premultiplier=1.0):
    t = proj.shape[0]
    width = DSA_HEADS * DSA_HEAD_DIM
    bm = _block(t, 512)
    table = pl.BlockSpec((1, bm, LANES), lambda i: (1, i, 0))
    return pl.pallas_call(
        functools.partial(_head_norm_rope_kernel, heads=DSA_HEADS, premultiplier=premultiplier),
        grid=(t // bm,),
        in_specs=[pl.BlockSpec((bm, width), lambda i: (i, col_block)),
                  pl.BlockSpec((1, LANES), lambda i: (0, 0)), table, table],
        out_specs=pl.BlockSpec((bm, width), lambda i: (i, 0)),
        out_shape=jax.ShapeDtypeStruct((t, width), BF16),
        compiler_params=_params("parallel"),
        name="head_norm_rope",
    )(proj, g.reshape(1, LANES), cos, sin)


def _rope_only_kernel(x_ref, cos_ref, sin_ref, o_ref):
    cos, sin = cos_ref[0], sin_ref[0]
    for c in range(x_ref.shape[1] // LANES):
        x = x_ref[:, c * LANES:(c + 1) * LANES]
        o_ref[:, c * LANES:(c + 1) * LANES] = _rope_tile(x, cos, sin, IDX_ROT // 2, IDX_DIM).astype(o_ref.dtype)


def _rope_only(proj, width, col_block, cos, sin):
    t = proj.shape[0]
    bm = _block(t, 512)
    table = pl.BlockSpec((1, bm, LANES), lambda i: (2, i, 0))
    return pl.pallas_call(
        _rope_only_kernel,
        grid=(t // bm,),
        in_specs=[pl.BlockSpec((bm, width), lambda i: (i, col_block)), table, table],
        out_specs=pl.BlockSpec((bm, width), lambda i: (i, 0)),
        out_shape=jax.ShapeDtypeStruct((t, width), BF16),
        compiler_params=_params("parallel"),
        name="indexer_rope",
    )(proj, cos, sin)


SCORE_COLS = 256
COUNT_ACC_ROWS = 64


def _sortable_key(x):
    bits = lax.bitcast_convert_type(x, jnp.int32)
    return bits ^ ((bits >> 31) & 0x7FFFFFFF)


def _indexer_kernel(qi_ref, kit_ref, w_ref, sel_ref, key_ref, wb_ref, qh_ref, *, tq, tk, topk, seq, w_scale):
    i = pl.program_id(1)
    n_tiles = pl.cdiv((i + 1) * tq, tk)
    row_chunk = _chunk_of(i * tq + lax.broadcasted_iota(jnp.int32, (tq, 1), 0))

    w = w_ref[...] * w_scale
    for h in range(IDX_HEADS):
        wb_ref[h] = jnp.broadcast_to(w[:, h:h + 1], (tq, LANES))
        qh_ref[h] = qi_ref[:, h * IDX_DIM:(h + 1) * IDX_DIM]

    def score_tile(j, carry):
        for part in range(tk // SCORE_COLS):
            off = pl.multiple_of(j * tk + part * SCORE_COLS, SCORE_COLS)
            kt = kit_ref[:, pl.ds(off, SCORE_COLS)]
            acc = [jnp.zeros((tq, LANES), F32) for _ in range(SCORE_COLS // LANES)]
            for h in range(IDX_HEADS):
                logits = _dot(qh_ref[h], kt)
                wh = wb_ref[h]
                for c in range(SCORE_COLS // LANES):
                    acc[c] = acc[c] + jnp.maximum(logits[:, c * LANES:(c + 1) * LANES], 0.0) * wh
            for c in range(SCORE_COLS // LANES):
                col = off + c * LANES + lax.broadcasted_iota(jnp.int32, (1, LANES), 1)
                score = jnp.where(_chunk_of(col) <= row_chunk, acc[c], NEG_INF)
                key_ref[pl.ds(off + c * LANES, LANES), :] = _sortable_key(score.T)
        return carry

    lax.fori_loop(0, n_tiles, score_tile, 0)

    def count(pred):
        rows = min(tk, COUNT_ACC_ROWS)

        def tile(j, acc):
            off = pl.multiple_of(j * tk, tk)
            hit = jnp.where(pred(key_ref[pl.ds(off, tk), :], off), 1.0, 0.0)
            return acc + jnp.sum(hit.reshape(tk // rows, rows, tq), axis=0)
        acc = lax.fori_loop(0, n_tiles, tile, jnp.zeros((rows, tq), F32))
        return jnp.sum(acc, axis=0, keepdims=True)

    def value_bit(b, lo):
        cand = lo + lax.shift_left(jnp.int32(1), 31 - b)
        n_ge = count(lambda kt, _: kt >= cand)
        return jnp.where(n_ge >= topk, cand, lo)

    thr = lax.fori_loop(0, 32, value_bit, jnp.full((1, tq), INT_MIN, jnp.int32))
    n_ge = count(lambda kt, _: kt >= thr)
    surplus = jnp.logical_and(n_ge > topk, thr > NEG_INF_KEY)

    def write_tile(j, carry):
        off = pl.multiple_of(j * tk, tk)
        kt = key_ref[pl.ds(off, tk), :]
        chosen = jnp.logical_and(kt >= thr, kt > NEG_INF_KEY)
        sel_ref[pl.ds(off, tk), :] = jnp.where(chosen, 1.0, 0.0).astype(sel_ref.dtype)
        return carry

    lax.fori_loop(0, n_tiles, write_tile, 0)

    def clear_tile(j, carry):
        off = pl.multiple_of(j * tk, tk)
        sel_ref[pl.ds(off, tk), :] = jnp.zeros((tk, tq), sel_ref.dtype)
        return carry

    lax.fori_loop(n_tiles, seq // tk, clear_tile, 0)

    @pl.when(jnp.max(jnp.where(surplus, 1.0, 0.0)) > 0.0)
    def _():
        n_gt = count(lambda kt, _: kt > thr)
        need = topk - n_gt

        def position(first_key):
            return first_key + lax.broadcasted_iota(jnp.int32, (tk, tq), 0)

        def position_bit(b, base):
            cand = base + lax.shift_left(jnp.int32(1), (seq - 1).bit_length() - 1 - b)
            n_before = count(lambda kt, k0: jnp.logical_and(kt == thr, position(k0) < cand))
            return jnp.where(n_before < need, cand, base)

        last = lax.fori_loop(0, (seq - 1).bit_length(), position_bit, jnp.zeros((1, tq), jnp.int32))

        def rewrite_tile(j, carry):
            off = pl.multiple_of(j * tk, tk)
            kt = key_ref[pl.ds(off, tk), :]
            chosen = jnp.logical_or(kt > thr, jnp.logical_and(kt == thr, position(off) <= last))
            chosen = jnp.logical_and(chosen, kt > NEG_INF_KEY)
            sel_ref[pl.ds(off, tk), :] = jnp.where(chosen, 1.0, 0.0).astype(sel_ref.dtype)
            return carry

        lax.fori_loop(0, n_tiles, rewrite_tile, 0)


def _indexer_select(qi, kit, w, w_col_block, batch, topk, tq=128, tk=512):
    s = kit.shape[2]
    tq = _block(s, tq)
    tk = _block(s, tk)
    assert tk % SCORE_COLS == 0
    assert tq == LANES, "score tiles are transposed one (LANES, LANES) tile at a time"
    nq = s // tq
    w_scale = (IDX_DIM ** -0.5) * (IDX_HEADS ** -0.5)
    return pl.pallas_call(
        functools.partial(_indexer_kernel, tq=tq, tk=tk, topk=topk, seq=s, w_scale=w_scale),
        grid=(batch, nq),
        in_specs=[pl.BlockSpec((tq, IDX_HEADS * IDX_DIM), lambda b, i: (b * nq + i, 0)),
                  pl.BlockSpec((None, IDX_DIM, s), lambda b, i: (b, 0, 0)),
                  pl.BlockSpec((tq, LANES), lambda b, i: (b * nq + i, w_col_block))],
        out_specs=pl.BlockSpec((None, s, tq), lambda b, i: (b, 0, i)),
        out_shape=jax.ShapeDtypeStruct((batch, s, s), BF16),
        scratch_shapes=[pltpu.VMEM((s, tq), jnp.int32), pltpu.VMEM((IDX_HEADS, tq, LANES), F32),
                        pltpu.VMEM((IDX_HEADS, tq, IDX_DIM), BF16)],
        compiler_params=_params("parallel", "parallel"),
        name="indexer_select",
    )(qi, kit, w)


def _memory_attn_kernel(q_ref, kv_ref, gq_ref, gk_ref, o_ref, *, scale):
    hd = MEM_HEAD_DIM
    gq, gk = gq_ref[...], gk_ref[...]
    for h in range(MEM_HEADS):
        q = q_ref[:, h * hd:(h + 1) * hd]
        k = kv_ref[:, h * hd:(h + 1) * hd]
        v = kv_ref[:, (MEM_HEADS + h) * hd:(MEM_HEADS + h + 1) * hd].astype(BF16)
        qn = (q * _rms_scale(q, hd) * gq).astype(BF16)
        kn = (k * _rms_scale(k, hd) * gk).astype(BF16)
        s = _dot_nt(qn, kn) * scale
        e = jnp.exp(s - jnp.max(s, axis=-1, keepdims=True))
        p = e / jnp.sum(e, axis=-1, keepdims=True)
        o_ref[:, h * hd:(h + 1) * hd] = _dot(p.astype(BF16), v).astype(o_ref.dtype)


def _memory_attention(proj, col_block, kv, g_q, g_k, batch, tq=512):
    t = proj.shape[0]
    s = t // batch
    m = kv.shape[0] // batch
    width = MEM_HEADS * MEM_HEAD_DIM
    tq = _block(s, tq)
    nq = s // tq
    g_spec = pl.BlockSpec((1, MEM_HEAD_DIM), lambda b, i: (0, 0))
    return pl.pallas_call(
        functools.partial(_memory_attn_kernel, scale=MEM_HEAD_DIM ** -0.5),
        grid=(batch, nq),
        in_specs=[pl.BlockSpec((tq, width), lambda b, i: (b * nq + i, col_block)),
                  pl.BlockSpec((m, 2 * width), lambda b, i: (b, 0)), g_spec, g_spec],
        out_specs=pl.BlockSpec((tq, width), lambda b, i: (b * nq + i, 0)),
        out_shape=jax.ShapeDtypeStruct((t, width), BF16),
        compiler_params=_params("parallel", "parallel"),
        name="memory_attention",
    )(proj, kv, g_q.reshape(1, -1), g_k.reshape(1, -1))


def _projection_layout(q_rank, kv_rank):
    dsa = DSA_HEADS * DSA_HEAD_DIM
    sizes = [("c_q", q_rank), ("c_kv", kv_rank), ("k_r", MLA_ROPE), ("q_d", dsa), ("k_d", dsa), ("v_d", dsa),
             ("q_i", IDX_HEADS * IDX_DIM), ("k_i", IDX_DIM), ("w_i", IDX_HEADS), ("q_m", MEM_HEADS * MEM_HEAD_DIM)]
    src, start = {}, 0
    for name, n in sizes:
        src[name] = (start, n)
        start += n
    return src, start


def _hybrid_mixer(x, h, mem, positions, w_in, g_cq, w_uq, g_ckv, w_ukv, g_q_mla, g_k_mla, g_q_dsa, g_k_dsa,
                  g_mem, w_mem_kv, g_q_mem, g_k_mem, w_o_mla, w_o_dsa, w_o_mem, w_gate, w_out, batch):
    t, d = h.shape
    s = t // batch
    q_rank, kv_rank = w_uq.shape[0], w_ukv.shape[0]
    src, d_in = _projection_layout(q_rank, kv_rank)
    assert d_in == w_in.shape[1]
    dsa = DSA_HEADS * DSA_HEAD_DIM
    assert q_rank == dsa, "the reordered projection layout assumes equal widths for these groups"

    take = lambda name: w_in[:, src[name][0]:src[name][0] + src[name][1]]
    pieces = [("c_q", take("c_q")), ("q_d", take("q_d")), ("k_d", take("k_d")), ("v_d", take("v_d")),
              ("q_i", take("q_i")), ("q_m", take("q_m")), ("c_kv", take("c_kv")),
              ("k_r", jnp.concatenate([take("k_r"), take("k_i")], axis=1)),
              ("w_i", _pad_last(take("w_i"), LANES))]
    col, start = {}, 0
    for name, piece in pieces:
        assert start % piece.shape[1] == 0, name
        col[name] = start
        start += piece.shape[1]
    w_in_r = jnp.concatenate([p for _, p in pieces], axis=1).astype(BF16)

    proj = _matmul(h, w_in_r, F32, bm=1024, bn=768, name="input_projection")
    cos, sin = _rope_tables(positions)

    q_a, k_a, vt_a = _mla_prep(proj, col, g_cq, w_uq, g_ckv, w_ukv, g_q_mla, g_k_mla, cos, sin, batch)
    y_mla = _attention(q_a, k_a, vt_a, None, batch, MLA_HEADS,
                       MLA_QK_PAD, MLA_V, "chunk_causal_attention")

    q_b = _head_norm_rope(proj, col["q_d"] // dsa, g_q_dsa, cos, sin, _query_premultiplier(DSA_HEAD_DIM))
    k_b = _head_norm_rope(proj, col["k_d"] // dsa, g_k_dsa, cos, sin)
    vt_b = _heads_transposed(proj, col["v_d"] // dsa, batch, DSA_HEADS, DSA_HEAD_DIM)
    qi_width = IDX_HEADS * IDX_DIM
    qi = _rope_only(proj, qi_width, col["q_i"] // qi_width, cos, sin)
    ki = _rope_only(proj, LANES, col["k_r"] // LANES, cos, sin)[:, IDX_DIM:]
    kit = ki.reshape(batch, s, IDX_DIM).transpose(0, 2, 1)
    topk = min(DSA_MAX_TOPK, s // 4)
    selected_t = _indexer_select(qi, kit, proj, col["w_i"] // LANES, batch, topk)
    y_dsa = _attention(q_b, k_b, vt_b, selected_t, batch, DSA_HEADS,
                       DSA_HEAD_DIM, DSA_HEAD_DIM, "masked_attention")

    m_len = mem.shape[1]
    mem_n = _rmsnorm(mem.reshape(batch * m_len, d), g_mem)
    kv_mem = _matmul(mem_n, w_mem_kv.astype(BF16), F32, bm=512, bn=512, name="memory_kv")
    y_mem = _memory_attention(proj, col["q_m"] // (MEM_HEADS * MEM_HEAD_DIM), kv_mem, g_q_mem, g_k_mem, batch)

    gated = _gate_combine(h, (y_mla, y_dsa, y_mem), w_gate.astype(BF16),
                          (w_o_mla.astype(BF16), w_o_dsa.astype(BF16), w_o_mem.astype(BF16)))
    return _matmul_residual(gated, w_out.astype(BF16), x, 1.0, bm=1024, name="mixer_out")


def kernel(x, mem, positions, g_ff1, w_ff1_gate, w_ff1_up, w_ff1_down, g_mix, w_in, g_cq, w_uq, g_ckv, w_ukv, g_q_mla, g_k_mla, g_q_dsa, g_k_dsa, g_mem, w_mem_kv, g_q_mem, g_k_mem, w_o_mla, w_o_dsa, w_o_mem, w_gate, w_out, g_ff2, w_ff2_gate, w_ff2_up, w_ff2_down):
    batch, seq, d = x.shape
    xt = x.reshape(batch * seq, d)
    for l in range(g_ff1.shape[0]):
        xt = _ffn_half_step(xt, g_ff1[l], w_ff1_gate[l], w_ff1_up[l], w_ff1_down[l])
        h = _rmsnorm(xt, g_mix[l])
        xt = _hybrid_mixer(xt, h, mem, positions, w_in[l], g_cq[l], w_uq[l], g_ckv[l], w_ukv[l],
                           g_q_mla[l], g_k_mla[l], g_q_dsa[l], g_k_dsa[l], g_mem[l], w_mem_kv[l],
                           g_q_mem[l], g_k_mem[l], w_o_mla[l], w_o_dsa[l], w_o_mem[l], w_gate[l], w_out[l], batch)
        xt = _ffn_half_step(xt, g_ff2[l], w_ff2_gate[l], w_ff2_up[l], w_ff2_down[l])
    return xt.reshape(batch, seq, d)
```

```python
import functools

import jax
import jax.numpy as jnp
from jax import lax
from jax.experimental import pallas as pl
from jax.experimental.pallas import tpu as pltpu

CHUNK = 64
ROPE_THETA = 500000.0
EPS = 1e-6

MLA_HEADS = 12
MLA_NOPE = 128
MLA_ROPE = 64
MLA_V = 128
MLA_QK = MLA_ROPE + MLA_NOPE
MLA_QK_PAD = 256

DSA_HEADS = 12
DSA_HEAD_DIM = 128
DSA_ROT = DSA_HEAD_DIM // 4
DSA_MAX_TOPK = 256
IDX_HEADS = 32
IDX_DIM = 64
IDX_ROT = IDX_DIM // 4

MEM_HEADS = 4
MEM_HEAD_DIM = 256

LANES = 128
SUBLANES = 8
VMEM_LIMIT_BYTES = 56 * 1024 * 1024

F32 = jnp.float32
BF16 = jnp.bfloat16
NEG_INF = float("-inf")
INT_MIN = -2 ** 31
LOG2_E = 1.4426950408889634
NEG_INF_KEY = (0xFF800000 - (1 << 32)) ^ 0x7FFFFFFF


def _params(*semantics):
    return pltpu.CompilerParams(dimension_semantics=semantics, vmem_limit_bytes=VMEM_LIMIT_BYTES)


def _block(dim, preferred):
    if dim <= preferred:
        return dim
    b = preferred - preferred % LANES
    while b >= LANES:
        if dim % b == 0:
            return b
        b -= LANES
    return dim


def _rms_scale(x, width):
    return lax.rsqrt(jnp.sum(x * x, axis=-1, keepdims=True) * (1.0 / width) + EPS)


def _dot(a, b):
    return jnp.dot(a, b, preferred_element_type=F32)


def _dot_nt(a, b):
    return lax.dot_general(a, b, (((1,), (1,)), ((), ())), preferred_element_type=F32)


def _sigmoid(x):
    return 1.0 / (1.0 + jnp.exp(-x))


def _query_premultiplier(head_dim):
    return head_dim ** -0.5 * LOG2_E


def _chunk_of(frame):
    return lax.shift_right_logical(frame, CHUNK.bit_length() - 1)


def _rmsnorm_kernel(x_ref, g_ref, o_ref):
    x = x_ref[...]
    o_ref[...] = (x * _rms_scale(x, x.shape[-1]) * g_ref[...]).astype(o_ref.dtype)


def _rmsnorm(x, g, out_dtype=BF16):
    m, d = x.shape
    bm = _block(m, 512)
    return pl.pallas_call(
        _rmsnorm_kernel,
        grid=(m // bm,),
        in_specs=[pl.BlockSpec((bm, d), lambda i: (i, 0)), pl.BlockSpec((1, d), lambda i: (0, 0))],
        out_specs=pl.BlockSpec((bm, d), lambda i: (i, 0)),
        out_shape=jax.ShapeDtypeStruct((m, d), out_dtype),
        compiler_params=_params("parallel"),
        name="rmsnorm",
    )(x, g.reshape(1, d))


def _matmul_kernel(a_ref, b_ref, o_ref):
    o_ref[...] = _dot(a_ref[...], b_ref[...]).astype(o_ref.dtype)


def _matmul(a, b, out_dtype, bm=1024, bn=512, name="matmul"):
    m, k = a.shape
    n = b.shape[1]
    bm, bn = _block(m, bm), _block(n, bn)
    return pl.pallas_call(
        _matmul_kernel,
        grid=(m // bm, n // bn),
        in_specs=[pl.BlockSpec((bm, k), lambda i, j: (i, 0)), pl.BlockSpec((k, bn), lambda i, j: (0, j))],
        out_specs=pl.BlockSpec((bm, bn), lambda i, j: (i, j)),
        out_shape=jax.ShapeDtypeStruct((m, n), out_dtype),
        compiler_params=_params("parallel", "parallel"),
        name=name,
    )(a, b)


def _swiglu_up_kernel(a_ref, wg_ref, wu_ref, o_ref):
    a = a_ref[...]
    g = _dot(a, wg_ref[...])
    u = _dot(a, wu_ref[...])
    o_ref[...] = (g * _sigmoid(g) * u).astype(o_ref.dtype)


def _swiglu_up(a, wg, wu, bm=1024, bn=512):
    m, k = a.shape
    n = wg.shape[1]
    bm, bn = _block(m, bm), _block(n, bn)
    return pl.pallas_call(
        _swiglu_up_kernel,
        grid=(m // bm, n // bn),
        in_specs=[pl.BlockSpec((bm, k), lambda i, j: (i, 0)),
                  pl.BlockSpec((k, bn), lambda i, j: (0, j)),
                  pl.BlockSpec((k, bn), lambda i, j: (0, j))],
        out_specs=pl.BlockSpec((bm, bn), lambda i, j: (i, j)),
        out_shape=jax.ShapeDtypeStruct((m, n), BF16),
        compiler_params=_params("parallel", "parallel"),
        name="swiglu_up",
    )(a, wg, wu)


def _matmul_residual_kernel(a_ref, b_ref, x_ref, o_ref, *, scale):
    o_ref[...] = x_ref[...] + scale * _dot(a_ref[...], b_ref[...])


def _matmul_residual(a, b, x, scale, bm=512, bn=512, name="matmul_residual"):
    m, k = a.shape
    n = b.shape[1]
    bm, bn = _block(m, bm), _block(n, bn)
    return pl.pallas_call(
        functools.partial(_matmul_residual_kernel, scale=scale),
        grid=(m // bm, n // bn),
        in_specs=[pl.BlockSpec((bm, k), lambda i, j: (i, 0)),
                  pl.BlockSpec((k, bn), lambda i, j: (0, j)),
                  pl.BlockSpec((bm, bn), lambda i, j: (i, j))],
        out_specs=pl.BlockSpec((bm, bn), lambda i, j: (i, j)),
        out_shape=jax.ShapeDtypeStruct((m, n), F32),
        compiler_params=_params("parallel", "parallel"),
        name=name,
    )(a, b, x)


def _ffn_half_step(x, g, w_gate, w_up, w_down):
    h = _rmsnorm(x, g)
    a = _swiglu_up(h, w_gate.astype(BF16), w_up.astype(BF16))
    return _matmul_residual(a, w_down.astype(BF16), x, 0.5, name="ffn_down")


def _gate_combine_kernel(h_ref, ya_ref, yb_ref, yc_ref, wg_ref, wa_ref, wb_ref, wc_ref, o_ref):
    h = h_ref[...]
    out = _sigmoid(_dot(h, wg_ref[0])) * _dot(ya_ref[...], wa_ref[...])
    out = out + _sigmoid(_dot(h, wg_ref[1])) * _dot(yb_ref[...], wb_ref[...])
    out = out + _sigmoid(_dot(h, wg_ref[2])) * _dot(yc_ref[...], wc_ref[...])
    o_ref[...] = out.astype(o_ref.dtype)


def _gate_combine(h, ys, w_gate, w_os, bm=512, bn=512):
    m, d = h.shape
    n = w_gate.shape[-1]
    bm, bn = _block(m, bm), _block(n, bn)
    row = lambda w: pl.BlockSpec((bm, w), lambda i, j: (i, 0))
    col = lambda k: pl.BlockSpec((k, bn), lambda i, j: (0, j))
    return pl.pallas_call(
        _gate_combine_kernel,
        grid=(m // bm, n // bn),
        in_specs=[row(d)] + [row(y.shape[1]) for y in ys]
                 + [pl.BlockSpec((3, d, bn), lambda i, j: (0, 0, j))] + [col(w.shape[0]) for w in w_os],
        out_specs=pl.BlockSpec((bm, bn), lambda i, j: (i, j)),
        out_shape=jax.ShapeDtypeStruct((m, n), BF16),
        compiler_params=_params("parallel", "parallel"),
        name="gate_combine",
    )(h, *ys, w_gate, *w_os)


def _rope_tables_kernel(pos_ref, invf_ref, sgn_ref, cos_ref, sin_ref):
    pos = pos_ref[...].astype(F32)
    for t in range(invf_ref.shape[0]):
        ang = pos * invf_ref[t:t + 1, :]
        cos_ref[t] = jnp.cos(ang)
        sin_ref[t] = jnp.sin(ang) * sgn_ref[t:t + 1, :]


def _rope_lane_tables(rot, period):
    half = rot // 2
    inv_freq = ROPE_THETA ** (-jnp.arange(half, dtype=F32) / half)
    lane = jnp.arange(LANES) % period
    invf = jnp.where(lane < rot, inv_freq[lane % half], 0.0)
    sgn = jnp.where(lane < half, -1.0, jnp.where(lane < rot, 1.0, 0.0))
    return invf.astype(F32), sgn.astype(F32)


def _rope_tables(positions):
    t = positions.size
    specs = [_rope_lane_tables(MLA_ROPE, LANES), _rope_lane_tables(DSA_ROT, LANES),
             _rope_lane_tables(IDX_ROT, IDX_DIM)]
    invf = jnp.stack([s[0] for s in specs])
    sgn = jnp.stack([s[1] for s in specs])
    bm = _block(t, 1024)
    n = invf.shape[0]
    const = pl.BlockSpec((n, LANES), lambda i: (0, 0))
    out = pl.BlockSpec((n, bm, LANES), lambda i: (0, i, 0))
    return pl.pallas_call(
        _rope_tables_kernel,
        grid=(t // bm,),
        in_specs=[pl.BlockSpec((bm, 1), lambda i: (i, 0)), const, const],
        out_specs=[out, out],
        out_shape=[jax.ShapeDtypeStruct((n, t, LANES), F32)] * 2,
        compiler_params=_params("parallel"),
        name="rope_tables",
    )(positions.reshape(t, 1), invf, sgn)


def _rope_tile(x, cos, sin, half, period):
    lane = lax.broadcasted_iota(jnp.int32, x.shape, 1) & (period - 1)
    partner = jnp.where(lane < half, pltpu.roll(x, LANES - half, 1), pltpu.roll(x, half, 1))
    return x * cos + partner * sin


def _mla_q_kernel(cq_ref, gcq_ref, w_ref, gq_ref, cos_ref, sin_ref, o_ref):
    x = cq_ref[...]
    xn = (x * _rms_scale(x, x.shape[-1]) * gcq_ref[...]).astype(BF16)
    q = _dot(xn, w_ref[...])
    cos, sin = cos_ref[0], sin_ref[0]
    g = gq_ref[...] * _query_premultiplier(MLA_QK)
    for h in range(MLA_HEADS):
        lo = h * MLA_QK_PAD
        qh = q[:, lo:lo + MLA_QK_PAD]
        y = qh * _rms_scale(qh, MLA_QK) * g
        o_ref[:, lo:lo + LANES] = _rope_tile(y[:, :LANES], cos, sin, MLA_ROPE // 2, LANES).astype(o_ref.dtype)
        o_ref[:, lo + LANES:lo + MLA_QK_PAD] = y[:, LANES:].astype(o_ref.dtype)


def _store_heads_transposed(x, vt_ref):
    heads, hd, _ = vt_ref.shape
    for h in range(heads):
        vt_ref[h] = x[:, h * hd:(h + 1) * hd].T.astype(vt_ref.dtype)


def _mla_kv_kernel(ckv_ref, kr_ref, gckv_ref, wk_ref, wv_ref, gk_ref, cos_ref, sin_ref, k_ref, vt_ref):
    x = ckv_ref[...]
    xn = (x * _rms_scale(x, x.shape[-1]) * gckv_ref[...]).astype(BF16)
    k_nope = _dot(xn, wk_ref[...])
    _store_heads_transposed(_dot(xn, wv_ref[...]), vt_ref)
    kr = kr_ref[...]
    lane = lax.broadcasted_iota(jnp.int32, kr.shape, 1)
    kr = jnp.where(lane < MLA_ROPE, kr, 0.0)
    cos, sin = cos_ref[0], sin_ref[0]
    g = gk_ref[...]
    for h in range(MLA_HEADS):
        lo = h * MLA_QK_PAD
        ka = k_nope[:, lo:lo + LANES] + kr
        kb = k_nope[:, lo + LANES:lo + MLA_QK_PAD]
        ms = (jnp.sum(ka * ka, axis=-1, keepdims=True) + jnp.sum(kb * kb, axis=-1, keepdims=True)) * (1.0 / MLA_QK)
        r = lax.rsqrt(ms + EPS)
        ya = ka * r * g[:, :LANES]
        yb = kb * r * g[:, LANES:]
        k_ref[:, lo:lo + LANES] = _rope_tile(ya, cos, sin, MLA_ROPE // 2, LANES).astype(k_ref.dtype)
        k_ref[:, lo + LANES:lo + MLA_QK_PAD] = yb.astype(k_ref.dtype)


def _pad_last(a, width):
    return jnp.pad(a, [(0, 0)] * (a.ndim - 1) + [(0, width - a.shape[-1])])


def _mla_prep(proj, col, g_cq, w_uq, g_ckv, w_ukv, g_q, g_k, cos, sin, batch):
    t = proj.shape[0]
    q_rank, kv_rank = w_uq.shape[0], w_ukv.shape[0]
    hp = MLA_HEADS * MLA_QK_PAD
    w_q = _pad_last(w_uq.reshape(q_rank, MLA_HEADS, MLA_QK), MLA_QK_PAD).reshape(q_rank, hp).astype(BF16)
    w_kv = w_ukv.reshape(kv_rank, MLA_HEADS, MLA_NOPE + MLA_V)
    w_k = jnp.pad(w_kv[..., :MLA_NOPE], ((0, 0), (0, 0), (MLA_ROPE, MLA_QK_PAD - MLA_QK)))
    w_k = w_k.reshape(kv_rank, hp).astype(BF16)
    w_v = w_kv[..., MLA_NOPE:].reshape(kv_rank, MLA_HEADS * MLA_V).astype(BF16)
    g_qp = _pad_last(g_q, MLA_QK_PAD).reshape(1, MLA_QK_PAD)
    g_kp = _pad_last(g_k, MLA_QK_PAD).reshape(1, MLA_QK_PAD)

    bm = _block(t, 512)
    rows = lambda w, c: pl.BlockSpec((bm, w), lambda i, c=c: (i, c))
    full = lambda a: pl.BlockSpec(a.shape, lambda i: (0,) * a.ndim)
    table = pl.BlockSpec((1, bm, LANES), lambda i: (0, i, 0))
    g_cq2, g_ckv2 = g_cq.reshape(1, -1), g_ckv.reshape(1, -1)

    q = pl.pallas_call(
        _mla_q_kernel,
        grid=(t // bm,),
        in_specs=[rows(q_rank, col["c_q"] // q_rank), full(g_cq2), full(w_q), full(g_qp), table, table],
        out_specs=pl.BlockSpec((bm, hp), lambda i: (i, 0)),
        out_shape=jax.ShapeDtypeStruct((t, hp), BF16),
        compiler_params=_params("parallel"),
        name="mla_q_prep",
    )(proj, g_cq2, w_q, g_qp, cos, sin)

    k, v = pl.pallas_call(
        _mla_kv_kernel,
        grid=(t // bm,),
        in_specs=[rows(kv_rank, col["c_kv"] // kv_rank), rows(LANES, col["k_r"] // LANES), full(g_ckv2),
                  full(w_k), full(w_v), full(g_kp), table, table],
        out_specs=[pl.BlockSpec((bm, hp), lambda i: (i, 0)),
                   _heads_transposed_spec(MLA_HEADS, MLA_V, bm, t // batch)],
        out_shape=[jax.ShapeDtypeStruct((t, hp), BF16),
                   jax.ShapeDtypeStruct((batch, MLA_HEADS, MLA_V, t // batch), BF16)],
        compiler_params=_params("parallel"),
        name="mla_kv_prep",
    )(proj, proj, g_ckv2, w_k, w_v, g_kp, cos, sin)
    return q, k, v


ATTN_HEADS_PER_STEP = 3


def _attn_scratch(tq, dk, dv):
    per_head = [pltpu.VMEM((1, tq), F32), pltpu.VMEM((1, tq), F32), pltpu.VMEM((dv, tq), F32)]
    per_parity = [pltpu.VMEM((tq, tq), F32), pltpu.VMEM((1, tq), F32),
                  pltpu.VMEM((tq, tq), BF16), pltpu.VMEM((1, tq), F32)]
    return (per_head + 2 * per_parity) * ATTN_HEADS_PER_STEP + [pltpu.VMEM((ATTN_HEADS_PER_STEP, dk, tq), BF16)]


def _attn_kernel(q_ref, k_ref, vt_ref, *rest, tq, dk, dv, selected_keys):
    n_scratch = 11 * ATTN_HEADS_PER_STEP + 1
    scratch = rest[-n_scratch:]
    qt_ref = scratch[-1]
    sel_ref = rest[0] if selected_keys else None
    o_ref = rest[-n_scratch - 1]
    i = pl.program_id(2)
    heads = range(ATTN_HEADS_PER_STEP)
    m_of, l_of, acc_of, s_of, mx_of, p_of, al_of = [], [], [], [], [], [], []
    for h in heads:
        refs = scratch[11 * h:11 * (h + 1)]
        m_of.append(refs[0]); l_of.append(refs[1]); acc_of.append(refs[2])
        s_of.append((refs[3], refs[7])); mx_of.append((refs[4], refs[8]))
        p_of.append((refs[5], refs[9])); al_of.append((refs[6], refs[10]))
    for h in heads:
        qt_ref[h] = q_ref[:, h * dk:(h + 1) * dk].T

    for h in heads:
        m_of[h][...] = jnp.full((1, tq), NEG_INF, F32)
        l_of[h][...] = jnp.zeros((1, tq), F32)
        acc_of[h][...] = jnp.zeros((dv, tq), F32)
        s_of[h][1][...] = jnp.full((tq, tq), NEG_INF, F32)
        mx_of[h][1][...] = jnp.full((1, tq), NEG_INF, F32)
        p_of[h][0][...] = jnp.zeros((tq, tq), BF16)
        al_of[h][0][...] = jnp.zeros((1, tq), F32)

    def scores(block, par, diagonal):
        off = pl.multiple_of(block * tq, tq)
        if selected_keys:
            visible = sel_ref[pl.ds(off, tq), :].astype(F32) > 0.0
        elif diagonal:
            key_chunk = _chunk_of(lax.broadcasted_iota(jnp.int32, (tq, tq), 0))
            query_chunk = _chunk_of(lax.broadcasted_iota(jnp.int32, (tq, tq), 1))
            visible = key_chunk <= query_chunk
        else:
            visible = None
        for h in heads:
            s = _dot(k_ref[pl.ds(off, tq), h * dk:(h + 1) * dk], qt_ref[h])
            if visible is not None:
                s = jnp.where(visible, s, NEG_INF)
            s_of[h][par][...] = s
            mx_of[h][par][...] = jnp.max(s, axis=0, keepdims=True)

    def softmax(par):
        for h in heads:
            m_prev = m_of[h][...]
            m_new = jnp.maximum(m_prev, mx_of[h][par][...])
            m_safe = jnp.where(m_new == NEG_INF, 0.0, m_new)
            alpha = jnp.exp2(m_prev - m_safe)
            p = jnp.exp2(s_of[h][par][...] - m_safe)
            l_of[h][...] = alpha * l_of[h][...] + jnp.sum(p, axis=0, keepdims=True)
            p_of[h][par][...] = p.astype(BF16)
            al_of[h][par][...] = alpha
            m_of[h][...] = m_new

    def values(block, par):
        off = pl.multiple_of(jnp.maximum(block, 0) * tq, tq)
        for h in heads:
            acc_of[h][...] = al_of[h][par][...] * acc_of[h][...] + _dot(vt_ref[h, :, pl.ds(off, tq)],
                                                                         p_of[h][par][...])

    def step(t, par, diagonal=False):
        values(t - 2, par)
        softmax(1 - par)
        scores(t, par, diagonal)

    def pair(u, carry):
        step(2 * u, 0)
        step(2 * u + 1, 1)
        return carry

    lax.fori_loop(0, lax.shift_right_logical(i, 1), pair, 0)

    def drain(par):
        step(i, par, diagonal=True)
        values(i - 1, 1 - par)
        softmax(par)
        values(i, par)

    odd = (i & 1) == 1

    @pl.when(odd)
    def _():
        step(i - 1, 0)
        drain(1)

    @pl.when(jnp.logical_not(odd))
    def _():
        drain(0)

    for h in heads:
        o = acc_of[h][...] / l_of[h][...]
        o_ref[:, h * dv:(h + 1) * dv] = o.T.astype(o_ref.dtype)


def _attention(q, k, vt, selected_t, batch, heads, dk, dv, name, tq=256):
    t = q.shape[0]
    s = t // batch
    tq = _block(s, tq)
    nq = s // tq
    hp = ATTN_HEADS_PER_STEP
    in_specs = [pl.BlockSpec((tq, hp * dk), lambda b, h, i: (b * nq + i, h)),
                pl.BlockSpec((s, hp * dk), lambda b, h, i: (b, h)),
                pl.BlockSpec((None, hp, dv, s), lambda b, h, i: (b, h, 0, 0))]
    operands = [q, k, vt]
    if selected_t is not None:
        in_specs.append(pl.BlockSpec((None, s, tq), lambda b, h, i: (b, 0, i)))
        operands.append(selected_t)
    return pl.pallas_call(
        functools.partial(_attn_kernel, tq=tq, dk=dk, dv=dv, selected_keys=selected_t is not None),
        grid=(batch, heads // hp, nq),
        in_specs=in_specs,
        out_specs=pl.BlockSpec((tq, hp * dv), lambda b, h, i: (b * nq + i, h)),
        out_shape=jax.ShapeDtypeStruct((t, heads * dv), BF16),
        scratch_shapes=_attn_scratch(tq, dk, dv),
        compiler_params=_params("parallel", "parallel", "parallel"),
        name=name,
    )(*operands)


def _heads_transposed_spec(heads, hd, bm, seq):
    blocks_per_batch = seq // bm
    return pl.BlockSpec((None, heads, hd, bm), lambda i: (i // blocks_per_batch, 0, 0, i % blocks_per_batch))


def _heads_transposed_kernel(x_ref, vt_ref):
    _store_heads_transposed(x_ref[...], vt_ref)


def _heads_transposed(proj, col_block, batch, heads, hd):
    t = proj.shape[0]
    bm = _block(t // batch, 512)
    return pl.pallas_call(
        _heads_transposed_kernel,
        grid=(t // bm,),
        in_specs=[pl.BlockSpec((bm, heads * hd), lambda i: (i, col_block))],
        out_specs=_heads_transposed_spec(heads, hd, bm, t // batch),
        out_shape=jax.ShapeDtypeStruct((batch, heads, hd, t // batch), BF16),
        compiler_params=_params("parallel"),
        name="heads_transposed",
    )(proj)


def _head_norm_rope_kernel(x_ref, g_ref, cos_ref, sin_ref, o_ref, *, heads, premultiplier):
    cos, sin = cos_ref[0], sin_ref[0]
    g = g_ref[...] * premultiplier
    for h in range(heads):
        x = x_ref[:, h * LANES:(h + 1) * LANES]
        y = x * _rms_scale(x, LANES) * g
        o_ref[:, h * LANES:(h + 1) * LANES] = _rope_tile(y, cos, sin, DSA_ROT // 2, LANES).astype(o_ref.dtype)


def _head_norm_rope(proj, col_block, g, cos, sin, premultiplier=1.0):
    t = proj.shape[0]
    width = DSA_HEADS * DSA_HEAD_DIM
    bm = _block(t, 512)
    table = pl.BlockSpec((1, bm, LANES), lambda i: (1, i, 0))
    return pl.pallas_call(
        functools.partial(_head_norm_rope_kernel, heads=DSA_HEADS, premultiplier=premultiplier),
        grid=(t // bm,),
        in_specs=[pl.BlockSpec((bm, width), lambda i: (i, col_block)),
                  pl.BlockSpec((1, LANES), lambda i: (0, 0)), table, table],
        out_specs=pl.BlockSpec((bm, width), lambda i: (i, 0)),
        out_shape=jax.ShapeDtypeStruct((t, width), BF16),
        compiler_params=_params("parallel"),
        name="head_norm_rope",
    )(proj, g.reshape(1, LANES), cos, sin)


def _rope_only_kernel(x_ref, cos_ref, sin_ref, o_ref):
    cos, sin = cos_ref[0], sin_ref[0]
    for c in range(x_ref.shape[1] // LANES):
        x = x_ref[:, c * LANES:(c + 1) * LANES]
        o_ref[:, c * LANES:(c + 1) * LANES] = _rope_tile(x, cos, sin, IDX_ROT // 2, IDX_DIM).astype(o_ref.dtype)


def _rope_only(proj, width, col_block, cos, sin):
    t = proj.shape[0]
    bm = _block(t, 512)
    table = pl.BlockSpec((1, bm, LANES), lambda i: (2, i, 0))
    return pl.pallas_call(
        _rope_only_kernel,
        grid=(t // bm,),
        in_specs=[pl.BlockSpec((bm, width), lambda i: (i, col_block)), table, table],
        out_specs=pl.BlockSpec((bm, width), lambda i: (i, 0)),
        out_shape=jax.ShapeDtypeStruct((t, width), BF16),
        compiler_params=_params("parallel"),
        name="indexer_rope",
    )(proj, cos, sin)


SCORE_COLS = 256
COUNT_ACC_ROWS = 64


def _sortable_key(x):
    bits = lax.bitcast_convert_type(x, jnp.int32)
    return bits ^ ((bits >> 31) & 0x7FFFFFFF)


def _indexer_kernel(qi_ref, kit_ref, w_ref, *rest, tq, tk, topk, seq, w_scale, n_casts):
    cast_src, sel_ref, cast_dst = rest[:n_casts], rest[n_casts], rest[n_casts + 1:2 * n_casts + 1]
    key_ref, wb_ref, qh_ref = rest[2 * n_casts + 1:]
    for src, dst in zip(cast_src, cast_dst):
        dst[...] = src[...].astype(dst.dtype)

    i = pl.program_id(1)
    n_tiles = pl.cdiv((i + 1) * tq, tk)
    row_chunk = _chunk_of(i * tq + lax.broadcasted_iota(jnp.int32, (tq, 1), 0))

    w = w_ref[...] * w_scale
    for h in range(IDX_HEADS):
        wb_ref[h] = jnp.broadcast_to(w[:, h:h + 1], (tq, LANES))
        qh_ref[h] = qi_ref[:, h * IDX_DIM:(h + 1) * IDX_DIM]

    def score_tile(j, carry):
        for part in range(tk // SCORE_COLS):
            off = pl.multiple_of(j * tk + part * SCORE_COLS, SCORE_COLS)
            kt = kit_ref[:, pl.ds(off, SCORE_COLS)]
            acc = [jnp.zeros((tq, LANES), F32) for _ in range(SCORE_COLS // LANES)]
            for h in range(IDX_HEADS):
                logits = _dot(qh_ref[h], kt)
                wh = wb_ref[h]
                for c in range(SCORE_COLS // LANES):
                    acc[c] = acc[c] + jnp.maximum(logits[:, c * LANES:(c + 1) * LANES], 0.0) * wh
            for c in range(SCORE_COLS // LANES):
                col = off + c * LANES + lax.broadcasted_iota(jnp.int32, (1, LANES), 1)
                score = jnp.where(_chunk_of(col) <= row_chunk, acc[c], NEG_INF)
                key_ref[pl.ds(off + c * LANES, LANES), :] = _sortable_key(score.T)
        return carry

    lax.fori_loop(0, n_tiles, score_tile, 0)

    def count(pred):
        rows = min(tk, COUNT_ACC_ROWS)

        def tile(j, acc):
            off = pl.multiple_of(j * tk, tk)
            hit = jnp.where(pred(key_ref[pl.ds(off, tk), :], off), 1.0, 0.0)
            return acc + jnp.sum(hit.reshape(tk // rows, rows, tq), axis=0)
        acc = lax.fori_loop(0, n_tiles, tile, jnp.zeros((rows, tq), F32))
        return jnp.sum(acc, axis=0, keepdims=True)

    def value_bit(b, lo):
        cand = lo + lax.shift_left(jnp.int32(1), 31 - b)
        n_ge = count(lambda kt, _: kt >= cand)
        return jnp.where(n_ge >= topk, cand, lo)

    thr = lax.fori_loop(0, 32, value_bit, jnp.full((1, tq), INT_MIN, jnp.int32))
    n_ge = count(lambda kt, _: kt >= thr)
    surplus = jnp.logical_and(n_ge > topk, thr > NEG_INF_KEY)

    def write_tile(j, carry):
        off = pl.multiple_of(j * tk, tk)
        kt = key_ref[pl.ds(off, tk), :]
        chosen = jnp.logical_and(kt >= thr, kt > NEG_INF_KEY)
        sel_ref[pl.ds(off, tk), :] = jnp.where(chosen, 1.0, 0.0).astype(sel_ref.dtype)
        return carry

    lax.fori_loop(0, n_tiles, write_tile, 0)

    def clear_tile(j, carry):
        off = pl.multiple_of(j * tk, tk)
        sel_ref[pl.ds(off, tk), :] = jnp.zeros((tk, tq), sel_ref.dtype)
        return carry

    lax.fori_loop(n_tiles, seq // tk, clear_tile, 0)

    @pl.when(jnp.max(jnp.where(surplus, 1.0, 0.0)) > 0.0)
    def _():
        n_gt = count(lambda kt, _: kt > thr)
        need = topk - n_gt

        def position(first_key):
            return first_key + lax.broadcasted_iota(jnp.int32, (tk, tq), 0)

        def position_bit(b, base):
            cand = base + lax.shift_left(jnp.int32(1), (seq - 1).bit_length() - 1 - b)
            n_before = count(lambda kt, k0: jnp.logical_and(kt == thr, position(k0) < cand))
            return jnp.where(n_before < need, cand, base)

        last = lax.fori_loop(0, (seq - 1).bit_length(), position_bit, jnp.zeros((1, tq), jnp.int32))

        def rewrite_tile(j, carry):
            off = pl.multiple_of(j * tk, tk)
            kt = key_ref[pl.ds(off, tk), :]
            chosen = jnp.logical_or(kt > thr, jnp.logical_and(kt == thr, position(off) <= last))
            chosen = jnp.logical_and(chosen, kt > NEG_INF_KEY)
            sel_ref[pl.ds(off, tk), :] = jnp.where(chosen, 1.0, 0.0).astype(sel_ref.dtype)
            return carry

        lax.fori_loop(0, n_tiles, rewrite_tile, 0)


def _indexer_select(qi, kit, w, w_col_block, batch, topk, weights_to_cast=(), tq=128, tk=512):
    s = kit.shape[2]
    tq = _block(s, tq)
    tk = _block(s, tk)
    assert tk % SCORE_COLS == 0
    assert tq == LANES, "score tiles are transposed one (LANES, LANES) tile at a time"
    nq = s // tq
    steps = batch * nq
    w_scale = (IDX_DIM ** -0.5) * (IDX_HEADS ** -0.5)

    slabs = []
    for a in weights_to_cast:
        rows = a.size // a.shape[-1]
        assert rows % steps == 0, a.shape
        slabs.append(a.reshape(steps, rows // steps, a.shape[-1]))
    slab_spec = lambda a: pl.BlockSpec((None,) + a.shape[1:], lambda b, i: (b * nq + i, 0, 0))

    outs = pl.pallas_call(
        functools.partial(_indexer_kernel, tq=tq, tk=tk, topk=topk, seq=s, w_scale=w_scale, n_casts=len(slabs)),
        grid=(batch, nq),
        in_specs=[pl.BlockSpec((tq, IDX_HEADS * IDX_DIM), lambda b, i: (b * nq + i, 0)),
                  pl.BlockSpec((None, IDX_DIM, s), lambda b, i: (b, 0, 0)),
                  pl.BlockSpec((tq, LANES), lambda b, i: (b * nq + i, w_col_block))]
                 + [slab_spec(a) for a in slabs],
        out_specs=[pl.BlockSpec((None, s, tq), lambda b, i: (b, 0, i))] + [slab_spec(a) for a in slabs],
        out_shape=[jax.ShapeDtypeStruct((batch, s, s), BF16)]
                  + [jax.ShapeDtypeStruct(a.shape, BF16) for a in slabs],
        scratch_shapes=[pltpu.VMEM((s, tq), jnp.int32), pltpu.VMEM((IDX_HEADS, tq, LANES), F32),
                        pltpu.VMEM((IDX_HEADS, tq, IDX_DIM), BF16)],
        compiler_params=_params("parallel", "parallel"),
        name="indexer_select",
    )(qi, kit, w, *slabs)
    return outs[0], tuple(o.reshape(a.shape) for o, a in zip(outs[1:], weights_to_cast))


def _memory_attn_kernel(q_ref, kv_ref, gq_ref, gk_ref, o_ref, *, scale):
    hd = MEM_HEAD_DIM
    gq, gk = gq_ref[...], gk_ref[...]
    for h in range(MEM_HEADS):
        q = q_ref[:, h * hd:(h + 1) * hd]
        k = kv_ref[:, h * hd:(h + 1) * hd]
        v = kv_ref[:, (MEM_HEADS + h) * hd:(MEM_HEADS + h + 1) * hd].astype(BF16)
        qn = (q * _rms_scale(q, hd) * gq).astype(BF16)
        kn = (k * _rms_scale(k, hd) * gk).astype(BF16)
        s = _dot_nt(qn, kn) * scale
        e = jnp.exp(s - jnp.max(s, axis=-1, keepdims=True))
        p = e / jnp.sum(e, axis=-1, keepdims=True)
        o_ref[:, h * hd:(h + 1) * hd] = _dot(p.astype(BF16), v).astype(o_ref.dtype)


def _memory_attention(proj, col_block, kv, g_q, g_k, batch, tq=512):
    t = proj.shape[0]
    s = t // batch
    m = kv.shape[0] // batch
    width = MEM_HEADS * MEM_HEAD_DIM
    tq = _block(s, tq)
    nq = s // tq
    g_spec = pl.BlockSpec((1, MEM_HEAD_DIM), lambda b, i: (0, 0))
    return pl.pallas_call(
        functools.partial(_memory_attn_kernel, scale=MEM_HEAD_DIM ** -0.5),
        grid=(batch, nq),
        in_specs=[pl.BlockSpec((tq, width), lambda b, i: (b * nq + i, col_block)),
                  pl.BlockSpec((m, 2 * width), lambda b, i: (b, 0)), g_spec, g_spec],
        out_specs=pl.BlockSpec((tq, width), lambda b, i: (b * nq + i, 0)),
        out_shape=jax.ShapeDtypeStruct((t, width), BF16),
        compiler_params=_params("parallel", "parallel"),
        name="memory_attention",
    )(proj, kv, g_q.reshape(1, -1), g_k.reshape(1, -1))


def _projection_layout(q_rank, kv_rank):
    dsa = DSA_HEADS * DSA_HEAD_DIM
    sizes = [("c_q", q_rank), ("c_kv", kv_rank), ("k_r", MLA_ROPE), ("q_d", dsa), ("k_d", dsa), ("v_d", dsa),
             ("q_i", IDX_HEADS * IDX_DIM), ("k_i", IDX_DIM), ("w_i", IDX_HEADS), ("q_m", MEM_HEADS * MEM_HEAD_DIM)]
    src, start = {}, 0
    for name, n in sizes:
        src[name] = (start, n)
        start += n
    return src, start


def _hybrid_mixer(x, h, mem, positions, w_in, g_cq, w_uq, g_ckv, w_ukv, g_q_mla, g_k_mla, g_q_dsa, g_k_dsa,
                  g_mem, w_mem_kv, g_q_mem, g_k_mem, w_o_mla, w_o_dsa, w_o_mem, w_gate, w_out, batch, later_weights):
    t, d = h.shape
    s = t // batch
    q_rank, kv_rank = w_uq.shape[0], w_ukv.shape[0]
    src, d_in = _projection_layout(q_rank, kv_rank)
    assert d_in == w_in.shape[1]
    dsa = DSA_HEADS * DSA_HEAD_DIM
    assert q_rank == dsa, "the reordered projection layout assumes equal widths for these groups"

    take = lambda name: w_in[:, src[name][0]:src[name][0] + src[name][1]]
    pieces = [("c_q", take("c_q")), ("q_d", take("q_d")), ("k_d", take("k_d")), ("v_d", take("v_d")),
              ("q_i", take("q_i")), ("q_m", take("q_m")), ("c_kv", take("c_kv")),
              ("k_r", jnp.concatenate([take("k_r"), take("k_i")], axis=1)),
              ("w_i", _pad_last(take("w_i"), LANES))]
    col, start = {}, 0
    for name, piece in pieces:
        assert start % piece.shape[1] == 0, name
        col[name] = start
        start += piece.shape[1]
    w_in_r = jnp.concatenate([p for _, p in pieces], axis=1).astype(BF16)

    proj = _matmul(h, w_in_r, F32, bm=1024, bn=768, name="input_projection")
    cos, sin = _rope_tables(positions)

    q_a, k_a, vt_a = _mla_prep(proj, col, g_cq, w_uq, g_ckv, w_ukv, g_q_mla, g_k_mla, cos, sin, batch)
    y_mla = _attention(q_a, k_a, vt_a, None, batch, MLA_HEADS,
                       MLA_QK_PAD, MLA_V, "chunk_causal_attention")

    q_b = _head_norm_rope(proj, col["q_d"] // dsa, g_q_dsa, cos, sin, _query_premultiplier(DSA_HEAD_DIM))
    k_b = _head_norm_rope(proj, col["k_d"] // dsa, g_k_dsa, cos, sin)
    vt_b = _heads_transposed(proj, col["v_d"] // dsa, batch, DSA_HEADS, DSA_HEAD_DIM)
    qi_width = IDX_HEADS * IDX_DIM
    qi = _rope_only(proj, qi_width, col["q_i"] // qi_width, cos, sin)
    ki = _rope_only(proj, LANES, col["k_r"] // LANES, cos, sin)[:, IDX_DIM:]
    kit = ki.reshape(batch, s, IDX_DIM).transpose(0, 2, 1)
    topk = min(DSA_MAX_TOPK, s // 4)
    own_weights = (w_gate, w_o_mla, w_o_dsa, w_o_mem, w_out)
    selected_t, cast = _indexer_select(qi, kit, proj, col["w_i"] // LANES, batch, topk,
                                       own_weights + tuple(later_weights))
    w_gate, w_o_mla, w_o_dsa, w_o_mem, w_out = cast[:len(own_weights)]
    y_dsa = _attention(q_b, k_b, vt_b, selected_t, batch, DSA_HEADS,
                       DSA_HEAD_DIM, DSA_HEAD_DIM, "masked_attention")

    m_len = mem.shape[1]
    mem_n = _rmsnorm(mem.reshape(batch * m_len, d), g_mem)
    kv_mem = _matmul(mem_n, w_mem_kv.astype(BF16), F32, bm=512, bn=512, name="memory_kv")
    y_mem = _memory_attention(proj, col["q_m"] // (MEM_HEADS * MEM_HEAD_DIM), kv_mem, g_q_mem, g_k_mem, batch)

    gated = _gate_combine(h, (y_mla, y_dsa, y_mem), w_gate, (w_o_mla, w_o_dsa, w_o_mem))
    return (_matmul_residual(gated, w_out, x, 1.0, bm=1024, name="mixer_out"), cast[len(own_weights):])


def kernel(x, mem, positions, g_ff1, w_ff1_gate, w_ff1_up, w_ff1_down, g_mix, w_in, g_cq, w_uq, g_ckv, w_ukv, g_q_mla, g_k_mla, g_q_dsa, g_k_dsa, g_mem, w_mem_kv, g_q_mem, g_k_mem, w_o_mla, w_o_dsa, w_o_mem, w_gate, w_out, g_ff2, w_ff2_gate, w_ff2_up, w_ff2_down):
    batch, seq, d = x.shape
    xt = x.reshape(batch * seq, d)
    for l in range(g_ff1.shape[0]):
        xt = _ffn_half_step(xt, g_ff1[l], w_ff1_gate[l], w_ff1_up[l], w_ff1_down[l])
        h = _rmsnorm(xt, g_mix[l])
        xt, ff2 = _hybrid_mixer(xt, h, mem, positions, w_in[l], g_cq[l], w_uq[l], g_ckv[l], w_ukv[l],
                                g_q_mla[l], g_k_mla[l], g_q_dsa[l], g_k_dsa[l], g_mem[l], w_mem_kv[l],
                                g_q_mem[l], g_k_mem[l], w_o_mla[l], w_o_dsa[l], w_o_mem[l], w_gate[l], w_out[l],
                                batch, (w_ff2_gate[l], w_ff2_up[l], w_ff2_down[l]))
        xt = _ffn_half_step(xt, g_ff2[l], *ff2)
    return xt.reshape(batch, seq, d)
```

```python
import functools

import jax
import jax.numpy as jnp
from jax import lax
from jax.experimental import pallas as pl
from jax.experimental.pallas import tpu as pltpu

CHUNK = 64
ROPE_THETA = 500000.0
EPS = 1e-6

MLA_HEADS = 12
MLA_NOPE = 128
MLA_ROPE = 64
MLA_V = 128
MLA_QK = MLA_ROPE + MLA_NOPE
MLA_QK_PAD = 256

DSA_HEADS = 12
DSA_HEAD_DIM = 128
DSA_ROT = DSA_HEAD_DIM // 4
DSA_MAX_TOPK = 256
IDX_HEADS = 32
IDX_DIM = 64
IDX_ROT = IDX_DIM // 4

MEM_HEADS = 4
MEM_HEAD_DIM = 256

LANES = 128
SUBLANES = 8
BF16_SUBLANES = 16
VMEM_LIMIT_BYTES = 56 * 1024 * 1024

F32 = jnp.float32
BF16 = jnp.bfloat16
NEG_INF = float("-inf")
INT_MIN = -2 ** 31
LOG2_E = 1.4426950408889634
NEG_INF_KEY = (0xFF800000 - (1 << 32)) ^ 0x7FFFFFFF


def _params(*semantics):
    return pltpu.CompilerParams(dimension_semantics=semantics, vmem_limit_bytes=VMEM_LIMIT_BYTES)


def _block(dim, preferred):
    if dim <= preferred:
        return dim
    b = preferred - preferred % LANES
    while b >= LANES:
        if dim % b == 0:
            return b
        b -= LANES
    return dim


def _rms_scale(x, width):
    return lax.rsqrt(jnp.sum(x * x, axis=-1, keepdims=True) * (1.0 / width) + EPS)


def _dot(a, b):
    return jnp.dot(a, b, preferred_element_type=F32)


def _dot_nt(a, b):
    return lax.dot_general(a, b, (((1,), (1,)), ((), ())), preferred_element_type=F32)


def _sigmoid(x):
    return 1.0 / (1.0 + jnp.exp(-x))


def _query_premultiplier(head_dim):
    return head_dim ** -0.5 * LOG2_E


def _chunk_of(frame):
    return lax.shift_right_logical(frame, CHUNK.bit_length() - 1)


def _rmsnorm_kernel(x_ref, g_ref, o_ref):
    x = x_ref[...]
    o_ref[...] = (x * _rms_scale(x, x.shape[-1]) * g_ref[...]).astype(o_ref.dtype)


def _rmsnorm(x, g, out_dtype=BF16):
    m, d = x.shape
    bm = _block(m, 512)
    return pl.pallas_call(
        _rmsnorm_kernel,
        grid=(m // bm,),
        in_specs=[pl.BlockSpec((bm, d), lambda i: (i, 0)), pl.BlockSpec((1, d), lambda i: (0, 0))],
        out_specs=pl.BlockSpec((bm, d), lambda i: (i, 0)),
        out_shape=jax.ShapeDtypeStruct((m, d), out_dtype),
        compiler_params=_params("parallel"),
        name="rmsnorm",
    )(x, g.reshape(1, d))


def _matmul_kernel(a_ref, b_ref, o_ref):
    o_ref[...] = _dot(a_ref[...], b_ref[...]).astype(o_ref.dtype)


def _matmul(a, b, out_dtype, bm=1024, bn=512, name="matmul"):
    m, k = a.shape
    n = b.shape[1]
    bm, bn = _block(m, bm), _block(n, bn)
    return pl.pallas_call(
        _matmul_kernel,
        grid=(m // bm, n // bn),
        in_specs=[pl.BlockSpec((bm, k), lambda i, j: (i, 0)), pl.BlockSpec((k, bn), lambda i, j: (0, j))],
        out_specs=pl.BlockSpec((bm, bn), lambda i, j: (i, j)),
        out_shape=jax.ShapeDtypeStruct((m, n), out_dtype),
        compiler_params=_params("parallel", "parallel"),
        name=name,
    )(a, b)


def _swiglu_up_kernel(a_ref, wg_ref, wu_ref, o_ref):
    a = a_ref[...]
    g = _dot(a, wg_ref[...])
    u = _dot(a, wu_ref[...])
    o_ref[...] = (g * _sigmoid(g) * u).astype(o_ref.dtype)


def _swiglu_up(a, wg, wu, bm=1024, bn=512):
    m, k = a.shape
    n = wg.shape[1]
    bm, bn = _block(m, bm), _block(n, bn)
    return pl.pallas_call(
        _swiglu_up_kernel,
        grid=(m // bm, n // bn),
        in_specs=[pl.BlockSpec((bm, k), lambda i, j: (i, 0)),
                  pl.BlockSpec((k, bn), lambda i, j: (0, j)),
                  pl.BlockSpec((k, bn), lambda i, j: (0, j))],
        out_specs=pl.BlockSpec((bm, bn), lambda i, j: (i, j)),
        out_shape=jax.ShapeDtypeStruct((m, n), BF16),
        compiler_params=_params("parallel", "parallel"),
        name="swiglu_up",
    )(a, wg, wu)


def _matmul_residual_kernel(a_ref, b_ref, x_ref, o_ref, *, scale):
    o_ref[...] = x_ref[...] + scale * _dot(a_ref[...], b_ref[...])


def _matmul_residual(a, b, x, scale, bm=512, bn=512, name="matmul_residual"):
    m, k = a.shape
    n = b.shape[1]
    bm, bn = _block(m, bm), _block(n, bn)
    return pl.pallas_call(
        functools.partial(_matmul_residual_kernel, scale=scale),
        grid=(m // bm, n // bn),
        in_specs=[pl.BlockSpec((bm, k), lambda i, j: (i, 0)),
                  pl.BlockSpec((k, bn), lambda i, j: (0, j)),
                  pl.BlockSpec((bm, bn), lambda i, j: (i, j))],
        out_specs=pl.BlockSpec((bm, bn), lambda i, j: (i, j)),
        out_shape=jax.ShapeDtypeStruct((m, n), F32),
        compiler_params=_params("parallel", "parallel"),
        name=name,
    )(a, b, x)


def _ffn_half_step(x, g, w_gate, w_up, w_down):
    h = _rmsnorm(x, g)
    a = _swiglu_up(h, w_gate.astype(BF16), w_up.astype(BF16))
    return _matmul_residual(a, w_down.astype(BF16), x, 0.5, name="ffn_down")


def _gate_combine_kernel(h_ref, ya_ref, yb_ref, yc_ref, wg_ref, wa_ref, wb_ref, wc_ref, o_ref):
    h = h_ref[...]
    out = _sigmoid(_dot(h, wg_ref[0])) * _dot(ya_ref[...], wa_ref[...])
    out = out + _sigmoid(_dot(h, wg_ref[1])) * _dot(yb_ref[...], wb_ref[...])
    out = out + _sigmoid(_dot(h, wg_ref[2])) * _dot(yc_ref[...], wc_ref[...])
    o_ref[...] = out.astype(o_ref.dtype)


def _gate_combine(h, ys, w_gate, w_os, bm=512, bn=512):
    m, d = h.shape
    n = w_gate.shape[-1]
    bm, bn = _block(m, bm), _block(n, bn)
    row = lambda w: pl.BlockSpec((bm, w), lambda i, j: (i, 0))
    col = lambda k: pl.BlockSpec((k, bn), lambda i, j: (0, j))
    return pl.pallas_call(
        _gate_combine_kernel,
        grid=(m // bm, n // bn),
        in_specs=[row(d)] + [row(y.shape[1]) for y in ys]
                 + [pl.BlockSpec((3, d, bn), lambda i, j: (0, 0, j))] + [col(w.shape[0]) for w in w_os],
        out_specs=pl.BlockSpec((bm, bn), lambda i, j: (i, j)),
        out_shape=jax.ShapeDtypeStruct((m, n), BF16),
        compiler_params=_params("parallel", "parallel"),
        name="gate_combine",
    )(h, *ys, w_gate, *w_os)


def _rope_tables_kernel(pos_ref, invf_ref, sgn_ref, cos_ref, sin_ref):
    pos = pos_ref[...].astype(F32)
    for t in range(invf_ref.shape[0]):
        ang = pos * invf_ref[t:t + 1, :]
        cos_ref[t] = jnp.cos(ang)
        sin_ref[t] = jnp.sin(ang) * sgn_ref[t:t + 1, :]


def _rope_lane_tables(rot, period):
    half = rot // 2
    inv_freq = ROPE_THETA ** (-jnp.arange(half, dtype=F32) / half)
    lane = jnp.arange(LANES) % period
    invf = jnp.where(lane < rot, inv_freq[lane % half], 0.0)
    sgn = jnp.where(lane < half, -1.0, jnp.where(lane < rot, 1.0, 0.0))
    return invf.astype(F32), sgn.astype(F32)


def _rope_tables(positions):
    t = positions.size
    specs = [_rope_lane_tables(MLA_ROPE, LANES), _rope_lane_tables(DSA_ROT, LANES),
             _rope_lane_tables(IDX_ROT, IDX_DIM)]
    invf = jnp.stack([s[0] for s in specs])
    sgn = jnp.stack([s[1] for s in specs])
    bm = _block(t, 1024)
    n = invf.shape[0]
    const = pl.BlockSpec((n, LANES), lambda i: (0, 0))
    out = pl.BlockSpec((n, bm, LANES), lambda i: (0, i, 0))
    return pl.pallas_call(
        _rope_tables_kernel,
        grid=(t // bm,),
        in_specs=[pl.BlockSpec((bm, 1), lambda i: (i, 0)), const, const],
        out_specs=[out, out],
        out_shape=[jax.ShapeDtypeStruct((n, t, LANES), F32)] * 2,
        compiler_params=_params("parallel"),
        name="rope_tables",
    )(positions.reshape(t, 1), invf, sgn)


def _rope_tile(x, cos, sin, half, period):
    lane = lax.broadcasted_iota(jnp.int32, x.shape, 1) & (period - 1)
    partner = jnp.where(lane < half, pltpu.roll(x, LANES - half, 1), pltpu.roll(x, half, 1))
    return x * cos + partner * sin


def _mla_q_kernel(cq_ref, gcq_ref, w_ref, gq_ref, cos_ref, sin_ref, o_ref):
    x = cq_ref[...]
    xn = (x * _rms_scale(x, x.shape[-1]) * gcq_ref[...]).astype(BF16)
    q = _dot(xn, w_ref[...])
    cos, sin = cos_ref[0], sin_ref[0]
    g = gq_ref[...] * _query_premultiplier(MLA_QK)
    for h in range(MLA_HEADS):
        lo = h * MLA_QK_PAD
        qh = q[:, lo:lo + MLA_QK_PAD]
        y = qh * _rms_scale(qh, MLA_QK) * g
        o_ref[:, lo:lo + LANES] = _rope_tile(y[:, :LANES], cos, sin, MLA_ROPE // 2, LANES).astype(o_ref.dtype)
        o_ref[:, lo + LANES:lo + MLA_QK_PAD] = y[:, LANES:].astype(o_ref.dtype)


def _store_heads_transposed(x, vt_ref):
    heads, hd, _ = vt_ref.shape
    for h in range(heads):
        vt_ref[h] = x[:, h * hd:(h + 1) * hd].T.astype(vt_ref.dtype)


def _mla_kv_kernel(ckv_ref, kr_ref, gckv_ref, wk_ref, wv_ref, gk_ref, cos_ref, sin_ref, k_ref, vt_ref):
    x = ckv_ref[...]
    xn = (x * _rms_scale(x, x.shape[-1]) * gckv_ref[...]).astype(BF16)
    k_nope = _dot(xn, wk_ref[...])
    _store_heads_transposed(_dot(xn, wv_ref[...]), vt_ref)
    kr = kr_ref[...]
    lane = lax.broadcasted_iota(jnp.int32, kr.shape, 1)
    kr = jnp.where(lane < MLA_ROPE, kr, 0.0)
    cos, sin = cos_ref[0], sin_ref[0]
    g = gk_ref[...]
    for h in range(MLA_HEADS):
        lo = h * MLA_QK_PAD
        ka = k_nope[:, lo:lo + LANES] + kr
        kb = k_nope[:, lo + LANES:lo + MLA_QK_PAD]
        ms = (jnp.sum(ka * ka, axis=-1, keepdims=True) + jnp.sum(kb * kb, axis=-1, keepdims=True)) * (1.0 / MLA_QK)
        r = lax.rsqrt(ms + EPS)
        ya = ka * r * g[:, :LANES]
        yb = kb * r * g[:, LANES:]
        k_ref[:, lo:lo + LANES] = _rope_tile(ya, cos, sin, MLA_ROPE // 2, LANES).astype(k_ref.dtype)
        k_ref[:, lo + LANES:lo + MLA_QK_PAD] = yb.astype(k_ref.dtype)


def _pad_last(a, width):
    return jnp.pad(a, [(0, 0)] * (a.ndim - 1) + [(0, width - a.shape[-1])])


def _mla_prep(proj, col, g_cq, w_uq, g_ckv, w_ukv, g_q, g_k, cos, sin, batch):
    t = proj.shape[0]
    q_rank, kv_rank = w_uq.shape[0], w_ukv.shape[0]
    hp = MLA_HEADS * MLA_QK_PAD
    w_q = _pad_last(w_uq.reshape(q_rank, MLA_HEADS, MLA_QK), MLA_QK_PAD).reshape(q_rank, hp).astype(BF16)
    w_kv = w_ukv.reshape(kv_rank, MLA_HEADS, MLA_NOPE + MLA_V)
    w_k = jnp.pad(w_kv[..., :MLA_NOPE], ((0, 0), (0, 0), (MLA_ROPE, MLA_QK_PAD - MLA_QK)))
    w_k = w_k.reshape(kv_rank, hp).astype(BF16)
    w_v = w_kv[..., MLA_NOPE:].reshape(kv_rank, MLA_HEADS * MLA_V).astype(BF16)
    g_qp = _pad_last(g_q, MLA_QK_PAD).reshape(1, MLA_QK_PAD)
    g_kp = _pad_last(g_k, MLA_QK_PAD).reshape(1, MLA_QK_PAD)

    bm = _block(t, 512)
    rows = lambda w, c: pl.BlockSpec((bm, w), lambda i, c=c: (i, c))
    full = lambda a: pl.BlockSpec(a.shape, lambda i: (0,) * a.ndim)
    table = pl.BlockSpec((1, bm, LANES), lambda i: (0, i, 0))
    g_cq2, g_ckv2 = g_cq.reshape(1, -1), g_ckv.reshape(1, -1)

    q = pl.pallas_call(
        _mla_q_kernel,
        grid=(t // bm,),
        in_specs=[rows(q_rank, col["c_q"] // q_rank), full(g_cq2), full(w_q), full(g_qp), table, table],
        out_specs=pl.BlockSpec((bm, hp), lambda i: (i, 0)),
        out_shape=jax.ShapeDtypeStruct((t, hp), BF16),
        compiler_params=_params("parallel"),
        name="mla_q_prep",
    )(proj, g_cq2, w_q, g_qp, cos, sin)

    k, v = pl.pallas_call(
        _mla_kv_kernel,
        grid=(t // bm,),
        in_specs=[rows(kv_rank, col["c_kv"] // kv_rank), rows(LANES, col["k_r"] // LANES), full(g_ckv2),
                  full(w_k), full(w_v), full(g_kp), table, table],
        out_specs=[pl.BlockSpec((bm, hp), lambda i: (i, 0)),
                   _heads_transposed_spec(MLA_HEADS, MLA_V, bm, t // batch)],
        out_shape=[jax.ShapeDtypeStruct((t, hp), BF16),
                   jax.ShapeDtypeStruct((batch, MLA_HEADS, MLA_V, t // batch), BF16)],
        compiler_params=_params("parallel"),
        name="mla_kv_prep",
    )(proj, proj, g_ckv2, w_k, w_v, g_kp, cos, sin)
    return q, k, v


ATTN_HEADS_PER_STEP = 3


def _attn_scratch(tq, dk, dv):
    per_head = [pltpu.VMEM((1, tq), F32), pltpu.VMEM((1, tq), F32), pltpu.VMEM((dv, tq), F32)]
    per_parity = [pltpu.VMEM((tq, tq), F32), pltpu.VMEM((1, tq), F32),
                  pltpu.VMEM((tq, tq), BF16), pltpu.VMEM((1, tq), F32)]
    return (per_head + 2 * per_parity) * ATTN_HEADS_PER_STEP + [pltpu.VMEM((ATTN_HEADS_PER_STEP, dk, tq), BF16)]


def _attn_kernel(q_ref, k_ref, vt_ref, *rest, tq, dk, dv, selected_keys):
    n_scratch = 11 * ATTN_HEADS_PER_STEP + 1
    scratch = rest[-n_scratch:]
    qt_ref = scratch[-1]
    sel_ref = rest[0] if selected_keys else None
    o_ref = rest[-n_scratch - 1]
    i = pl.program_id(2)
    heads = range(ATTN_HEADS_PER_STEP)
    m_of, l_of, acc_of, s_of, mx_of, p_of, al_of = [], [], [], [], [], [], []
    for h in heads:
        refs = scratch[11 * h:11 * (h + 1)]
        m_of.append(refs[0]); l_of.append(refs[1]); acc_of.append(refs[2])
        s_of.append((refs[3], refs[7])); mx_of.append((refs[4], refs[8]))
        p_of.append((refs[5], refs[9])); al_of.append((refs[6], refs[10]))
    for h in heads:
        qt_ref[h] = q_ref[:, h * dk:(h + 1) * dk].T

    for h in heads:
        m_of[h][...] = jnp.full((1, tq), NEG_INF, F32)
        l_of[h][...] = jnp.zeros((1, tq), F32)
        acc_of[h][...] = jnp.zeros((dv, tq), F32)
        s_of[h][1][...] = jnp.full((tq, tq), NEG_INF, F32)
        mx_of[h][1][...] = jnp.full((1, tq), NEG_INF, F32)
        p_of[h][0][...] = jnp.zeros((tq, tq), BF16)
        al_of[h][0][...] = jnp.zeros((1, tq), F32)

    def scores(block, par, diagonal):
        off = pl.multiple_of(block * tq, tq)
        if selected_keys:
            visible = sel_ref[pl.ds(off, tq), :].astype(F32) > 0.0
        elif diagonal:
            key_chunk = _chunk_of(lax.broadcasted_iota(jnp.int32, (tq, tq), 0))
            query_chunk = _chunk_of(lax.broadcasted_iota(jnp.int32, (tq, tq), 1))
            visible = key_chunk <= query_chunk
        else:
            visible = None
        for h in heads:
            s = _dot(k_ref[pl.ds(off, tq), h * dk:(h + 1) * dk], qt_ref[h])
            if visible is not None:
                s = jnp.where(visible, s, NEG_INF)
            s_of[h][par][...] = s
            mx_of[h][par][...] = jnp.max(s, axis=0, keepdims=True)

    def softmax(par):
        for h in heads:
            m_prev = m_of[h][...]
            m_new = jnp.maximum(m_prev, mx_of[h][par][...])
            m_safe = jnp.where(m_new == NEG_INF, 0.0, m_new)
            alpha = jnp.exp2(m_prev - m_safe)
            p = jnp.exp2(s_of[h][par][...] - m_safe)
            l_of[h][...] = alpha * l_of[h][...] + jnp.sum(p, axis=0, keepdims=True)
            p_of[h][par][...] = p.astype(BF16)
            al_of[h][par][...] = alpha
            m_of[h][...] = m_new

    def values(block, par):
        off = pl.multiple_of(jnp.maximum(block, 0) * tq, tq)
        for h in heads:
            acc_of[h][...] = al_of[h][par][...] * acc_of[h][...] + _dot(vt_ref[h, :, pl.ds(off, tq)],
                                                                         p_of[h][par][...])

    def step(t, par, diagonal=False):
        values(t - 2, par)
        softmax(1 - par)
        scores(t, par, diagonal)

    def pair(u, carry):
        step(2 * u, 0)
        step(2 * u + 1, 1)
        return carry

    lax.fori_loop(0, lax.shift_right_logical(i, 1), pair, 0)

    def drain(par):
        step(i, par, diagonal=True)
        values(i - 1, 1 - par)
        softmax(par)
        values(i, par)

    odd = (i & 1) == 1

    @pl.when(odd)
    def _():
        step(i - 1, 0)
        drain(1)

    @pl.when(jnp.logical_not(odd))
    def _():
        drain(0)

    for h in heads:
        o = acc_of[h][...] / l_of[h][...]
        o_ref[:, h * dv:(h + 1) * dv] = o.T.astype(o_ref.dtype)


def _attention(q, k, vt, selected_t, batch, heads, dk, dv, name, tq=256):
    t = q.shape[0]
    s = t // batch
    tq = _block(s, tq)
    nq = s // tq
    hp = ATTN_HEADS_PER_STEP
    in_specs = [pl.BlockSpec((tq, hp * dk), lambda b, h, i: (b * nq + i, h)),
                pl.BlockSpec((s, hp * dk), lambda b, h, i: (b, h)),
                pl.BlockSpec((None, hp, dv, s), lambda b, h, i: (b, h, 0, 0))]
    operands = [q, k, vt]
    if selected_t is not None:
        in_specs.append(pl.BlockSpec((None, s, tq), lambda b, h, i: (b, 0, i)))
        operands.append(selected_t)
    return pl.pallas_call(
        functools.partial(_attn_kernel, tq=tq, dk=dk, dv=dv, selected_keys=selected_t is not None),
        grid=(batch, heads // hp, nq),
        in_specs=in_specs,
        out_specs=pl.BlockSpec((tq, hp * dv), lambda b, h, i: (b * nq + i, h)),
        out_shape=jax.ShapeDtypeStruct((t, heads * dv), BF16),
        scratch_shapes=_attn_scratch(tq, dk, dv),
        compiler_params=_params("parallel", "parallel", "parallel"),
        name=name,
    )(*operands)


def _heads_transposed_spec(heads, hd, bm, seq):
    blocks_per_batch = seq // bm
    return pl.BlockSpec((None, heads, hd, bm), lambda i: (i // blocks_per_batch, 0, 0, i % blocks_per_batch))


def _heads_transposed_kernel(x_ref, vt_ref):
    _store_heads_transposed(x_ref[...], vt_ref)


def _heads_transposed(proj, col_block, batch, heads, hd):
    t = proj.shape[0]
    bm = _block(t // batch, 512)
    return pl.pallas_call(
        _heads_transposed_kernel,
        grid=(t // bm,),
        in_specs=[pl.BlockSpec((bm, heads * hd), lambda i: (i, col_block))],
        out_specs=_heads_transposed_spec(heads, hd, bm, t // batch),
        out_shape=jax.ShapeDtypeStruct((batch, heads, hd, t // batch), BF16),
        compiler_params=_params("parallel"),
        name="heads_transposed",
    )(proj)


def _head_norm_rope_kernel(x_ref, g_ref, cos_ref, sin_ref, o_ref, *, heads, premultiplier):
    cos, sin = cos_ref[0], sin_ref[0]
    g = g_ref[...] * premultiplier
    for h in range(heads):
        x = x_ref[:, h * LANES:(h + 1) * LANES]
        y = x * _rms_scale(x, LANES) * g
        o_ref[:, h * LANES:(h + 1) * LANES] = _rope_tile(y, cos, sin, DSA_ROT // 2, LANES).astype(o_ref.dtype)


def _head_norm_rope(proj, col_block, g, cos, sin, premultiplier=1.0):
    t = proj.shape[0]
    width = DSA_HEADS * DSA_HEAD_DIM
    bm = _block(t, 512)
    table = pl.BlockSpec((1, bm, LANES), lambda i: (1, i, 0))
    return pl.pallas_call(
        functools.partial(_head_norm_rope_kernel, heads=DSA_HEADS, premultiplier=premultiplier),
        grid=(t // bm,),
        in_specs=[pl.BlockSpec((bm, width), lambda i: (i, col_block)),
                  pl.BlockSpec((1, LANES), lambda i: (0, 0)), table, table],
        out_specs=pl.BlockSpec((bm, width), lambda i: (i, 0)),
        out_shape=jax.ShapeDtypeStruct((t, width), BF16),
        compiler_params=_params("parallel"),
        name="head_norm_rope",
    )(proj, g.reshape(1, LANES), cos, sin)


def _rope_only_kernel(x_ref, cos_ref, sin_ref, o_ref):
    cos, sin = cos_ref[0], sin_ref[0]
    for c in range(x_ref.shape[1] // LANES):
        x = x_ref[:, c * LANES:(c + 1) * LANES]
        o_ref[:, c * LANES:(c + 1) * LANES] = _rope_tile(x, cos, sin, IDX_ROT // 2, IDX_DIM).astype(o_ref.dtype)


def _rope_only(proj, width, col_block, cos, sin):
    t = proj.shape[0]
    bm = _block(t, 512)
    table = pl.BlockSpec((1, bm, LANES), lambda i: (2, i, 0))
    return pl.pallas_call(
        _rope_only_kernel,
        grid=(t // bm,),
        in_specs=[pl.BlockSpec((bm, width), lambda i: (i, col_block)), table, table],
        out_specs=pl.BlockSpec((bm, width), lambda i: (i, 0)),
        out_shape=jax.ShapeDtypeStruct((t, width), BF16),
        compiler_params=_params("parallel"),
        name="indexer_rope",
    )(proj, cos, sin)


SCORE_COLS = 256
COUNT_ACC_ROWS = 64


def _sortable_key(x):
    bits = lax.bitcast_convert_type(x, jnp.int32)
    return bits ^ ((bits >> 31) & 0x7FFFFFFF)


def _indexer_kernel(qi_ref, kit_ref, w_ref, *rest, tq, tk, topk, seq, w_scale, n_casts):
    cast_src, sel_ref, cast_dst = rest[:n_casts], rest[n_casts], rest[n_casts + 1:2 * n_casts + 1]
    key_ref, wb_ref, qh_ref = rest[2 * n_casts + 1:]
    for src, dst in zip(cast_src, cast_dst):
        dst[...] = src[...].astype(dst.dtype)

    i = pl.program_id(1)
    n_tiles = pl.cdiv((i + 1) * tq, tk)
    row_chunk = _chunk_of(i * tq + lax.broadcasted_iota(jnp.int32, (tq, 1), 0))

    w = w_ref[...] * w_scale
    for h in range(IDX_HEADS):
        wb_ref[h] = jnp.broadcast_to(w[:, h:h + 1], (tq, LANES))
        qh_ref[h] = qi_ref[:, h * IDX_DIM:(h + 1) * IDX_DIM]

    def score_tile(j, carry):
        for part in range(tk // SCORE_COLS):
            off = pl.multiple_of(j * tk + part * SCORE_COLS, SCORE_COLS)
            kt = kit_ref[:, pl.ds(off, SCORE_COLS)]
            acc = [jnp.zeros((tq, LANES), F32) for _ in range(SCORE_COLS // LANES)]
            for h in range(IDX_HEADS):
                logits = _dot(qh_ref[h], kt)
                wh = wb_ref[h]
                for c in range(SCORE_COLS // LANES):
                    acc[c] = acc[c] + jnp.maximum(logits[:, c * LANES:(c + 1) * LANES], 0.0) * wh
            for c in range(SCORE_COLS // LANES):
                col = off + c * LANES + lax.broadcasted_iota(jnp.int32, (1, LANES), 1)
                score = jnp.where(_chunk_of(col) <= row_chunk, acc[c], NEG_INF)
                key_ref[pl.ds(off + c * LANES, LANES), :] = _sortable_key(score.T)
        return carry

    lax.fori_loop(0, n_tiles, score_tile, 0)

    def count(pred):
        rows = min(tk, COUNT_ACC_ROWS)

        def tile(j, acc):
            off = pl.multiple_of(j * tk, tk)
            hit = jnp.where(pred(key_ref[pl.ds(off, tk), :], off), 1.0, 0.0)
            return acc + jnp.sum(hit.reshape(tk // rows, rows, tq), axis=0)
        acc = lax.fori_loop(0, n_tiles, tile, jnp.zeros((rows, tq), F32))
        return jnp.sum(acc, axis=0, keepdims=True)

    def value_bit(b, lo):
        cand = lo + lax.shift_left(jnp.int32(1), 31 - b)
        n_ge = count(lambda kt, _: kt >= cand)
        return jnp.where(n_ge >= topk, cand, lo)

    thr = lax.fori_loop(0, 32, value_bit, jnp.full((1, tq), INT_MIN, jnp.int32))
    n_ge = count(lambda kt, _: kt >= thr)
    surplus = jnp.logical_and(n_ge > topk, thr > NEG_INF_KEY)

    def write_tile(j, carry):
        off = pl.multiple_of(j * tk, tk)
        kt = key_ref[pl.ds(off, tk), :]
        chosen = jnp.logical_and(kt >= thr, kt > NEG_INF_KEY)
        sel_ref[pl.ds(off, tk), :] = jnp.where(chosen, 1.0, 0.0).astype(sel_ref.dtype)
        return carry

    lax.fori_loop(0, n_tiles, write_tile, 0)

    def clear_tile(j, carry):
        off = pl.multiple_of(j * tk, tk)
        sel_ref[pl.ds(off, tk), :] = jnp.zeros((tk, tq), sel_ref.dtype)
        return carry

    lax.fori_loop(n_tiles, seq // tk, clear_tile, 0)

    @pl.when(jnp.max(jnp.where(surplus, 1.0, 0.0)) > 0.0)
    def _():
        n_gt = count(lambda kt, _: kt > thr)
        need = topk - n_gt

        def position(first_key):
            return first_key + lax.broadcasted_iota(jnp.int32, (tk, tq), 0)

        def position_bit(b, base):
            cand = base + lax.shift_left(jnp.int32(1), (seq - 1).bit_length() - 1 - b)
            n_before = count(lambda kt, k0: jnp.logical_and(kt == thr, position(k0) < cand))
            return jnp.where(n_before < need, cand, base)

        last = lax.fori_loop(0, (seq - 1).bit_length(), position_bit, jnp.zeros((1, tq), jnp.int32))

        def rewrite_tile(j, carry):
            off = pl.multiple_of(j * tk, tk)
            kt = key_ref[pl.ds(off, tk), :]
            chosen = jnp.logical_or(kt > thr, jnp.logical_and(kt == thr, position(off) <= last))
            chosen = jnp.logical_and(chosen, kt > NEG_INF_KEY)
            sel_ref[pl.ds(off, tk), :] = jnp.where(chosen, 1.0, 0.0).astype(sel_ref.dtype)
            return carry

        lax.fori_loop(0, n_tiles, rewrite_tile, 0)


def _indexer_select(qi, kit, w, w_col_block, batch, topk, weights_to_cast=(), tq=128, tk=512):
    s = kit.shape[2]
    tq = _block(s, tq)
    tk = _block(s, tk)
    assert tk % SCORE_COLS == 0
    assert tq == LANES, "score tiles are transposed one (LANES, LANES) tile at a time"
    nq = s // tq
    steps = batch * nq
    w_scale = (IDX_DIM ** -0.5) * (IDX_HEADS ** -0.5)

    slabs = [a.reshape(-1, a.shape[-1]) for a in weights_to_cast]

    def slab_spec(a):
        rows = a.shape[0]
        r = next(r for r in range(BF16_SUBLANES, rows + 1, BF16_SUBLANES)
                 if rows % r == 0 and rows // r <= steps)
        last = rows // r - 1
        return pl.BlockSpec((r, a.shape[1]), lambda b, i: (jnp.minimum(b * nq + i, last), 0))

    outs = pl.pallas_call(
        functools.partial(_indexer_kernel, tq=tq, tk=tk, topk=topk, seq=s, w_scale=w_scale, n_casts=len(slabs)),
        grid=(batch, nq),
        in_specs=[pl.BlockSpec((tq, IDX_HEADS * IDX_DIM), lambda b, i: (b * nq + i, 0)),
                  pl.BlockSpec((None, IDX_DIM, s), lambda b, i: (b, 0, 0)),
                  pl.BlockSpec((tq, LANES), lambda b, i: (b * nq + i, w_col_block))]
                 + [slab_spec(a) for a in slabs],
        out_specs=[pl.BlockSpec((None, s, tq), lambda b, i: (b, 0, i))] + [slab_spec(a) for a in slabs],
        out_shape=[jax.ShapeDtypeStruct((batch, s, s), BF16)]
                  + [jax.ShapeDtypeStruct(a.shape, BF16) for a in slabs],
        scratch_shapes=[pltpu.VMEM((s, tq), jnp.int32), pltpu.VMEM((IDX_HEADS, tq, LANES), F32),
                        pltpu.VMEM((IDX_HEADS, tq, IDX_DIM), BF16)],
        compiler_params=_params("arbitrary", "arbitrary"),
        name="indexer_select",
    )(qi, kit, w, *slabs)
    return outs[0], tuple(o.reshape(a.shape) for o, a in zip(outs[1:], weights_to_cast))


def _memory_attn_kernel(q_ref, kv_ref, gq_ref, gk_ref, o_ref, *, scale):
    hd = MEM_HEAD_DIM
    gq, gk = gq_ref[...], gk_ref[...]
    for h in range(MEM_HEADS):
        q = q_ref[:, h * hd:(h + 1) * hd]
        k = kv_ref[:, h * hd:(h + 1) * hd]
        v = kv_ref[:, (MEM_HEADS + h) * hd:(MEM_HEADS + h + 1) * hd].astype(BF16)
        qn = (q * _rms_scale(q, hd) * gq).astype(BF16)
        kn = (k * _rms_scale(k, hd) * gk).astype(BF16)
        s = _dot_nt(qn, kn) * scale
        e = jnp.exp(s - jnp.max(s, axis=-1, keepdims=True))
        p = e / jnp.sum(e, axis=-1, keepdims=True)
        o_ref[:, h * hd:(h + 1) * hd] = _dot(p.astype(BF16), v).astype(o_ref.dtype)


def _memory_attention(proj, col_block, kv, g_q, g_k, batch, tq=512):
    t = proj.shape[0]
    s = t // batch
    m = kv.shape[0] // batch
    width = MEM_HEADS * MEM_HEAD_DIM
    tq = _block(s, tq)
    nq = s // tq
    g_spec = pl.BlockSpec((1, MEM_HEAD_DIM), lambda b, i: (0, 0))
    return pl.pallas_call(
        functools.partial(_memory_attn_kernel, scale=MEM_HEAD_DIM ** -0.5),
        grid=(batch, nq),
        in_specs=[pl.BlockSpec((tq, width), lambda b, i: (b * nq + i, col_block)),
                  pl.BlockSpec((m, 2 * width), lambda b, i: (b, 0)), g_spec, g_spec],
        out_specs=pl.BlockSpec((tq, width), lambda b, i: (b * nq + i, 0)),
        out_shape=jax.ShapeDtypeStruct((t, width), BF16),
        compiler_params=_params("parallel", "parallel"),
        name="memory_attention",
    )(proj, kv, g_q.reshape(1, -1), g_k.reshape(1, -1))


def _projection_layout(q_rank, kv_rank):
    dsa = DSA_HEADS * DSA_HEAD_DIM
    sizes = [("c_q", q_rank), ("c_kv", kv_rank), ("k_r", MLA_ROPE), ("q_d", dsa), ("k_d", dsa), ("v_d", dsa),
             ("q_i", IDX_HEADS * IDX_DIM), ("k_i", IDX_DIM), ("w_i", IDX_HEADS), ("q_m", MEM_HEADS * MEM_HEAD_DIM)]
    src, start = {}, 0
    for name, n in sizes:
        src[name] = (start, n)
        start += n
    return src, start


def _hybrid_mixer(x, h, mem, positions, w_in, g_cq, w_uq, g_ckv, w_ukv, g_q_mla, g_k_mla, g_q_dsa, g_k_dsa,
                  g_mem, w_mem_kv, g_q_mem, g_k_mem, w_o_mla, w_o_dsa, w_o_mem, w_gate, w_out, batch, later_weights):
    t, d = h.shape
    s = t // batch
    q_rank, kv_rank = w_uq.shape[0], w_ukv.shape[0]
    src, d_in = _projection_layout(q_rank, kv_rank)
    assert d_in == w_in.shape[1]
    dsa = DSA_HEADS * DSA_HEAD_DIM
    assert q_rank == dsa, "the reordered projection layout assumes equal widths for these groups"

    take = lambda name: w_in[:, src[name][0]:src[name][0] + src[name][1]]
    pieces = [("c_q", take("c_q")), ("q_d", take("q_d")), ("k_d", take("k_d")), ("v_d", take("v_d")),
              ("q_i", take("q_i")), ("q_m", take("q_m")), ("c_kv", take("c_kv")),
              ("k_r", jnp.concatenate([take("k_r"), take("k_i")], axis=1)),
              ("w_i", _pad_last(take("w_i"), LANES))]
    col, start = {}, 0
    for name, piece in pieces:
        assert start % piece.shape[1] == 0, name
        col[name] = start
        start += piece.shape[1]
    w_in_r = jnp.concatenate([p for _, p in pieces], axis=1).astype(BF16)

    proj = _matmul(h, w_in_r, F32, bm=1024, bn=768, name="input_projection")
    cos, sin = _rope_tables(positions)

    q_a, k_a, vt_a = _mla_prep(proj, col, g_cq, w_uq, g_ckv, w_ukv, g_q_mla, g_k_mla, cos, sin, batch)
    y_mla = _attention(q_a, k_a, vt_a, None, batch, MLA_HEADS,
                       MLA_QK_PAD, MLA_V, "chunk_causal_attention")

    q_b = _head_norm_rope(proj, col["q_d"] // dsa, g_q_dsa, cos, sin, _query_premultiplier(DSA_HEAD_DIM))
    k_b = _head_norm_rope(proj, col["k_d"] // dsa, g_k_dsa, cos, sin)
    vt_b = _heads_transposed(proj, col["v_d"] // dsa, batch, DSA_HEADS, DSA_HEAD_DIM)
    qi_width = IDX_HEADS * IDX_DIM
    qi = _rope_only(proj, qi_width, col["q_i"] // qi_width, cos, sin)
    ki = _rope_only(proj, LANES, col["k_r"] // LANES, cos, sin)[:, IDX_DIM:]
    kit = ki.reshape(batch, s, IDX_DIM).transpose(0, 2, 1)
    topk = min(DSA_MAX_TOPK, s // 4)
    own_weights = (w_gate, w_o_mla, w_o_dsa, w_o_mem, w_out)
    selected_t, cast = _indexer_select(qi, kit, proj, col["w_i"] // LANES, batch, topk,
                                       own_weights + tuple(later_weights))
    w_gate, w_o_mla, w_o_dsa, w_o_mem, w_out = cast[:len(own_weights)]
    y_dsa = _attention(q_b, k_b, vt_b, selected_t, batch, DSA_HEADS,
                       DSA_HEAD_DIM, DSA_HEAD_DIM, "masked_attention")

    m_len = mem.shape[1]
    mem_n = _rmsnorm(mem.reshape(batch * m_len, d), g_mem)
    kv_mem = _matmul(mem_n, w_mem_kv.astype(BF16), F32, bm=512, bn=512, name="memory_kv")
    y_mem = _memory_attention(proj, col["q_m"] // (MEM_HEADS * MEM_HEAD_DIM), kv_mem, g_q_mem, g_k_mem, batch)

    gated = _gate_combine(h, (y_mla, y_dsa, y_mem), w_gate, (w_o_mla, w_o_dsa, w_o_mem))
    return (_matmul_residual(gated, w_out, x, 1.0, bm=1024, name="mixer_out"), cast[len(own_weights):])


def kernel(x, mem, positions, g_ff1, w_ff1_gate, w_ff1_up, w_ff1_down, g_mix, w_in, g_cq, w_uq, g_ckv, w_ukv, g_q_mla, g_k_mla, g_q_dsa, g_k_dsa, g_mem, w_mem_kv, g_q_mem, g_k_mem, w_o_mla, w_o_dsa, w_o_mem, w_gate, w_out, g_ff2, w_ff2_gate, w_ff2_up, w_ff2_down):
    batch, seq, d = x.shape
    xt = x.reshape(batch * seq, d)
    for l in range(g_ff1.shape[0]):
        xt = _ffn_half_step(xt, g_ff1[l], w_ff1_gate[l], w_ff1_up[l], w_ff1_down[l])
        h = _rmsnorm(xt, g_mix[l])
        xt, ff2 = _hybrid_mixer(xt, h, mem, positions, w_in[l], g_cq[l], w_uq[l], g_ckv[l], w_ukv[l],
                                g_q_mla[l], g_k_mla[l], g_q_dsa[l], g_k_dsa[l], g_mem[l], w_mem_kv[l],
                                g_q_mem[l], g_k_mem[l], w_o_mla[l], w_o_dsa[l], w_o_mem[l], w_gate[l], w_out[l],
                                batch, (w_ff2_gate[l], w_ff2_up[l], w_ff2_down[l]))
        xt = _ffn_half_step(xt, g_ff2[l], *ff2)
    return xt.reshape(batch, seq, d)
```

```python
import functools

import jax
import jax.numpy as jnp
from jax import lax
from jax.experimental import pallas as pl
from jax.experimental.pallas import tpu as pltpu

CHUNK = 64
ROPE_THETA = 500000.0
EPS = 1e-6

MLA_HEADS = 12
MLA_NOPE = 128
MLA_ROPE = 64
MLA_V = 128
MLA_QK = MLA_ROPE + MLA_NOPE
MLA_QK_PAD = 256

DSA_HEADS = 12
DSA_HEAD_DIM = 128
DSA_ROT = DSA_HEAD_DIM // 4
DSA_MAX_TOPK = 256
IDX_HEADS = 32
IDX_DIM = 64
IDX_ROT = IDX_DIM // 4

MEM_HEADS = 4
MEM_HEAD_DIM = 256

LANES = 128
SUBLANES = 8
BF16_SUBLANES = 16
VMEM_LIMIT_BYTES = 56 * 1024 * 1024

F32 = jnp.float32
BF16 = jnp.bfloat16
NEG_INF = float("-inf")
INT_MIN = -2 ** 31
LOG2_E = 1.4426950408889634
NEG_INF_KEY = (0xFF800000 - (1 << 32)) ^ 0x7FFFFFFF


def _params(*semantics):
    return pltpu.CompilerParams(dimension_semantics=semantics, vmem_limit_bytes=VMEM_LIMIT_BYTES)


def _block(dim, preferred):
    if dim <= preferred:
        return dim
    b = preferred - preferred % LANES
    while b >= LANES:
        if dim % b == 0:
            return b
        b -= LANES
    return dim


def _rms_scale(x, width):
    return lax.rsqrt(jnp.sum(x * x, axis=-1, keepdims=True) * (1.0 / width) + EPS)


def _dot(a, b):
    return jnp.dot(a, b, preferred_element_type=F32)


def _dot_nt(a, b):
    return lax.dot_general(a, b, (((1,), (1,)), ((), ())), preferred_element_type=F32)


def _sigmoid(x):
    return 1.0 / (1.0 + jnp.exp(-x))


def _query_premultiplier(head_dim):
    return head_dim ** -0.5 * LOG2_E


def _chunk_of(frame):
    return lax.shift_right_logical(frame, CHUNK.bit_length() - 1)


def _rmsnorm_kernel(x_ref, g_ref, o_ref):
    x = x_ref[...]
    o_ref[...] = (x * _rms_scale(x, x.shape[-1]) * g_ref[...]).astype(o_ref.dtype)


def _rmsnorm(x, g, out_dtype=BF16):
    m, d = x.shape
    bm = _block(m, 512)
    return pl.pallas_call(
        _rmsnorm_kernel,
        grid=(m // bm,),
        in_specs=[pl.BlockSpec((bm, d), lambda i: (i, 0)), pl.BlockSpec((1, d), lambda i: (0, 0))],
        out_specs=pl.BlockSpec((bm, d), lambda i: (i, 0)),
        out_shape=jax.ShapeDtypeStruct((m, d), out_dtype),
        compiler_params=_params("parallel"),
        name="rmsnorm",
    )(x, g.reshape(1, d))


def _matmul_kernel(a_ref, b_ref, o_ref):
    o_ref[...] = _dot(a_ref[...], b_ref[...]).astype(o_ref.dtype)


def _matmul(a, b, out_dtype, bm=1024, bn=512, name="matmul"):
    m, k = a.shape
    n = b.shape[1]
    bm, bn = _block(m, bm), _block(n, bn)
    return pl.pallas_call(
        _matmul_kernel,
        grid=(m // bm, n // bn),
        in_specs=[pl.BlockSpec((bm, k), lambda i, j: (i, 0)), pl.BlockSpec((k, bn), lambda i, j: (0, j))],
        out_specs=pl.BlockSpec((bm, bn), lambda i, j: (i, j)),
        out_shape=jax.ShapeDtypeStruct((m, n), out_dtype),
        compiler_params=_params("parallel", "parallel"),
        name=name,
    )(a, b)


def _cast_slabs(weights):
    return [a.reshape(-1, a.shape[-1]) for a in weights]


def _cast_slab_spec(a, steps, step_of):
    rows = a.shape[0]
    r = next(r for r in range(BF16_SUBLANES, rows + 1, BF16_SUBLANES) if rows % r == 0 and rows // r <= steps)
    last = rows // r - 1
    return pl.BlockSpec((r, a.shape[1]), lambda *g: (jnp.minimum(step_of(*g), last), 0))


def _cast_in_kernel(src_refs, dst_refs):
    for src, dst in zip(src_refs, dst_refs):
        dst[...] = src[...].astype(dst.dtype)


def _swiglu_up_kernel(a_ref, wg_ref, wu_ref, *rest, n_casts):
    _cast_in_kernel(rest[:n_casts], rest[n_casts + 1:])
    o_ref = rest[n_casts]
    a = a_ref[...]
    g = _dot(a, wg_ref[...])
    u = _dot(a, wu_ref[...])
    o_ref[...] = (g * _sigmoid(g) * u).astype(o_ref.dtype)


def _swiglu_up(a, wg, wu, weights_to_cast=(), bm=1024, bn=512):
    m, k = a.shape
    n = wg.shape[1]
    bm, bn = _block(m, bm), _block(n, bn)
    grid = (m // bm, n // bn)
    slabs = _cast_slabs(weights_to_cast)
    slab_specs = [_cast_slab_spec(s, grid[0] * grid[1], lambda i, j: i * grid[1] + j) for s in slabs]
    outs = pl.pallas_call(
        functools.partial(_swiglu_up_kernel, n_casts=len(slabs)),
        grid=grid,
        in_specs=[pl.BlockSpec((bm, k), lambda i, j: (i, 0)),
                  pl.BlockSpec((k, bn), lambda i, j: (0, j)),
                  pl.BlockSpec((k, bn), lambda i, j: (0, j))] + slab_specs,
        out_specs=[pl.BlockSpec((bm, bn), lambda i, j: (i, j))] + slab_specs,
        out_shape=[jax.ShapeDtypeStruct((m, n), BF16)] + [jax.ShapeDtypeStruct(s.shape, BF16) for s in slabs],
        compiler_params=_params("arbitrary", "arbitrary"),
        name="swiglu_up",
    )(a, wg, wu, *slabs)
    return outs[0], tuple(o.reshape(w.shape) for o, w in zip(outs[1:], weights_to_cast))


def _matmul_residual_kernel(a_ref, b_ref, x_ref, o_ref, *, scale):
    o_ref[...] = x_ref[...] + scale * _dot(a_ref[...], b_ref[...])


def _matmul_residual(a, b, x, scale, bm=512, bn=512, name="matmul_residual"):
    m, k = a.shape
    n = b.shape[1]
    bm, bn = _block(m, bm), _block(n, bn)
    return pl.pallas_call(
        functools.partial(_matmul_residual_kernel, scale=scale),
        grid=(m // bm, n // bn),
        in_specs=[pl.BlockSpec((bm, k), lambda i, j: (i, 0)),
                  pl.BlockSpec((k, bn), lambda i, j: (0, j)),
                  pl.BlockSpec((bm, bn), lambda i, j: (i, j))],
        out_specs=pl.BlockSpec((bm, bn), lambda i, j: (i, j)),
        out_shape=jax.ShapeDtypeStruct((m, n), F32),
        compiler_params=_params("parallel", "parallel"),
        name=name,
    )(a, b, x)


def _ffn_half_step(x, g, w_gate, w_up, w_down, later_weights=()):
    h = _rmsnorm(x, g)
    own = () if w_down.dtype == BF16 else (w_down,)
    a, cast = _swiglu_up(h, w_gate.astype(BF16), w_up.astype(BF16), own + tuple(later_weights))
    if own:
        w_down, cast = cast[0], cast[1:]
    return _matmul_residual(a, w_down, x, 0.5, name="ffn_down"), cast


def _gate_combine_kernel(h_ref, ya_ref, yb_ref, yc_ref, wg_ref, wa_ref, wb_ref, wc_ref, o_ref):
    h = h_ref[...]
    out = _sigmoid(_dot(h, wg_ref[0])) * _dot(ya_ref[...], wa_ref[...])
    out = out + _sigmoid(_dot(h, wg_ref[1])) * _dot(yb_ref[...], wb_ref[...])
    out = out + _sigmoid(_dot(h, wg_ref[2])) * _dot(yc_ref[...], wc_ref[...])
    o_ref[...] = out.astype(o_ref.dtype)


def _gate_combine(h, ys, w_gate, w_os, bm=512, bn=512):
    m, d = h.shape
    n = w_gate.shape[-1]
    bm, bn = _block(m, bm), _block(n, bn)
    row = lambda w: pl.BlockSpec((bm, w), lambda i, j: (i, 0))
    col = lambda k: pl.BlockSpec((k, bn), lambda i, j: (0, j))
    return pl.pallas_call(
        _gate_combine_kernel,
        grid=(m // bm, n // bn),
        in_specs=[row(d)] + [row(y.shape[1]) for y in ys]
                 + [pl.BlockSpec((3, d, bn), lambda i, j: (0, 0, j))] + [col(w.shape[0]) for w in w_os],
        out_specs=pl.BlockSpec((bm, bn), lambda i, j: (i, j)),
        out_shape=jax.ShapeDtypeStruct((m, n), BF16),
        compiler_params=_params("parallel", "parallel"),
        name="gate_combine",
    )(h, *ys, w_gate, *w_os)


def _rope_tables_kernel(pos_ref, invf_ref, sgn_ref, cos_ref, sin_ref):
    pos = pos_ref[...].astype(F32)
    for t in range(invf_ref.shape[0]):
        ang = pos * invf_ref[t:t + 1, :]
        cos_ref[t] = jnp.cos(ang)
        sin_ref[t] = jnp.sin(ang) * sgn_ref[t:t + 1, :]


def _rope_lane_tables(rot, period):
    half = rot // 2
    inv_freq = ROPE_THETA ** (-jnp.arange(half, dtype=F32) / half)
    lane = jnp.arange(LANES) % period
    invf = jnp.where(lane < rot, inv_freq[lane % half], 0.0)
    sgn = jnp.where(lane < half, -1.0, jnp.where(lane < rot, 1.0, 0.0))
    return invf.astype(F32), sgn.astype(F32)


def _rope_tables(positions):
    t = positions.size
    specs = [_rope_lane_tables(MLA_ROPE, LANES), _rope_lane_tables(DSA_ROT, LANES),
             _rope_lane_tables(IDX_ROT, IDX_DIM)]
    invf = jnp.stack([s[0] for s in specs])
    sgn = jnp.stack([s[1] for s in specs])
    bm = _block(t, 1024)
    n = invf.shape[0]
    const = pl.BlockSpec((n, LANES), lambda i: (0, 0))
    out = pl.BlockSpec((n, bm, LANES), lambda i: (0, i, 0))
    return pl.pallas_call(
        _rope_tables_kernel,
        grid=(t // bm,),
        in_specs=[pl.BlockSpec((bm, 1), lambda i: (i, 0)), const, const],
        out_specs=[out, out],
        out_shape=[jax.ShapeDtypeStruct((n, t, LANES), F32)] * 2,
        compiler_params=_params("parallel"),
        name="rope_tables",
    )(positions.reshape(t, 1), invf, sgn)


def _rope_tile(x, cos, sin, half, period):
    lane = lax.broadcasted_iota(jnp.int32, x.shape, 1) & (period - 1)
    partner = jnp.where(lane < half, pltpu.roll(x, LANES - half, 1), pltpu.roll(x, half, 1))
    return x * cos + partner * sin


def _mla_q_kernel(cq_ref, gcq_ref, w_ref, gq_ref, cos_ref, sin_ref, o_ref):
    x = cq_ref[...]
    xn = (x * _rms_scale(x, x.shape[-1]) * gcq_ref[...]).astype(BF16)
    q = _dot(xn, w_ref[...])
    cos, sin = cos_ref[0], sin_ref[0]
    g = gq_ref[...] * _query_premultiplier(MLA_QK)
    for h in range(MLA_HEADS):
        lo = h * MLA_QK_PAD
        qh = q[:, lo:lo + MLA_QK_PAD]
        y = qh * _rms_scale(qh, MLA_QK) * g
        o_ref[:, lo:lo + LANES] = _rope_tile(y[:, :LANES], cos, sin, MLA_ROPE // 2, LANES).astype(o_ref.dtype)
        o_ref[:, lo + LANES:lo + MLA_QK_PAD] = y[:, LANES:].astype(o_ref.dtype)


def _store_heads_transposed(x, vt_ref):
    heads, hd, _ = vt_ref.shape
    for h in range(heads):
        vt_ref[h] = x[:, h * hd:(h + 1) * hd].T.astype(vt_ref.dtype)


def _mla_kv_kernel(ckv_ref, kr_ref, gckv_ref, wk_ref, wv_ref, gk_ref, cos_ref, sin_ref, k_ref, vt_ref):
    x = ckv_ref[...]
    xn = (x * _rms_scale(x, x.shape[-1]) * gckv_ref[...]).astype(BF16)
    k_nope = _dot(xn, wk_ref[...])
    _store_heads_transposed(_dot(xn, wv_ref[...]), vt_ref)
    kr = kr_ref[...]
    lane = lax.broadcasted_iota(jnp.int32, kr.shape, 1)
    kr = jnp.where(lane < MLA_ROPE, kr, 0.0)
    cos, sin = cos_ref[0], sin_ref[0]
    g = gk_ref[...]
    for h in range(MLA_HEADS):
        lo = h * MLA_QK_PAD
        ka = k_nope[:, lo:lo + LANES] + kr
        kb = k_nope[:, lo + LANES:lo + MLA_QK_PAD]
        ms = (jnp.sum(ka * ka, axis=-1, keepdims=True) + jnp.sum(kb * kb, axis=-1, keepdims=True)) * (1.0 / MLA_QK)
        r = lax.rsqrt(ms + EPS)
        ya = ka * r * g[:, :LANES]
        yb = kb * r * g[:, LANES:]
        k_ref[:, lo:lo + LANES] = _rope_tile(ya, cos, sin, MLA_ROPE // 2, LANES).astype(k_ref.dtype)
        k_ref[:, lo + LANES:lo + MLA_QK_PAD] = yb.astype(k_ref.dtype)


def _pad_last(a, width):
    return jnp.pad(a, [(0, 0)] * (a.ndim - 1) + [(0, width - a.shape[-1])])


def _mla_prep(proj, col, g_cq, w_uq, g_ckv, w_ukv, g_q, g_k, cos, sin, batch):
    t = proj.shape[0]
    q_rank, kv_rank = w_uq.shape[0], w_ukv.shape[0]
    hp = MLA_HEADS * MLA_QK_PAD
    w_q = _pad_last(w_uq.reshape(q_rank, MLA_HEADS, MLA_QK), MLA_QK_PAD).reshape(q_rank, hp).astype(BF16)
    w_kv = w_ukv.reshape(kv_rank, MLA_HEADS, MLA_NOPE + MLA_V)
    w_k = jnp.pad(w_kv[..., :MLA_NOPE], ((0, 0), (0, 0), (MLA_ROPE, MLA_QK_PAD - MLA_QK)))
    w_k = w_k.reshape(kv_rank, hp).astype(BF16)
    w_v = w_kv[..., MLA_NOPE:].reshape(kv_rank, MLA_HEADS * MLA_V).astype(BF16)
    g_qp = _pad_last(g_q, MLA_QK_PAD).reshape(1, MLA_QK_PAD)
    g_kp = _pad_last(g_k, MLA_QK_PAD).reshape(1, MLA_QK_PAD)

    bm = _block(t, 512)
    rows = lambda w, c: pl.BlockSpec((bm, w), lambda i, c=c: (i, c))
    full = lambda a: pl.BlockSpec(a.shape, lambda i: (0,) * a.ndim)
    table = pl.BlockSpec((1, bm, LANES), lambda i: (0, i, 0))
    g_cq2, g_ckv2 = g_cq.reshape(1, -1), g_ckv.reshape(1, -1)

    q = pl.pallas_call(
        _mla_q_kernel,
        grid=(t // bm,),
        in_specs=[rows(q_rank, col["c_q"] // q_rank), full(g_cq2), full(w_q), full(g_qp), table, table],
        out_specs=pl.BlockSpec((bm, hp), lambda i: (i, 0)),
        out_shape=jax.ShapeDtypeStruct((t, hp), BF16),
        compiler_params=_params("parallel"),
        name="mla_q_prep",
    )(proj, g_cq2, w_q, g_qp, cos, sin)

    k, v = pl.pallas_call(
        _mla_kv_kernel,
        grid=(t // bm,),
        in_specs=[rows(kv_rank, col["c_kv"] // kv_rank), rows(LANES, col["k_r"] // LANES), full(g_ckv2),
                  full(w_k), full(w_v), full(g_kp), table, table],
        out_specs=[pl.BlockSpec((bm, hp), lambda i: (i, 0)),
                   _heads_transposed_spec(MLA_HEADS, MLA_V, bm, t // batch)],
        out_shape=[jax.ShapeDtypeStruct((t, hp), BF16),
                   jax.ShapeDtypeStruct((batch, MLA_HEADS, MLA_V, t // batch), BF16)],
        compiler_params=_params("parallel"),
        name="mla_kv_prep",
    )(proj, proj, g_ckv2, w_k, w_v, g_kp, cos, sin)
    return q, k, v


ATTN_HEADS_PER_STEP = 3


def _attn_scratch(tq, dk, dv):
    per_head = [pltpu.VMEM((1, tq), F32), pltpu.VMEM((1, tq), F32), pltpu.VMEM((dv, tq), F32)]
    per_parity = [pltpu.VMEM((tq, tq), F32), pltpu.VMEM((1, tq), F32),
                  pltpu.VMEM((tq, tq), BF16), pltpu.VMEM((1, tq), F32)]
    return (per_head + 2 * per_parity) * ATTN_HEADS_PER_STEP + [pltpu.VMEM((ATTN_HEADS_PER_STEP, dk, tq), BF16)]


def _attn_kernel(q_ref, k_ref, vt_ref, *rest, tq, dk, dv, selected_keys):
    n_scratch = 11 * ATTN_HEADS_PER_STEP + 1
    scratch = rest[-n_scratch:]
    qt_ref = scratch[-1]
    sel_ref = rest[0] if selected_keys else None
    o_ref = rest[-n_scratch - 1]
    i = pl.program_id(2)
    heads = range(ATTN_HEADS_PER_STEP)
    m_of, l_of, acc_of, s_of, mx_of, p_of, al_of = [], [], [], [], [], [], []
    for h in heads:
        refs = scratch[11 * h:11 * (h + 1)]
        m_of.append(refs[0]); l_of.append(refs[1]); acc_of.append(refs[2])
        s_of.append((refs[3], refs[7])); mx_of.append((refs[4], refs[8]))
        p_of.append((refs[5], refs[9])); al_of.append((refs[6], refs[10]))
    for h in heads:
        qt_ref[h] = q_ref[:, h * dk:(h + 1) * dk].T

    for h in heads:
        m_of[h][...] = jnp.full((1, tq), NEG_INF, F32)
        l_of[h][...] = jnp.zeros((1, tq), F32)
        acc_of[h][...] = jnp.zeros((dv, tq), F32)
        s_of[h][1][...] = jnp.full((tq, tq), NEG_INF, F32)
        mx_of[h][1][...] = jnp.full((1, tq), NEG_INF, F32)
        p_of[h][0][...] = jnp.zeros((tq, tq), BF16)
        al_of[h][0][...] = jnp.zeros((1, tq), F32)

    def scores(block, par, diagonal):
        off = pl.multiple_of(block * tq, tq)
        if selected_keys:
            visible = sel_ref[pl.ds(off, tq), :].astype(F32) > 0.0
        elif diagonal:
            key_chunk = _chunk_of(lax.broadcasted_iota(jnp.int32, (tq, tq), 0))
            query_chunk = _chunk_of(lax.broadcasted_iota(jnp.int32, (tq, tq), 1))
            visible = key_chunk <= query_chunk
        else:
            visible = None
        for h in heads:
            s = _dot(k_ref[pl.ds(off, tq), h * dk:(h + 1) * dk], qt_ref[h])
            if visible is not None:
                s = jnp.where(visible, s, NEG_INF)
            s_of[h][par][...] = s
            mx_of[h][par][...] = jnp.max(s, axis=0, keepdims=True)

    def softmax(par):
        for h in heads:
            m_prev = m_of[h][...]
            m_new = jnp.maximum(m_prev, mx_of[h][par][...])
            m_safe = jnp.where(m_new == NEG_INF, 0.0, m_new)
            alpha = jnp.exp2(m_prev - m_safe)
            p = jnp.exp2(s_of[h][par][...] - m_safe)
            l_of[h][...] = alpha * l_of[h][...] + jnp.sum(p, axis=0, keepdims=True)
            p_of[h][par][...] = p.astype(BF16)
            al_of[h][par][...] = alpha
            m_of[h][...] = m_new

    def values(block, par):
        off = pl.multiple_of(jnp.maximum(block, 0) * tq, tq)
        for h in heads:
            acc_of[h][...] = al_of[h][par][...] * acc_of[h][...] + _dot(vt_ref[h, :, pl.ds(off, tq)],
                                                                         p_of[h][par][...])

    def step(t, par, diagonal=False):
        values(t - 2, par)
        softmax(1 - par)
        scores(t, par, diagonal)

    def pair(u, carry):
        step(2 * u, 0)
        step(2 * u + 1, 1)
        return carry

    lax.fori_loop(0, lax.shift_right_logical(i, 1), pair, 0)

    def drain(par):
        step(i, par, diagonal=True)
        values(i - 1, 1 - par)
        softmax(par)
        values(i, par)

    odd = (i & 1) == 1

    @pl.when(odd)
    def _():
        step(i - 1, 0)
        drain(1)

    @pl.when(jnp.logical_not(odd))
    def _():
        drain(0)

    for h in heads:
        o = acc_of[h][...] / l_of[h][...]
        o_ref[:, h * dv:(h + 1) * dv] = o.T.astype(o_ref.dtype)


def _attention(q, k, vt, selected_t, batch, heads, dk, dv, name, tq=256):
    t = q.shape[0]
    s = t // batch
    tq = _block(s, tq)
    nq = s // tq
    hp = ATTN_HEADS_PER_STEP
    in_specs = [pl.BlockSpec((tq, hp * dk), lambda b, h, i: (b * nq + i, h)),
                pl.BlockSpec((s, hp * dk), lambda b, h, i: (b, h)),
                pl.BlockSpec((None, hp, dv, s), lambda b, h, i: (b, h, 0, 0))]
    operands = [q, k, vt]
    if selected_t is not None:
        in_specs.append(pl.BlockSpec((None, s, tq), lambda b, h, i: (b, 0, i)))
        operands.append(selected_t)
    return pl.pallas_call(
        functools.partial(_attn_kernel, tq=tq, dk=dk, dv=dv, selected_keys=selected_t is not None),
        grid=(batch, heads // hp, nq),
        in_specs=in_specs,
        out_specs=pl.BlockSpec((tq, hp * dv), lambda b, h, i: (b * nq + i, h)),
        out_shape=jax.ShapeDtypeStruct((t, heads * dv), BF16),
        scratch_shapes=_attn_scratch(tq, dk, dv),
        compiler_params=_params("parallel", "parallel", "parallel"),
        name=name,
    )(*operands)


def _heads_transposed_spec(heads, hd, bm, seq):
    blocks_per_batch = seq // bm
    return pl.BlockSpec((None, heads, hd, bm), lambda i: (i // blocks_per_batch, 0, 0, i % blocks_per_batch))


def _heads_transposed_kernel(x_ref, vt_ref):
    _store_heads_transposed(x_ref[...], vt_ref)


def _heads_transposed(proj, col_block, batch, heads, hd):
    t = proj.shape[0]
    bm = _block(t // batch, 512)
    return pl.pallas_call(
        _heads_transposed_kernel,
        grid=(t // bm,),
        in_specs=[pl.BlockSpec((bm, heads * hd), lambda i: (i, col_block))],
        out_specs=_heads_transposed_spec(heads, hd, bm, t // batch),
        out_shape=jax.ShapeDtypeStruct((batch, heads, hd, t // batch), BF16),
        compiler_params=_params("parallel"),
        name="heads_transposed",
    )(proj)


def _head_norm_rope_kernel(x_ref, g_ref, cos_ref, sin_ref, o_ref, *, heads, premultiplier):
    cos, sin = cos_ref[0], sin_ref[0]
    g = g_ref[...] * premultiplier
    for h in range(heads):
        x = x_ref[:, h * LANES:(h + 1) * LANES]
        y = x * _rms_scale(x, LANES) * g
        o_ref[:, h * LANES:(h + 1) * LANES] = _rope_tile(y, cos, sin, DSA_ROT // 2, LANES).astype(o_ref.dtype)


def _head_norm_rope(proj, col_block, g, cos, sin, premultiplier=1.0):
    t = proj.shape[0]
    width = DSA_HEADS * DSA_HEAD_DIM
    bm = _block(t, 512)
    table = pl.BlockSpec((1, bm, LANES), lambda i: (1, i, 0))
    return pl.pallas_call(
        functools.partial(_head_norm_rope_kernel, heads=DSA_HEADS, premultiplier=premultiplier),
        grid=(t // bm,),
        in_specs=[pl.BlockSpec((bm, width), lambda i: (i, col_block)),
                  pl.BlockSpec((1, LANES), lambda i: (0, 0)), table, table],
        out_specs=pl.BlockSpec((bm, width), lambda i: (i, 0)),
        out_shape=jax.ShapeDtypeStruct((t, width), BF16),
        compiler_params=_params("parallel"),
        name="head_norm_rope",
    )(proj, g.reshape(1, LANES), cos, sin)


def _rope_only_kernel(x_ref, cos_ref, sin_ref, o_ref):
    cos, sin = cos_ref[0], sin_ref[0]
    for c in range(x_ref.shape[1] // LANES):
        x = x_ref[:, c * LANES:(c + 1) * LANES]
        o_ref[:, c * LANES:(c + 1) * LANES] = _rope_tile(x, cos, sin, IDX_ROT // 2, IDX_DIM).astype(o_ref.dtype)


def _rope_only(proj, width, col_block, cos, sin):
    t = proj.shape[0]
    bm = _block(t, 512)
    table = pl.BlockSpec((1, bm, LANES), lambda i: (2, i, 0))
    return pl.pallas_call(
        _rope_only_kernel,
        grid=(t // bm,),
        in_specs=[pl.BlockSpec((bm, width), lambda i: (i, col_block)), table, table],
        out_specs=pl.BlockSpec((bm, width), lambda i: (i, 0)),
        out_shape=jax.ShapeDtypeStruct((t, width), BF16),
        compiler_params=_params("parallel"),
        name="indexer_rope",
    )(proj, cos, sin)


SCORE_COLS = 256
COUNT_ACC_ROWS = 64


def _sortable_key(x):
    bits = lax.bitcast_convert_type(x, jnp.int32)
    return bits ^ ((bits >> 31) & 0x7FFFFFFF)


def _indexer_kernel(qi_ref, kit_ref, w_ref, *rest, tq, tk, topk, seq, w_scale, n_casts):
    cast_src, sel_ref, cast_dst = rest[:n_casts], rest[n_casts], rest[n_casts + 1:2 * n_casts + 1]
    key_ref, wb_ref, qh_ref = rest[2 * n_casts + 1:]
    _cast_in_kernel(cast_src, cast_dst)

    i = pl.program_id(1)
    n_tiles = pl.cdiv((i + 1) * tq, tk)
    row_chunk = _chunk_of(i * tq + lax.broadcasted_iota(jnp.int32, (tq, 1), 0))

    w = w_ref[...] * w_scale
    for h in range(IDX_HEADS):
        wb_ref[h] = jnp.broadcast_to(w[:, h:h + 1], (tq, LANES))
        qh_ref[h] = qi_ref[:, h * IDX_DIM:(h + 1) * IDX_DIM]

    def score_tile(j, carry):
        for part in range(tk // SCORE_COLS):
            off = pl.multiple_of(j * tk + part * SCORE_COLS, SCORE_COLS)
            kt = kit_ref[:, pl.ds(off, SCORE_COLS)]
            acc = [jnp.zeros((tq, LANES), F32) for _ in range(SCORE_COLS // LANES)]
            for h in range(IDX_HEADS):
                logits = _dot(qh_ref[h], kt)
                wh = wb_ref[h]
                for c in range(SCORE_COLS // LANES):
                    acc[c] = acc[c] + jnp.maximum(logits[:, c * LANES:(c + 1) * LANES], 0.0) * wh
            for c in range(SCORE_COLS // LANES):
                col = off + c * LANES + lax.broadcasted_iota(jnp.int32, (1, LANES), 1)
                score = jnp.where(_chunk_of(col) <= row_chunk, acc[c], NEG_INF)
                key_ref[pl.ds(off + c * LANES, LANES), :] = _sortable_key(score.T)
        return carry

    lax.fori_loop(0, n_tiles, score_tile, 0)

    def count(pred):
        rows = min(tk, COUNT_ACC_ROWS)

        def tile(j, acc):
            off = pl.multiple_of(j * tk, tk)
            hit = jnp.where(pred(key_ref[pl.ds(off, tk), :], off), 1.0, 0.0)
            return acc + jnp.sum(hit.reshape(tk // rows, rows, tq), axis=0)
        acc = lax.fori_loop(0, n_tiles, tile, jnp.zeros((rows, tq), F32))
        return jnp.sum(acc, axis=0, keepdims=True)

    def value_bit(b, lo):
        cand = lo + lax.shift_left(jnp.int32(1), 31 - b)
        n_ge = count(lambda kt, _: kt >= cand)
        return jnp.where(n_ge >= topk, cand, lo)

    thr = lax.fori_loop(0, 32, value_bit, jnp.full((1, tq), INT_MIN, jnp.int32))
    n_ge = count(lambda kt, _: kt >= thr)
    surplus = jnp.logical_and(n_ge > topk, thr > NEG_INF_KEY)

    def write_tile(j, carry):
        off = pl.multiple_of(j * tk, tk)
        kt = key_ref[pl.ds(off, tk), :]
        chosen = jnp.logical_and(kt >= thr, kt > NEG_INF_KEY)
        sel_ref[pl.ds(off, tk), :] = jnp.where(chosen, 1.0, 0.0).astype(sel_ref.dtype)
        return carry

    lax.fori_loop(0, n_tiles, write_tile, 0)

    def clear_tile(j, carry):
        off = pl.multiple_of(j * tk, tk)
        sel_ref[pl.ds(off, tk), :] = jnp.zeros((tk, tq), sel_ref.dtype)
        return carry

    lax.fori_loop(n_tiles, seq // tk, clear_tile, 0)

    @pl.when(jnp.max(jnp.where(surplus, 1.0, 0.0)) > 0.0)
    def _():
        n_gt = count(lambda kt, _: kt > thr)
        need = topk - n_gt

        def position(first_key):
            return first_key + lax.broadcasted_iota(jnp.int32, (tk, tq), 0)

        def position_bit(b, base):
            cand = base + lax.shift_left(jnp.int32(1), (seq - 1).bit_length() - 1 - b)
            n_before = count(lambda kt, k0: jnp.logical_and(kt == thr, position(k0) < cand))
            return jnp.where(n_before < need, cand, base)

        last = lax.fori_loop(0, (seq - 1).bit_length(), position_bit, jnp.zeros((1, tq), jnp.int32))

        def rewrite_tile(j, carry):
            off = pl.multiple_of(j * tk, tk)
            kt = key_ref[pl.ds(off, tk), :]
            chosen = jnp.logical_or(kt > thr, jnp.logical_and(kt == thr, position(off) <= last))
            chosen = jnp.logical_and(chosen, kt > NEG_INF_KEY)
            sel_ref[pl.ds(off, tk), :] = jnp.where(chosen, 1.0, 0.0).astype(sel_ref.dtype)
            return carry

        lax.fori_loop(0, n_tiles, rewrite_tile, 0)


def _indexer_select(qi, kit, w, w_col_block, batch, topk, weights_to_cast=(), tq=128, tk=512):
    s = kit.shape[2]
    tq = _block(s, tq)
    tk = _block(s, tk)
    assert tk % SCORE_COLS == 0
    assert tq == LANES, "score tiles are transposed one (LANES, LANES) tile at a time"
    nq = s // tq
    steps = batch * nq
    w_scale = (IDX_DIM ** -0.5) * (IDX_HEADS ** -0.5)

    slabs = _cast_slabs(weights_to_cast)
    slab_spec = lambda a: _cast_slab_spec(a, steps, lambda b, i: b * nq + i)

    outs = pl.pallas_call(
        functools.partial(_indexer_kernel, tq=tq, tk=tk, topk=topk, seq=s, w_scale=w_scale, n_casts=len(slabs)),
        grid=(batch, nq),
        in_specs=[pl.BlockSpec((tq, IDX_HEADS * IDX_DIM), lambda b, i: (b * nq + i, 0)),
                  pl.BlockSpec((None, IDX_DIM, s), lambda b, i: (b, 0, 0)),
                  pl.BlockSpec((tq, LANES), lambda b, i: (b * nq + i, w_col_block))]
                 + [slab_spec(a) for a in slabs],
        out_specs=[pl.BlockSpec((None, s, tq), lambda b, i: (b, 0, i))] + [slab_spec(a) for a in slabs],
        out_shape=[jax.ShapeDtypeStruct((batch, s, s), BF16)]
                  + [jax.ShapeDtypeStruct(a.shape, BF16) for a in slabs],
        scratch_shapes=[pltpu.VMEM((s, tq), jnp.int32), pltpu.VMEM((IDX_HEADS, tq, LANES), F32),
                        pltpu.VMEM((IDX_HEADS, tq, IDX_DIM), BF16)],
        compiler_params=_params("arbitrary", "arbitrary"),
        name="indexer_select",
    )(qi, kit, w, *slabs)
    return outs[0], tuple(o.reshape(a.shape) for o, a in zip(outs[1:], weights_to_cast))


def _memory_attn_kernel(q_ref, kv_ref, gq_ref, gk_ref, o_ref, *, scale):
    hd = MEM_HEAD_DIM
    gq, gk = gq_ref[...], gk_ref[...]
    for h in range(MEM_HEADS):
        q = q_ref[:, h * hd:(h + 1) * hd]
        k = kv_ref[:, h * hd:(h + 1) * hd]
        v = kv_ref[:, (MEM_HEADS + h) * hd:(MEM_HEADS + h + 1) * hd].astype(BF16)
        qn = (q * _rms_scale(q, hd) * gq).astype(BF16)
        kn = (k * _rms_scale(k, hd) * gk).astype(BF16)
        s = _dot_nt(qn, kn) * scale
        e = jnp.exp(s - jnp.max(s, axis=-1, keepdims=True))
        p = e / jnp.sum(e, axis=-1, keepdims=True)
        o_ref[:, h * hd:(h + 1) * hd] = _dot(p.astype(BF16), v).astype(o_ref.dtype)


def _memory_attention(proj, col_block, kv, g_q, g_k, batch, tq=512):
    t = proj.shape[0]
    s = t // batch
    m = kv.shape[0] // batch
    width = MEM_HEADS * MEM_HEAD_DIM
    tq = _block(s, tq)
    nq = s // tq
    g_spec = pl.BlockSpec((1, MEM_HEAD_DIM), lambda b, i: (0, 0))
    return pl.pallas_call(
        functools.partial(_memory_attn_kernel, scale=MEM_HEAD_DIM ** -0.5),
        grid=(batch, nq),
        in_specs=[pl.BlockSpec((tq, width), lambda b, i: (b * nq + i, col_block)),
                  pl.BlockSpec((m, 2 * width), lambda b, i: (b, 0)), g_spec, g_spec],
        out_specs=pl.BlockSpec((tq, width), lambda b, i: (b * nq + i, 0)),
        out_shape=jax.ShapeDtypeStruct((t, width), BF16),
        compiler_params=_params("parallel", "parallel"),
        name="memory_attention",
    )(proj, kv, g_q.reshape(1, -1), g_k.reshape(1, -1))


def _reordered_input_projection(w_in, q_rank, kv_rank):
    dsa = DSA_HEADS * DSA_HEAD_DIM
    assert q_rank == dsa, "the reordered projection layout assumes equal widths for these groups"
    sizes = [("c_q", q_rank), ("c_kv", kv_rank), ("k_r", MLA_ROPE), ("q_d", dsa), ("k_d", dsa), ("v_d", dsa),
             ("q_i", IDX_HEADS * IDX_DIM), ("k_i", IDX_DIM), ("w_i", IDX_HEADS), ("q_m", MEM_HEADS * MEM_HEAD_DIM)]
    src, start = {}, 0
    for name, n in sizes:
        src[name] = (start, n)
        start += n
    assert start == w_in.shape[1]
    take = lambda name: w_in[:, src[name][0]:src[name][0] + src[name][1]]
    pieces = [("c_q", take("c_q")), ("q_d", take("q_d")), ("k_d", take("k_d")), ("v_d", take("v_d")),
              ("q_i", take("q_i")), ("q_m", take("q_m")), ("c_kv", take("c_kv")),
              ("k_r", jnp.concatenate([take("k_r"), take("k_i")], axis=1)),
              ("w_i", _pad_last(take("w_i"), LANES))]
    col, start = {}, 0
    for name, piece in pieces:
        assert start % piece.shape[1] == 0, name
        col[name] = start
        start += piece.shape[1]
    return jnp.concatenate([p for _, p in pieces], axis=1), col


def _hybrid_mixer(x, h, mem, positions, w_in_r, col, g_cq, w_uq, g_ckv, w_ukv, g_q_mla, g_k_mla, g_q_dsa, g_k_dsa,
                  g_mem, w_mem_kv, g_q_mem, g_k_mem, w_o_mla, w_o_dsa, w_o_mem, w_gate, w_out, batch, later_weights):
    t, d = h.shape
    s = t // batch
    dsa = DSA_HEADS * DSA_HEAD_DIM
    proj = _matmul(h, w_in_r, F32, bm=1024, bn=768, name="input_projection")
    cos, sin = _rope_tables(positions)

    q_a, k_a, vt_a = _mla_prep(proj, col, g_cq, w_uq, g_ckv, w_ukv, g_q_mla, g_k_mla, cos, sin, batch)
    y_mla = _attention(q_a, k_a, vt_a, None, batch, MLA_HEADS,
                       MLA_QK_PAD, MLA_V, "chunk_causal_attention")

    q_b = _head_norm_rope(proj, col["q_d"] // dsa, g_q_dsa, cos, sin, _query_premultiplier(DSA_HEAD_DIM))
    k_b = _head_norm_rope(proj, col["k_d"] // dsa, g_k_dsa, cos, sin)
    vt_b = _heads_transposed(proj, col["v_d"] // dsa, batch, DSA_HEADS, DSA_HEAD_DIM)
    qi_width = IDX_HEADS * IDX_DIM
    qi = _rope_only(proj, qi_width, col["q_i"] // qi_width, cos, sin)
    ki = _rope_only(proj, LANES, col["k_r"] // LANES, cos, sin)[:, IDX_DIM:]
    kit = ki.reshape(batch, s, IDX_DIM).transpose(0, 2, 1)
    topk = min(DSA_MAX_TOPK, s // 4)
    own_weights = (w_gate, w_o_mla, w_o_dsa, w_o_mem, w_out)
    selected_t, cast = _indexer_select(qi, kit, proj, col["w_i"] // LANES, batch, topk,
                                       own_weights + tuple(later_weights))
    w_gate, w_o_mla, w_o_dsa, w_o_mem, w_out = cast[:len(own_weights)]
    y_dsa = _attention(q_b, k_b, vt_b, selected_t, batch, DSA_HEADS,
                       DSA_HEAD_DIM, DSA_HEAD_DIM, "masked_attention")

    m_len = mem.shape[1]
    mem_n = _rmsnorm(mem.reshape(batch * m_len, d), g_mem)
    kv_mem = _matmul(mem_n, w_mem_kv.astype(BF16), F32, bm=512, bn=512, name="memory_kv")
    y_mem = _memory_attention(proj, col["q_m"] // (MEM_HEADS * MEM_HEAD_DIM), kv_mem, g_q_mem, g_k_mem, batch)

    gated = _gate_combine(h, (y_mla, y_dsa, y_mem), w_gate, (w_o_mla, w_o_dsa, w_o_mem))
    return (_matmul_residual(gated, w_out, x, 1.0, bm=1024, name="mixer_out"), cast[len(own_weights):])


def kernel(x, mem, positions, g_ff1, w_ff1_gate, w_ff1_up, w_ff1_down, g_mix, w_in, g_cq, w_uq, g_ckv, w_ukv, g_q_mla, g_k_mla, g_q_dsa, g_k_dsa, g_mem, w_mem_kv, g_q_mem, g_k_mem, w_o_mla, w_o_dsa, w_o_mem, w_gate, w_out, g_ff2, w_ff2_gate, w_ff2_up, w_ff2_down):
    batch, seq, d = x.shape
    xt = x.reshape(batch * seq, d)
    for l in range(g_ff1.shape[0]):
        w_in_r, col = _reordered_input_projection(w_in[l], w_uq.shape[1], w_ukv.shape[1])
        xt, (w_in_r,) = _ffn_half_step(xt, g_ff1[l], w_ff1_gate[l], w_ff1_up[l], w_ff1_down[l], (w_in_r,))
        h = _rmsnorm(xt, g_mix[l])
        xt, ff2 = _hybrid_mixer(xt, h, mem, positions, w_in_r, col, g_cq[l], w_uq[l], g_ckv[l], w_ukv[l],
                                g_q_mla[l], g_k_mla[l], g_q_dsa[l], g_k_dsa[l], g_mem[l], w_mem_kv[l],
                                g_q_mem[l], g_k_mem[l], w_o_mla[l], w_o_dsa[l], w_o_mem[l], w_gate[l], w_out[l],
                                batch, (w_ff2_gate[l], w_ff2_up[l], w_ff2_down[l]))
        xt, _ = _ffn_half_step(xt, g_ff2[l], *ff2)
    return xt.reshape(batch, seq, d)
```

```python
import functools

import jax
import jax.numpy as jnp
from jax import lax
from jax.experimental import pallas as pl
from jax.experimental.pallas import tpu as pltpu

CHUNK = 64
ROPE_THETA = 500000.0
EPS = 1e-6

MLA_HEADS = 12
MLA_NOPE = 128
MLA_ROPE = 64
MLA_V = 128
MLA_QK = MLA_ROPE + MLA_NOPE
MLA_QK_PAD = 256

DSA_HEADS = 12
DSA_HEAD_DIM = 128
DSA_ROT = DSA_HEAD_DIM // 4
DSA_MAX_TOPK = 256
IDX_HEADS = 32
IDX_DIM = 64
IDX_ROT = IDX_DIM // 4

MEM_HEADS = 4
MEM_HEAD_DIM = 256

LANES = 128
SUBLANES = 8
BF16_SUBLANES = 16
VMEM_LIMIT_BYTES = 56 * 1024 * 1024

F32 = jnp.float32
BF16 = jnp.bfloat16
NEG_INF = float("-inf")
INT_MIN = -2 ** 31
LOG2_E = 1.4426950408889634
NEG_INF_KEY = (0xFF800000 - (1 << 32)) ^ 0x7FFFFFFF


def _params(*semantics):
    return pltpu.CompilerParams(dimension_semantics=semantics, vmem_limit_bytes=VMEM_LIMIT_BYTES)


def _block(dim, preferred):
    if dim <= preferred:
        return dim
    b = preferred - preferred % LANES
    while b >= LANES:
        if dim % b == 0:
            return b
        b -= LANES
    return dim


def _rms_scale(x, width):
    return lax.rsqrt(jnp.sum(x * x, axis=-1, keepdims=True) * (1.0 / width) + EPS)


def _dot(a, b):
    return jnp.dot(a, b, preferred_element_type=F32)


def _dot_nt(a, b):
    return lax.dot_general(a, b, (((1,), (1,)), ((), ())), preferred_element_type=F32)


def _sigmoid(x):
    return 1.0 / (1.0 + jnp.exp(-x))


def _query_premultiplier(head_dim):
    return head_dim ** -0.5 * LOG2_E


def _chunk_of(frame):
    return lax.shift_right_logical(frame, CHUNK.bit_length() - 1)


def _rmsnorm_kernel(x_ref, g_ref, o_ref):
    x = x_ref[...]
    o_ref[...] = (x * _rms_scale(x, x.shape[-1]) * g_ref[...]).astype(o_ref.dtype)


def _rmsnorm(x, g, out_dtype=BF16):
    m, d = x.shape
    bm = _block(m, 512)
    return pl.pallas_call(
        _rmsnorm_kernel,
        grid=(m // bm,),
        in_specs=[pl.BlockSpec((bm, d), lambda i: (i, 0)), pl.BlockSpec((1, d), lambda i: (0, 0))],
        out_specs=pl.BlockSpec((bm, d), lambda i: (i, 0)),
        out_shape=jax.ShapeDtypeStruct((m, d), out_dtype),
        compiler_params=_params("parallel"),
        name="rmsnorm",
    )(x, g.reshape(1, d))


def _matmul_kernel(a_ref, b_ref, o_ref):
    o_ref[...] = _dot(a_ref[...], b_ref[...]).astype(o_ref.dtype)


def _matmul(a, b, out_dtype, bm=1024, bn=512, name="matmul"):
    m, k = a.shape
    n = b.shape[1]
    bm, bn = _block(m, bm), _block(n, bn)
    return pl.pallas_call(
        _matmul_kernel,
        grid=(m // bm, n // bn),
        in_specs=[pl.BlockSpec((bm, k), lambda i, j: (i, 0)), pl.BlockSpec((k, bn), lambda i, j: (0, j))],
        out_specs=pl.BlockSpec((bm, bn), lambda i, j: (i, j)),
        out_shape=jax.ShapeDtypeStruct((m, n), out_dtype),
        compiler_params=_params("parallel", "parallel"),
        name=name,
    )(a, b)


def _cast_slabs(weights):
    return [a.reshape(-1, a.shape[-1]) for a in weights]


def _cast_slab_spec(a, steps, step_of):
    rows = a.shape[0]
    r = next(r for r in range(BF16_SUBLANES, rows + 1, BF16_SUBLANES) if rows % r == 0 and rows // r <= steps)
    last = rows // r - 1
    return pl.BlockSpec((r, a.shape[1]), lambda *g: (jnp.minimum(step_of(*g), last), 0))


def _cast_in_kernel(src_refs, dst_refs):
    for src, dst in zip(src_refs, dst_refs):
        dst[...] = src[...].astype(dst.dtype)


def _swiglu_up_kernel(a_ref, wg_ref, wu_ref, *rest, n_casts):
    _cast_in_kernel(rest[:n_casts], rest[n_casts + 1:])
    o_ref = rest[n_casts]
    a = a_ref[...]
    g = _dot(a, wg_ref[...])
    u = _dot(a, wu_ref[...])
    o_ref[...] = (g * _sigmoid(g) * u).astype(o_ref.dtype)


def _swiglu_up(a, wg, wu, weights_to_cast=(), bm=1024, bn=512):
    m, k = a.shape
    n = wg.shape[1]
    bm, bn = _block(m, bm), _block(n, bn)
    grid = (m // bm, n // bn)
    slabs = _cast_slabs(weights_to_cast)
    slab_specs = [_cast_slab_spec(s, grid[0] * grid[1], lambda i, j: i * grid[1] + j) for s in slabs]
    outs = pl.pallas_call(
        functools.partial(_swiglu_up_kernel, n_casts=len(slabs)),
        grid=grid,
        in_specs=[pl.BlockSpec((bm, k), lambda i, j: (i, 0)),
                  pl.BlockSpec((k, bn), lambda i, j: (0, j)),
                  pl.BlockSpec((k, bn), lambda i, j: (0, j))] + slab_specs,
        out_specs=[pl.BlockSpec((bm, bn), lambda i, j: (i, j))] + slab_specs,
        out_shape=[jax.ShapeDtypeStruct((m, n), BF16)] + [jax.ShapeDtypeStruct(s.shape, BF16) for s in slabs],
        compiler_params=_params("arbitrary", "arbitrary"),
        name="swiglu_up",
    )(a, wg, wu, *slabs)
    return outs[0], tuple(o.reshape(w.shape) for o, w in zip(outs[1:], weights_to_cast))


def _matmul_residual_kernel(a_ref, b_ref, x_ref, o_ref, *, scale):
    o_ref[...] = x_ref[...] + scale * _dot(a_ref[...], b_ref[...])


def _matmul_residual(a, b, x, scale, bm=512, bn=512, name="matmul_residual"):
    m, k = a.shape
    n = b.shape[1]
    bm, bn = _block(m, bm), _block(n, bn)
    return pl.pallas_call(
        functools.partial(_matmul_residual_kernel, scale=scale),
        grid=(m // bm, n // bn),
        in_specs=[pl.BlockSpec((bm, k), lambda i, j: (i, 0)),
                  pl.BlockSpec((k, bn), lambda i, j: (0, j)),
                  pl.BlockSpec((bm, bn), lambda i, j: (i, j))],
        out_specs=pl.BlockSpec((bm, bn), lambda i, j: (i, j)),
        out_shape=jax.ShapeDtypeStruct((m, n), F32),
        compiler_params=_params("parallel", "parallel"),
        name=name,
    )(a, b, x)


def _ffn_half_step(x, g, w_gate, w_up, w_down, later_weights=()):
    h = _rmsnorm(x, g)
    own = () if w_down.dtype == BF16 else (w_down,)
    a, cast = _swiglu_up(h, w_gate.astype(BF16), w_up.astype(BF16), own + tuple(later_weights))
    if own:
        w_down, cast = cast[0], cast[1:]
    return _matmul_residual(a, w_down, x, 0.5, name="ffn_down"), cast


def _gate_combine_kernel(h_ref, ya_ref, yb_ref, yc_ref, wg_ref, wa_ref, wb_ref, wc_ref, o_ref):
    h = h_ref[...]
    out = _sigmoid(_dot(h, wg_ref[0])) * _dot(ya_ref[...], wa_ref[...])
    out = out + _sigmoid(_dot(h, wg_ref[1])) * _dot(yb_ref[...], wb_ref[...])
    out = out + _sigmoid(_dot(h, wg_ref[2])) * _dot(yc_ref[...], wc_ref[...])
    o_ref[...] = out.astype(o_ref.dtype)


def _gate_combine(h, ys, w_gate, w_os, bm=512, bn=512):
    m, d = h.shape
    n = w_gate.shape[-1]
    bm, bn = _block(m, bm), _block(n, bn)
    row = lambda w: pl.BlockSpec((bm, w), lambda i, j: (i, 0))
    col = lambda k: pl.BlockSpec((k, bn), lambda i, j: (0, j))
    return pl.pallas_call(
        _gate_combine_kernel,
        grid=(m // bm, n // bn),
        in_specs=[row(d)] + [row(y.shape[1]) for y in ys]
                 + [pl.BlockSpec((3, d, bn), lambda i, j: (0, 0, j))] + [col(w.shape[0]) for w in w_os],
        out_specs=pl.BlockSpec((bm, bn), lambda i, j: (i, j)),
        out_shape=jax.ShapeDtypeStruct((m, n), BF16),
        compiler_params=_params("parallel", "parallel"),
        name="gate_combine",
    )(h, *ys, w_gate, *w_os)


def _rope_tables_kernel(pos_ref, invf_ref, sgn_ref, cos_ref, sin_ref):
    pos = pos_ref[...].astype(F32)
    for t in range(invf_ref.shape[0]):
        ang = pos * invf_ref[t:t + 1, :]
        cos_ref[t] = jnp.cos(ang)
        sin_ref[t] = jnp.sin(ang) * sgn_ref[t:t + 1, :]


def _rope_lane_tables(rot, period):
    half = rot // 2
    inv_freq = ROPE_THETA ** (-jnp.arange(half, dtype=F32) / half)
    lane = jnp.arange(LANES) % period
    invf = jnp.where(lane < rot, inv_freq[lane % half], 0.0)
    sgn = jnp.where(lane < half, -1.0, jnp.where(lane < rot, 1.0, 0.0))
    return invf.astype(F32), sgn.astype(F32)


def _rope_tables(positions):
    t = positions.size
    specs = [_rope_lane_tables(MLA_ROPE, LANES), _rope_lane_tables(DSA_ROT, LANES),
             _rope_lane_tables(IDX_ROT, IDX_DIM)]
    invf = jnp.stack([s[0] for s in specs])
    sgn = jnp.stack([s[1] for s in specs])
    bm = _block(t, 1024)
    n = invf.shape[0]
    const = pl.BlockSpec((n, LANES), lambda i: (0, 0))
    out = pl.BlockSpec((n, bm, LANES), lambda i: (0, i, 0))
    return pl.pallas_call(
        _rope_tables_kernel,
        grid=(t // bm,),
        in_specs=[pl.BlockSpec((bm, 1), lambda i: (i, 0)), const, const],
        out_specs=[out, out],
        out_shape=[jax.ShapeDtypeStruct((n, t, LANES), F32)] * 2,
        compiler_params=_params("parallel"),
        name="rope_tables",
    )(positions.reshape(t, 1), invf, sgn)


def _rope_tile(x, cos, sin, half, period):
    lane = lax.broadcasted_iota(jnp.int32, x.shape, 1) & (period - 1)
    partner = jnp.where(lane < half, pltpu.roll(x, LANES - half, 1), pltpu.roll(x, half, 1))
    return x * cos + partner * sin


def _mla_q_kernel(cq_ref, gcq_ref, w_ref, gq_ref, cos_ref, sin_ref, o_ref):
    x = cq_ref[...]
    xn = (x * _rms_scale(x, x.shape[-1]) * gcq_ref[...]).astype(BF16)
    q = _dot(xn, w_ref[...])
    cos, sin = cos_ref[0], sin_ref[0]
    g = gq_ref[...] * _query_premultiplier(MLA_QK)
    for h in range(MLA_HEADS):
        lo = h * MLA_QK_PAD
        qh = q[:, lo:lo + MLA_QK_PAD]
        y = qh * _rms_scale(qh, MLA_QK) * g
        o_ref[:, lo:lo + LANES] = _rope_tile(y[:, :LANES], cos, sin, MLA_ROPE // 2, LANES).astype(o_ref.dtype)
        o_ref[:, lo + LANES:lo + MLA_QK_PAD] = y[:, LANES:].astype(o_ref.dtype)


def _store_heads_transposed(x, vt_ref):
    heads, hd, _ = vt_ref.shape
    for h in range(heads):
        vt_ref[h] = x[:, h * hd:(h + 1) * hd].T.astype(vt_ref.dtype)


def _mla_kv_kernel(ckv_ref, kr_ref, gckv_ref, wk_ref, wv_ref, gk_ref, cos_ref, sin_ref, k_ref, vt_ref):
    x = ckv_ref[...]
    xn = (x * _rms_scale(x, x.shape[-1]) * gckv_ref[...]).astype(BF16)
    k_nope = _dot(xn, wk_ref[...])
    _store_heads_transposed(_dot(xn, wv_ref[...]), vt_ref)
    kr = kr_ref[...]
    lane = lax.broadcasted_iota(jnp.int32, kr.shape, 1)
    kr = jnp.where(lane < MLA_ROPE, kr, 0.0)
    cos, sin = cos_ref[0], sin_ref[0]
    g = gk_ref[...]
    for h in range(MLA_HEADS):
        lo = h * MLA_QK_PAD
        ka = k_nope[:, lo:lo + LANES] + kr
        kb = k_nope[:, lo + LANES:lo + MLA_QK_PAD]
        ms = (jnp.sum(ka * ka, axis=-1, keepdims=True) + jnp.sum(kb * kb, axis=-1, keepdims=True)) * (1.0 / MLA_QK)
        r = lax.rsqrt(ms + EPS)
        ya = ka * r * g[:, :LANES]
        yb = kb * r * g[:, LANES:]
        k_ref[:, lo:lo + LANES] = _rope_tile(ya, cos, sin, MLA_ROPE // 2, LANES).astype(k_ref.dtype)
        k_ref[:, lo + LANES:lo + MLA_QK_PAD] = yb.astype(k_ref.dtype)


def _pad_last(a, width):
    return jnp.pad(a, [(0, 0)] * (a.ndim - 1) + [(0, width - a.shape[-1])])


def _mla_prep(proj, col, g_cq, w_uq, g_ckv, w_ukv, g_q, g_k, cos, sin, batch):
    t = proj.shape[0]
    q_rank, kv_rank = w_uq.shape[0], w_ukv.shape[0]
    hp = MLA_HEADS * MLA_QK_PAD
    w_q = _pad_last(w_uq.reshape(q_rank, MLA_HEADS, MLA_QK), MLA_QK_PAD).reshape(q_rank, hp).astype(BF16)
    w_kv = w_ukv.reshape(kv_rank, MLA_HEADS, MLA_NOPE + MLA_V)
    w_k = jnp.pad(w_kv[..., :MLA_NOPE], ((0, 0), (0, 0), (MLA_ROPE, MLA_QK_PAD - MLA_QK)))
    w_k = w_k.reshape(kv_rank, hp).astype(BF16)
    w_v = w_kv[..., MLA_NOPE:].reshape(kv_rank, MLA_HEADS * MLA_V).astype(BF16)
    g_qp = _pad_last(g_q, MLA_QK_PAD).reshape(1, MLA_QK_PAD)
    g_kp = _pad_last(g_k, MLA_QK_PAD).reshape(1, MLA_QK_PAD)

    bm = _block(t, 512)
    rows = lambda w, c: pl.BlockSpec((bm, w), lambda i, c=c: (i, c))
    full = lambda a: pl.BlockSpec(a.shape, lambda i: (0,) * a.ndim)
    table = pl.BlockSpec((1, bm, LANES), lambda i: (0, i, 0))
    g_cq2, g_ckv2 = g_cq.reshape(1, -1), g_ckv.reshape(1, -1)

    q = pl.pallas_call(
        _mla_q_kernel,
        grid=(t // bm,),
        in_specs=[rows(q_rank, col["c_q"] // q_rank), full(g_cq2), full(w_q), full(g_qp), table, table],
        out_specs=pl.BlockSpec((bm, hp), lambda i: (i, 0)),
        out_shape=jax.ShapeDtypeStruct((t, hp), BF16),
        compiler_params=_params("parallel"),
        name="mla_q_prep",
    )(proj, g_cq2, w_q, g_qp, cos, sin)

    k, v = pl.pallas_call(
        _mla_kv_kernel,
        grid=(t // bm,),
        in_specs=[rows(kv_rank, col["c_kv"] // kv_rank), rows(LANES, col["k_r"] // LANES), full(g_ckv2),
                  full(w_k), full(w_v), full(g_kp), table, table],
        out_specs=[pl.BlockSpec((bm, hp), lambda i: (i, 0)),
                   _heads_transposed_spec(MLA_HEADS, MLA_V, bm, t // batch)],
        out_shape=[jax.ShapeDtypeStruct((t, hp), BF16),
                   jax.ShapeDtypeStruct((batch, MLA_HEADS, MLA_V, t // batch), BF16)],
        compiler_params=_params("parallel"),
        name="mla_kv_prep",
    )(proj, proj, g_ckv2, w_k, w_v, g_kp, cos, sin)
    return q, k, v


ATTN_HEADS_PER_STEP = 3


def _attn_scratch(tq, dk, dv):
    per_head = [pltpu.VMEM((1, tq), F32), pltpu.VMEM((1, tq), F32), pltpu.VMEM((dv, tq), F32)]
    per_parity = [pltpu.VMEM((tq, tq), F32), pltpu.VMEM((1, tq), F32),
                  pltpu.VMEM((tq, tq), BF16), pltpu.VMEM((1, tq), F32)]
    return (per_head + 2 * per_parity) * ATTN_HEADS_PER_STEP + [pltpu.VMEM((ATTN_HEADS_PER_STEP, dk, tq), BF16)]


def _attn_kernel(q_ref, k_ref, vt_ref, *rest, tq, dk, dv, selected_keys):
    n_scratch = 11 * ATTN_HEADS_PER_STEP + 1
    scratch = rest[-n_scratch:]
    qt_ref = scratch[-1]
    sel_ref = rest[0] if selected_keys else None
    o_ref = rest[-n_scratch - 1]
    i = pl.program_id(2)
    heads = range(ATTN_HEADS_PER_STEP)
    m_of, l_of, acc_of, s_of, mx_of, p_of, al_of = [], [], [], [], [], [], []
    for h in heads:
        refs = scratch[11 * h:11 * (h + 1)]
        m_of.append(refs[0]); l_of.append(refs[1]); acc_of.append(refs[2])
        s_of.append((refs[3], refs[7])); mx_of.append((refs[4], refs[8]))
        p_of.append((refs[5], refs[9])); al_of.append((refs[6], refs[10]))
    for h in heads:
        qt_ref[h] = q_ref[:, h * dk:(h + 1) * dk].T

    for h in heads:
        m_of[h][...] = jnp.full((1, tq), NEG_INF, F32)
        l_of[h][...] = jnp.zeros((1, tq), F32)
        acc_of[h][...] = jnp.zeros((dv, tq), F32)
        s_of[h][1][...] = jnp.full((tq, tq), NEG_INF, F32)
        mx_of[h][1][...] = jnp.full((1, tq), NEG_INF, F32)
        p_of[h][0][...] = jnp.zeros((tq, tq), BF16)
        al_of[h][0][...] = jnp.zeros((1, tq), F32)

    def scores(block, par, diagonal):
        off = pl.multiple_of(block * tq, tq)
        if selected_keys:
            visible = sel_ref[pl.ds(off, tq), :].astype(F32) > 0.0
        elif diagonal:
            key_chunk = _chunk_of(lax.broadcasted_iota(jnp.int32, (tq, tq), 0))
            query_chunk = _chunk_of(lax.broadcasted_iota(jnp.int32, (tq, tq), 1))
            visible = key_chunk <= query_chunk
        else:
            visible = None
        for h in heads:
            s = _dot(k_ref[pl.ds(off, tq), h * dk:(h + 1) * dk], qt_ref[h])
            if visible is not None:
                s = jnp.where(visible, s, NEG_INF)
            s_of[h][par][...] = s
            mx_of[h][par][...] = jnp.max(s, axis=0, keepdims=True)

    def softmax(par):
        for h in heads:
            m_prev = m_of[h][...]
            m_new = jnp.maximum(m_prev, mx_of[h][par][...])
            m_safe = jnp.where(m_new == NEG_INF, 0.0, m_new)
            alpha = jnp.exp2(m_prev - m_safe)
            p = jnp.exp2(s_of[h][par][...] - m_safe)
            l_of[h][...] = alpha * l_of[h][...] + jnp.sum(p, axis=0, keepdims=True)
            p_of[h][par][...] = p.astype(BF16)
            al_of[h][par][...] = alpha
            m_of[h][...] = m_new

    def values(block, par):
        off = pl.multiple_of(jnp.maximum(block, 0) * tq, tq)
        for h in heads:
            acc_of[h][...] = al_of[h][par][...] * acc_of[h][...] + _dot(vt_ref[h, :, pl.ds(off, tq)],
                                                                         p_of[h][par][...])

    def step(t, par, diagonal=False):
        values(t - 2, par)
        softmax(1 - par)
        scores(t, par, diagonal)

    def pair(u, carry):
        step(2 * u, 0)
        step(2 * u + 1, 1)
        return carry

    lax.fori_loop(0, lax.shift_right_logical(i, 1), pair, 0)

    def drain(par):
        step(i, par, diagonal=True)
        values(i - 1, 1 - par)
        softmax(par)
        values(i, par)

    odd = (i & 1) == 1

    @pl.when(odd)
    def _():
        step(i - 1, 0)
        drain(1)

    @pl.when(jnp.logical_not(odd))
    def _():
        drain(0)

    for h in heads:
        o = acc_of[h][...] / l_of[h][...]
        o_ref[:, h * dv:(h + 1) * dv] = o.T.astype(o_ref.dtype)


def _attention(q, k, vt, selected_t, batch, heads, dk, dv, name, tq=256):
    t = q.shape[0]
    s = t // batch
    tq = _block(s, tq)
    nq = s // tq
    hp = ATTN_HEADS_PER_STEP
    in_specs = [pl.BlockSpec((tq, hp * dk), lambda b, h, i: (b * nq + i, h)),
                pl.BlockSpec((s, hp * dk), lambda b, h, i: (b, h)),
                pl.BlockSpec((None, hp, dv, s), lambda b, h, i: (b, h, 0, 0))]
    operands = [q, k, vt]
    if selected_t is not None:
        in_specs.append(pl.BlockSpec((None, s, tq), lambda b, h, i: (b, 0, i)))
        operands.append(selected_t)
    return pl.pallas_call(
        functools.partial(_attn_kernel, tq=tq, dk=dk, dv=dv, selected_keys=selected_t is not None),
        grid=(batch, heads // hp, nq),
        in_specs=in_specs,
        out_specs=pl.BlockSpec((tq, hp * dv), lambda b, h, i: (b * nq + i, h)),
        out_shape=jax.ShapeDtypeStruct((t, heads * dv), BF16),
        scratch_shapes=_attn_scratch(tq, dk, dv),
        compiler_params=_params("parallel", "parallel", "parallel"),
        name=name,
    )(*operands)


def _heads_transposed_spec(heads, hd, bm, seq):
    blocks_per_batch = seq // bm
    return pl.BlockSpec((None, heads, hd, bm), lambda i: (i // blocks_per_batch, 0, 0, i % blocks_per_batch))


def _heads_transposed_kernel(x_ref, vt_ref):
    _store_heads_transposed(x_ref[...], vt_ref)


def _heads_transposed(proj, col_block, batch, heads, hd):
    t = proj.shape[0]
    bm = _block(t // batch, 512)
    return pl.pallas_call(
        _heads_transposed_kernel,
        grid=(t // bm,),
        in_specs=[pl.BlockSpec((bm, heads * hd), lambda i: (i, col_block))],
        out_specs=_heads_transposed_spec(heads, hd, bm, t // batch),
        out_shape=jax.ShapeDtypeStruct((batch, heads, hd, t // batch), BF16),
        compiler_params=_params("parallel"),
        name="heads_transposed",
    )(proj)


def _head_norm_rope_kernel(x_ref, g_ref, cos_ref, sin_ref, o_ref, *, heads, premultiplier):
    cos, sin = cos_ref[0], sin_ref[0]
    g = g_ref[...] * premultiplier
    for h in range(heads):
        x = x_ref[:, h * LANES:(h + 1) * LANES]
        y = x * _rms_scale(x, LANES) * g
        o_ref[:, h * LANES:(h + 1) * LANES] = _rope_tile(y, cos, sin, DSA_ROT // 2, LANES).astype(o_ref.dtype)


def _head_norm_rope(proj, col_block, g, cos, sin, premultiplier=1.0):
    t = proj.shape[0]
    width = DSA_HEADS * DSA_HEAD_DIM
    bm = _block(t, 512)
    table = pl.BlockSpec((1, bm, LANES), lambda i: (1, i, 0))
    return pl.pallas_call(
        functools.partial(_head_norm_rope_kernel, heads=DSA_HEADS, premultiplier=premultiplier),
        grid=(t // bm,),
        in_specs=[pl.BlockSpec((bm, width), lambda i: (i, col_block)),
                  pl.BlockSpec((1, LANES), lambda i: (0, 0)), table, table],
        out_specs=pl.BlockSpec((bm, width), lambda i: (i, 0)),
        out_shape=jax.ShapeDtypeStruct((t, width), BF16),
        compiler_params=_params("parallel"),
        name="head_norm_rope",
    )(proj, g.reshape(1, LANES), cos, sin)


def _rope_only_kernel(x_ref, cos_ref, sin_ref, o_ref):
    cos, sin = cos_ref[0], sin_ref[0]
    for c in range(x_ref.shape[1] // LANES):
        x = x_ref[:, c * LANES:(c + 1) * LANES]
        o_ref[:, c * LANES:(c + 1) * LANES] = _rope_tile(x, cos, sin, IDX_ROT // 2, IDX_DIM).astype(o_ref.dtype)


def _rope_only(proj, width, col_block, cos, sin):
    t = proj.shape[0]
    bm = _block(t, 512)
    table = pl.BlockSpec((1, bm, LANES), lambda i: (2, i, 0))
    return pl.pallas_call(
        _rope_only_kernel,
        grid=(t // bm,),
        in_specs=[pl.BlockSpec((bm, width), lambda i: (i, col_block)), table, table],
        out_specs=pl.BlockSpec((bm, width), lambda i: (i, 0)),
        out_shape=jax.ShapeDtypeStruct((t, width), BF16),
        compiler_params=_params("parallel"),
        name="indexer_rope",
    )(proj, cos, sin)


SCORE_COLS = 256
COUNT_ACC_ROWS = 64


def _sortable_key(x):
    bits = lax.bitcast_convert_type(x, jnp.int32)
    return bits ^ ((bits >> 31) & 0x7FFFFFFF)


def _indexer_kernel(qi_ref, kit_ref, w_ref, *rest, tq, tk, topk, seq, w_scale, n_casts):
    cast_src, sel_ref, cast_dst = rest[:n_casts], rest[n_casts], rest[n_casts + 1:2 * n_casts + 1]
    key_ref, wb_ref, qh_ref = rest[2 * n_casts + 1:]
    _cast_in_kernel(cast_src, cast_dst)

    i = pl.program_id(1)
    n_tiles = pl.cdiv((i + 1) * tq, tk)
    row_chunk = _chunk_of(i * tq + lax.broadcasted_iota(jnp.int32, (tq, 1), 0))

    w = w_ref[...] * w_scale
    for h in range(IDX_HEADS):
        wb_ref[h] = jnp.broadcast_to(w[:, h:h + 1], (tq, LANES))
        qh_ref[h] = qi_ref[:, h * IDX_DIM:(h + 1) * IDX_DIM]

    def score_tile(j, carry):
        for part in range(tk // SCORE_COLS):
            off = pl.multiple_of(j * tk + part * SCORE_COLS, SCORE_COLS)
            kt = kit_ref[:, pl.ds(off, SCORE_COLS)]
            acc = [jnp.zeros((tq, LANES), F32) for _ in range(SCORE_COLS // LANES)]
            for h in range(IDX_HEADS):
                logits = _dot(qh_ref[h], kt)
                wh = wb_ref[h]
                for c in range(SCORE_COLS // LANES):
                    acc[c] = acc[c] + jnp.maximum(logits[:, c * LANES:(c + 1) * LANES], 0.0) * wh
            for c in range(SCORE_COLS // LANES):
                col = off + c * LANES + lax.broadcasted_iota(jnp.int32, (1, LANES), 1)
                score = jnp.where(_chunk_of(col) <= row_chunk, acc[c], NEG_INF)
                key_ref[pl.ds(off + c * LANES, LANES), :] = _sortable_key(score.T)
        return carry

    lax.fori_loop(0, n_tiles, score_tile, 0)

    def count(pred):
        rows = min(tk, COUNT_ACC_ROWS)

        def tile(j, acc):
            off = pl.multiple_of(j * tk, tk)
            hit = jnp.where(pred(key_ref[pl.ds(off, tk), :], off), 1.0, 0.0)
            return acc + jnp.sum(hit.reshape(tk // rows, rows, tq), axis=0)
        acc = lax.fori_loop(0, n_tiles, tile, jnp.zeros((rows, tq), F32))
        return jnp.sum(acc, axis=0, keepdims=True)

    def value_bit(b, lo):
        cand = lo + lax.shift_left(jnp.int32(1), 31 - b)
        n_ge = count(lambda kt, _: kt >= cand)
        return jnp.where(n_ge >= topk, cand, lo)

    thr = lax.fori_loop(0, 32, value_bit, jnp.full((1, tq), INT_MIN, jnp.int32))
    n_ge = count(lambda kt, _: kt >= thr)
    surplus = jnp.logical_and(n_ge > topk, thr > NEG_INF_KEY)

    def write_tile(j, carry):
        off = pl.multiple_of(j * tk, tk)
        kt = key_ref[pl.ds(off, tk), :]
        chosen = jnp.logical_and(kt >= thr, kt > NEG_INF_KEY)
        sel_ref[pl.ds(off, tk), :] = jnp.where(chosen, 1.0, 0.0).astype(sel_ref.dtype)
        return carry

    lax.fori_loop(0, n_tiles, write_tile, 0)

    def clear_tile(j, carry):
        off = pl.multiple_of(j * tk, tk)
        sel_ref[pl.ds(off, tk), :] = jnp.zeros((tk, tq), sel_ref.dtype)
        return carry

    lax.fori_loop(n_tiles, seq // tk, clear_tile, 0)

    @pl.when(jnp.max(jnp.where(surplus, 1.0, 0.0)) > 0.0)
    def _():
        n_gt = count(lambda kt, _: kt > thr)
        need = topk - n_gt

        def position(first_key):
            return first_key + lax.broadcasted_iota(jnp.int32, (tk, tq), 0)

        def position_bit(b, base):
            cand = base + lax.shift_left(jnp.int32(1), (seq - 1).bit_length() - 1 - b)
            n_before = count(lambda kt, k0: jnp.logical_and(kt == thr, position(k0) < cand))
            return jnp.where(n_before < need, cand, base)

        last = lax.fori_loop(0, (seq - 1).bit_length(), position_bit, jnp.zeros((1, tq), jnp.int32))

        def rewrite_tile(j, carry):
            off = pl.multiple_of(j * tk, tk)
            kt = key_ref[pl.ds(off, tk), :]
            chosen = jnp.logical_or(kt > thr, jnp.logical_and(kt == thr, position(off) <= last))
            chosen = jnp.logical_and(chosen, kt > NEG_INF_KEY)
            sel_ref[pl.ds(off, tk), :] = jnp.where(chosen, 1.0, 0.0).astype(sel_ref.dtype)
            return carry

        lax.fori_loop(0, n_tiles, rewrite_tile, 0)


def _indexer_select(qi, kit, w, w_col_block, batch, topk, weights_to_cast=(), tq=128, tk=512):
    s = kit.shape[2]
    tq = _block(s, tq)
    tk = _block(s, tk)
    assert tk % SCORE_COLS == 0
    assert tq == LANES, "score tiles are transposed one (LANES, LANES) tile at a time"
    nq = s // tq
    steps = batch * nq
    w_scale = (IDX_DIM ** -0.5) * (IDX_HEADS ** -0.5)

    slabs = _cast_slabs(weights_to_cast)
    slab_spec = lambda a: _cast_slab_spec(a, steps, lambda b, i: b * nq + i)

    outs = pl.pallas_call(
        functools.partial(_indexer_kernel, tq=tq, tk=tk, topk=topk, seq=s, w_scale=w_scale, n_casts=len(slabs)),
        grid=(batch, nq),
        in_specs=[pl.BlockSpec((tq, IDX_HEADS * IDX_DIM), lambda b, i: (b * nq + i, 0)),
                  pl.BlockSpec((None, IDX_DIM, s), lambda b, i: (b, 0, 0)),
                  pl.BlockSpec((tq, LANES), lambda b, i: (b * nq + i, w_col_block))]
                 + [slab_spec(a) for a in slabs],
        out_specs=[pl.BlockSpec((None, s, tq), lambda b, i: (b, 0, i))] + [slab_spec(a) for a in slabs],
        out_shape=[jax.ShapeDtypeStruct((batch, s, s), BF16)]
                  + [jax.ShapeDtypeStruct(a.shape, BF16) for a in slabs],
        scratch_shapes=[pltpu.VMEM((s, tq), jnp.int32), pltpu.VMEM((IDX_HEADS, tq, LANES), F32),
                        pltpu.VMEM((IDX_HEADS, tq, IDX_DIM), BF16)],
        compiler_params=_params("arbitrary", "arbitrary"),
        name="indexer_select",
    )(qi, kit, w, *slabs)
    return outs[0], tuple(o.reshape(a.shape) for o, a in zip(outs[1:], weights_to_cast))


def _memory_attn_kernel(q_ref, kv_ref, gq_ref, gk_ref, o_ref, *, scale):
    hd = MEM_HEAD_DIM
    gq, gk = gq_ref[...], gk_ref[...]
    for h in range(MEM_HEADS):
        q = q_ref[:, h * hd:(h + 1) * hd]
        k = kv_ref[:, h * hd:(h + 1) * hd]
        v = kv_ref[:, (MEM_HEADS + h) * hd:(MEM_HEADS + h + 1) * hd].astype(BF16)
        qn = (q * _rms_scale(q, hd) * gq).astype(BF16)
        kn = (k * _rms_scale(k, hd) * gk).astype(BF16)
        s = _dot_nt(qn, kn) * scale
        e = jnp.exp(s - jnp.max(s, axis=-1, keepdims=True))
        p = e / jnp.sum(e, axis=-1, keepdims=True)
        o_ref[:, h * hd:(h + 1) * hd] = _dot(p.astype(BF16), v).astype(o_ref.dtype)


def _memory_attention(proj, col_block, kv, g_q, g_k, batch, tq=512):
    t = proj.shape[0]
    s = t // batch
    m = kv.shape[0] // batch
    width = MEM_HEADS * MEM_HEAD_DIM
    tq = _block(s, tq)
    nq = s // tq
    g_spec = pl.BlockSpec((1, MEM_HEAD_DIM), lambda b, i: (0, 0))
    return pl.pallas_call(
        functools.partial(_memory_attn_kernel, scale=MEM_HEAD_DIM ** -0.5),
        grid=(batch, nq),
        in_specs=[pl.BlockSpec((tq, width), lambda b, i: (b * nq + i, col_block)),
                  pl.BlockSpec((m, 2 * width), lambda b, i: (b, 0)), g_spec, g_spec],
        out_specs=pl.BlockSpec((tq, width), lambda b, i: (b * nq + i, 0)),
        out_shape=jax.ShapeDtypeStruct((t, width), BF16),
        compiler_params=_params("parallel", "parallel"),
        name="memory_attention",
    )(proj, kv, g_q.reshape(1, -1), g_k.reshape(1, -1))


def _reordered_input_projection(w_in, q_rank, kv_rank):
    dsa = DSA_HEADS * DSA_HEAD_DIM
    assert q_rank == dsa, "the reordered projection layout assumes equal widths for these groups"
    sizes = [("c_q", q_rank), ("c_kv", kv_rank), ("k_r", MLA_ROPE), ("q_d", dsa), ("k_d", dsa), ("v_d", dsa),
             ("q_i", IDX_HEADS * IDX_DIM), ("k_i", IDX_DIM), ("w_i", IDX_HEADS), ("q_m", MEM_HEADS * MEM_HEAD_DIM)]
    src, start = {}, 0
    for name, n in sizes:
        src[name] = (start, n)
        start += n
    assert start == w_in.shape[1]
    take = lambda name: w_in[:, src[name][0]:src[name][0] + src[name][1]]
    pieces = [("c_q", take("c_q")), ("q_d", take("q_d")), ("k_d", take("k_d")), ("v_d", take("v_d")),
              ("q_i", take("q_i")), ("q_m", take("q_m")), ("c_kv", take("c_kv")),
              ("k_r", jnp.concatenate([take("k_r"), take("k_i")], axis=1)),
              ("w_i", _pad_last(take("w_i"), LANES))]
    col, start = {}, 0
    for name, piece in pieces:
        assert start % piece.shape[1] == 0, name
        col[name] = start
        start += piece.shape[1]
    return jnp.concatenate([p for _, p in pieces], axis=1), col


def _hybrid_mixer(x, h, mem, positions, w_in_r, col, g_cq, w_uq, g_ckv, w_ukv, g_q_mla, g_k_mla, g_q_dsa, g_k_dsa,
                  g_mem, w_mem_kv, g_q_mem, g_k_mem, w_o_mla, w_o_dsa, w_o_mem, w_gate, w_out, batch, later_weights):
    t, d = h.shape
    s = t // batch
    dsa = DSA_HEADS * DSA_HEAD_DIM
    proj = _matmul(h, w_in_r, F32, bm=1024, bn=768, name="input_projection")
    cos, sin = _rope_tables(positions)

    q_a, k_a, vt_a = _mla_prep(proj, col, g_cq, w_uq, g_ckv, w_ukv, g_q_mla, g_k_mla, cos, sin, batch)
    y_mla = _attention(q_a, k_a, vt_a, None, batch, MLA_HEADS,
                       MLA_QK_PAD, MLA_V, "chunk_causal_attention")

    q_b = _head_norm_rope(proj, col["q_d"] // dsa, g_q_dsa, cos, sin, _query_premultiplier(DSA_HEAD_DIM))
    k_b = _head_norm_rope(proj, col["k_d"] // dsa, g_k_dsa, cos, sin)
    vt_b = _heads_transposed(proj, col["v_d"] // dsa, batch, DSA_HEADS, DSA_HEAD_DIM)
    qi_width = IDX_HEADS * IDX_DIM
    qi = _rope_only(proj, qi_width, col["q_i"] // qi_width, cos, sin)
    ki = _rope_only(proj, LANES, col["k_r"] // LANES, cos, sin)[:, IDX_DIM:]
    kit = ki.reshape(batch, s, IDX_DIM).transpose(0, 2, 1)
    topk = min(DSA_MAX_TOPK, s // 4)
    own_weights = (w_gate, w_o_mla, w_o_dsa, w_o_mem, w_out)
    selected_t, cast = _indexer_select(qi, kit, proj, col["w_i"] // LANES, batch, topk,
                                       own_weights + tuple(later_weights))
    w_gate, w_o_mla, w_o_dsa, w_o_mem, w_out = cast[:len(own_weights)]
    y_dsa = _attention(q_b, k_b, vt_b, selected_t, batch, DSA_HEADS,
                       DSA_HEAD_DIM, DSA_HEAD_DIM, "masked_attention")

    m_len = mem.shape[1]
    mem_n = _rmsnorm(mem.reshape(batch * m_len, d), g_mem)
    kv_mem = _matmul(mem_n, w_mem_kv.astype(BF16), F32, bm=512, bn=512, name="memory_kv")
    y_mem = _memory_attention(proj, col["q_m"] // (MEM_HEADS * MEM_HEAD_DIM), kv_mem, g_q_mem, g_k_mem, batch)

    gated = _gate_combine(h, (y_mla, y_dsa, y_mem), w_gate, (w_o_mla, w_o_dsa, w_o_mem))
    return (_matmul_residual(gated, w_out, x, 1.0, bm=1024, name="mixer_out"), cast[len(own_weights):])


def kernel(x, mem, positions, g_ff1, w_ff1_gate, w_ff1_up, w_ff1_down, g_mix, w_in, g_cq, w_uq, g_ckv, w_ukv, g_q_mla, g_k_mla, g_q_dsa, g_k_dsa, g_mem, w_mem_kv, g_q_mem, g_k_mem, w_o_mla, w_o_dsa, w_o_mem, w_gate, w_out, g_ff2, w_ff2_gate, w_ff2_up, w_ff2_down):
    batch, seq, d = x.shape
    xt = x.reshape(batch * seq, d)
    for l in range(g_ff1.shape[0]):
        xt, (w_in_l,) = _ffn_half_step(xt, g_ff1[l], w_ff1_gate[l], w_ff1_up[l], w_ff1_down[l], (w_in[l],))
        w_in_r, col = _reordered_input_projection(w_in_l, w_uq.shape[1], w_ukv.shape[1])
        h = _rmsnorm(xt, g_mix[l])
        xt, ff2 = _hybrid_mixer(xt, h, mem, positions, w_in_r, col, g_cq[l], w_uq[l], g_ckv[l], w_ukv[l],
                                g_q_mla[l], g_k_mla[l], g_q_dsa[l], g_k_dsa[l], g_mem[l], w_mem_kv[l],
                                g_q_mem[l], g_k_mem[l], w_o_mla[l], w_o_dsa[l], w_o_mem[l], w_gate[l], w_out[l],
                                batch, (w_ff2_gate[l], w_ff2_up[l], w_ff2_down[l]))
        xt, _ = _ffn_half_step(xt, g_ff2[l], *ff2)
    return xt.reshape(batch, seq, d)
```

```python
import functools

import jax
import jax.numpy as jnp
from jax import lax
from jax.experimental import pallas as pl
from jax.experimental.pallas import tpu as pltpu

CHUNK = 64
ROPE_THETA = 500000.0
EPS = 1e-6

MLA_HEADS = 12
MLA_NOPE = 128
MLA_ROPE = 64
MLA_V = 128
MLA_QK = MLA_ROPE + MLA_NOPE
MLA_QK_PAD = 256

DSA_HEADS = 12
DSA_HEAD_DIM = 128
DSA_ROT = DSA_HEAD_DIM // 4
DSA_MAX_TOPK = 256
IDX_HEADS = 32
IDX_DIM = 64
IDX_ROT = IDX_DIM // 4

MEM_HEADS = 4
MEM_HEAD_DIM = 256

LANES = 128
SUBLANES = 8
BF16_SUBLANES = 16
VMEM_LIMIT_BYTES = 56 * 1024 * 1024

F32 = jnp.float32
BF16 = jnp.bfloat16
NEG_INF = float("-inf")
INT_MIN = -2 ** 31
LOG2_E = 1.4426950408889634
NEG_INF_KEY = (0xFF800000 - (1 << 32)) ^ 0x7FFFFFFF


def _params(*semantics):
    return pltpu.CompilerParams(dimension_semantics=semantics, vmem_limit_bytes=VMEM_LIMIT_BYTES)


def _block(dim, preferred):
    if dim <= preferred:
        return dim
    b = preferred - preferred % LANES
    while b >= LANES:
        if dim % b == 0:
            return b
        b -= LANES
    return dim


def _rms_scale(x, width):
    return lax.rsqrt(jnp.sum(x * x, axis=-1, keepdims=True) * (1.0 / width) + EPS)


def _dot(a, b):
    return jnp.dot(a, b, preferred_element_type=F32)


def _dot_nt(a, b):
    return lax.dot_general(a, b, (((1,), (1,)), ((), ())), preferred_element_type=F32)


def _sigmoid(x):
    return 1.0 / (1.0 + jnp.exp(-x))


def _query_premultiplier(head_dim):
    return head_dim ** -0.5 * LOG2_E


def _chunk_of(frame):
    return lax.shift_right_logical(frame, CHUNK.bit_length() - 1)


def _rmsnorm_kernel(x_ref, g_ref, o_ref):
    x = x_ref[...]
    o_ref[...] = (x * _rms_scale(x, x.shape[-1]) * g_ref[...]).astype(o_ref.dtype)


def _rmsnorm(x, g, out_dtype=BF16):
    m, d = x.shape
    bm = _block(m, 512)
    return pl.pallas_call(
        _rmsnorm_kernel,
        grid=(m // bm,),
        in_specs=[pl.BlockSpec((bm, d), lambda i: (i, 0)), pl.BlockSpec((1, d), lambda i: (0, 0))],
        out_specs=pl.BlockSpec((bm, d), lambda i: (i, 0)),
        out_shape=jax.ShapeDtypeStruct((m, d), out_dtype),
        compiler_params=_params("parallel"),
        name="rmsnorm",
    )(x, g.reshape(1, d))


def _matmul_kernel(a_ref, b_ref, o_ref):
    o_ref[...] = _dot(a_ref[...], b_ref[...]).astype(o_ref.dtype)


def _matmul(a, b, out_dtype, bm=1024, bn=512, name="matmul"):
    m, k = a.shape
    n = b.shape[1]
    bm, bn = _block(m, bm), _block(n, bn)
    return pl.pallas_call(
        _matmul_kernel,
        grid=(m // bm, n // bn),
        in_specs=[pl.BlockSpec((bm, k), lambda i, j: (i, 0)), pl.BlockSpec((k, bn), lambda i, j: (0, j))],
        out_specs=pl.BlockSpec((bm, bn), lambda i, j: (i, j)),
        out_shape=jax.ShapeDtypeStruct((m, n), out_dtype),
        compiler_params=_params("parallel", "parallel"),
        name=name,
    )(a, b)


def _cast_slabs(weights):
    return [a.reshape(-1, a.shape[-1]) for a in weights]


def _cast_slab_spec(a, steps, step_of):
    rows = a.shape[0]
    r = next(r for r in range(BF16_SUBLANES, rows + 1, BF16_SUBLANES) if rows % r == 0 and rows // r <= steps)
    last = rows // r - 1
    return pl.BlockSpec((r, a.shape[1]), lambda *g: (jnp.minimum(step_of(*g), last), 0))


def _cast_in_kernel(src_refs, dst_refs):
    for src, dst in zip(src_refs, dst_refs):
        dst[...] = src[...].astype(dst.dtype)


def _swiglu_up_kernel(a_ref, wg_ref, wu_ref, o_ref):
    a = a_ref[...]
    g = _dot(a, wg_ref[...])
    u = _dot(a, wu_ref[...])
    o_ref[...] = (g * _sigmoid(g) * u).astype(o_ref.dtype)


def _swiglu_up(a, wg, wu, bm=1024, bn=512):
    m, k = a.shape
    n = wg.shape[1]
    bm, bn = _block(m, bm), _block(n, bn)
    return pl.pallas_call(
        _swiglu_up_kernel,
        grid=(m // bm, n // bn),
        in_specs=[pl.BlockSpec((bm, k), lambda i, j: (i, 0)),
                  pl.BlockSpec((k, bn), lambda i, j: (0, j)),
                  pl.BlockSpec((k, bn), lambda i, j: (0, j))],
        out_specs=pl.BlockSpec((bm, bn), lambda i, j: (i, j)),
        out_shape=jax.ShapeDtypeStruct((m, n), BF16),
        compiler_params=_params("parallel", "parallel"),
        name="swiglu_up",
    )(a, wg, wu)


def _matmul_residual_kernel(a_ref, b_ref, x_ref, o_ref, *, scale):
    o_ref[...] = x_ref[...] + scale * _dot(a_ref[...], b_ref[...])


def _matmul_residual(a, b, x, scale, bm=512, bn=512, name="matmul_residual"):
    m, k = a.shape
    n = b.shape[1]
    bm, bn = _block(m, bm), _block(n, bn)
    return pl.pallas_call(
        functools.partial(_matmul_residual_kernel, scale=scale),
        grid=(m // bm, n // bn),
        in_specs=[pl.BlockSpec((bm, k), lambda i, j: (i, 0)),
                  pl.BlockSpec((k, bn), lambda i, j: (0, j)),
                  pl.BlockSpec((bm, bn), lambda i, j: (i, j))],
        out_specs=pl.BlockSpec((bm, bn), lambda i, j: (i, j)),
        out_shape=jax.ShapeDtypeStruct((m, n), F32),
        compiler_params=_params("parallel", "parallel"),
        name=name,
    )(a, b, x)


def _ffn_half_step(x, g, w_gate, w_up, w_down):
    h = _rmsnorm(x, g)
    a = _swiglu_up(h, w_gate.astype(BF16), w_up.astype(BF16))
    return _matmul_residual(a, w_down.astype(BF16), x, 0.5, name="ffn_down")


def _gate_combine_kernel(h_ref, ya_ref, yb_ref, yc_ref, wg_ref, wa_ref, wb_ref, wc_ref, o_ref):
    h = h_ref[...]
    out = _sigmoid(_dot(h, wg_ref[0])) * _dot(ya_ref[...], wa_ref[...])
    out = out + _sigmoid(_dot(h, wg_ref[1])) * _dot(yb_ref[...], wb_ref[...])
    out = out + _sigmoid(_dot(h, wg_ref[2])) * _dot(yc_ref[...], wc_ref[...])
    o_ref[...] = out.astype(o_ref.dtype)


def _gate_combine(h, ys, w_gate, w_os, bm=512, bn=512):
    m, d = h.shape
    n = w_gate.shape[-1]
    bm, bn = _block(m, bm), _block(n, bn)
    row = lambda w: pl.BlockSpec((bm, w), lambda i, j: (i, 0))
    col = lambda k: pl.BlockSpec((k, bn), lambda i, j: (0, j))
    return pl.pallas_call(
        _gate_combine_kernel,
        grid=(m // bm, n // bn),
        in_specs=[row(d)] + [row(y.shape[1]) for y in ys]
                 + [pl.BlockSpec((3, d, bn), lambda i, j: (0, 0, j))] + [col(w.shape[0]) for w in w_os],
        out_specs=pl.BlockSpec((bm, bn), lambda i, j: (i, j)),
        out_shape=jax.ShapeDtypeStruct((m, n), BF16),
        compiler_params=_params("parallel", "parallel"),
        name="gate_combine",
    )(h, *ys, w_gate, *w_os)


def _rope_tables_kernel(pos_ref, invf_ref, sgn_ref, cos_ref, sin_ref):
    pos = pos_ref[...].astype(F32)
    for t in range(invf_ref.shape[0]):
        ang = pos * invf_ref[t:t + 1, :]
        cos_ref[t] = jnp.cos(ang)
        sin_ref[t] = jnp.sin(ang) * sgn_ref[t:t + 1, :]


def _rope_lane_tables(rot, period):
    half = rot // 2
    inv_freq = ROPE_THETA ** (-jnp.arange(half, dtype=F32) / half)
    lane = jnp.arange(LANES) % period
    invf = jnp.where(lane < rot, inv_freq[lane % half], 0.0)
    sgn = jnp.where(lane < half, -1.0, jnp.where(lane < rot, 1.0, 0.0))
    return invf.astype(F32), sgn.astype(F32)


def _rope_tables(positions):
    t = positions.size
    specs = [_rope_lane_tables(MLA_ROPE, LANES), _rope_lane_tables(DSA_ROT, LANES),
             _rope_lane_tables(IDX_ROT, IDX_DIM)]
    invf = jnp.stack([s[0] for s in specs])
    sgn = jnp.stack([s[1] for s in specs])
    bm = _block(t, 1024)
    n = invf.shape[0]
    const = pl.BlockSpec((n, LANES), lambda i: (0, 0))
    out = pl.BlockSpec((n, bm, LANES), lambda i: (0, i, 0))
    return pl.pallas_call(
        _rope_tables_kernel,
        grid=(t // bm,),
        in_specs=[pl.BlockSpec((bm, 1), lambda i: (i, 0)), const, const],
        out_specs=[out, out],
        out_shape=[jax.ShapeDtypeStruct((n, t, LANES), F32)] * 2,
        compiler_params=_params("parallel"),
        name="rope_tables",
    )(positions.reshape(t, 1), invf, sgn)


def _rope_tile(x, cos, sin, half, period):
    lane = lax.broadcasted_iota(jnp.int32, x.shape, 1) & (period - 1)
    partner = jnp.where(lane < half, pltpu.roll(x, LANES - half, 1), pltpu.roll(x, half, 1))
    return x * cos + partner * sin


def _mla_q_kernel(cq_ref, gcq_ref, w_ref, gq_ref, cos_ref, sin_ref, o_ref):
    x = cq_ref[...]
    xn = (x * _rms_scale(x, x.shape[-1]) * gcq_ref[...]).astype(BF16)
    q = _dot(xn, w_ref[...])
    cos, sin = cos_ref[0], sin_ref[0]
    g = gq_ref[...] * _query_premultiplier(MLA_QK)
    for h in range(MLA_HEADS):
        lo = h * MLA_QK_PAD
        qh = q[:, lo:lo + MLA_QK_PAD]
        y = qh * _rms_scale(qh, MLA_QK) * g
        o_ref[:, lo:lo + LANES] = _rope_tile(y[:, :LANES], cos, sin, MLA_ROPE // 2, LANES).astype(o_ref.dtype)
        o_ref[:, lo + LANES:lo + MLA_QK_PAD] = y[:, LANES:].astype(o_ref.dtype)


def _store_heads_transposed(x, vt_ref):
    heads, hd, _ = vt_ref.shape
    for h in range(heads):
        vt_ref[h] = x[:, h * hd:(h + 1) * hd].T.astype(vt_ref.dtype)


def _mla_kv_kernel(ckv_ref, kr_ref, gckv_ref, wk_ref, wv_ref, gk_ref, cos_ref, sin_ref, k_ref, vt_ref):
    x = ckv_ref[...]
    xn = (x * _rms_scale(x, x.shape[-1]) * gckv_ref[...]).astype(BF16)
    k_nope = _dot(xn, wk_ref[...])
    _store_heads_transposed(_dot(xn, wv_ref[...]), vt_ref)
    kr = kr_ref[...]
    lane = lax.broadcasted_iota(jnp.int32, kr.shape, 1)
    kr = jnp.where(lane < MLA_ROPE, kr, 0.0)
    cos, sin = cos_ref[0], sin_ref[0]
    g = gk_ref[...]
    for h in range(MLA_HEADS):
        lo = h * MLA_QK_PAD
        ka = k_nope[:, lo:lo + LANES] + kr
        kb = k_nope[:, lo + LANES:lo + MLA_QK_PAD]
        ms = (jnp.sum(ka * ka, axis=-1, keepdims=True) + jnp.sum(kb * kb, axis=-1, keepdims=True)) * (1.0 / MLA_QK)
        r = lax.rsqrt(ms + EPS)
        ya = ka * r * g[:, :LANES]
        yb = kb * r * g[:, LANES:]
        k_ref[:, lo:lo + LANES] = _rope_tile(ya, cos, sin, MLA_ROPE // 2, LANES).astype(k_ref.dtype)
        k_ref[:, lo + LANES:lo + MLA_QK_PAD] = yb.astype(k_ref.dtype)


def _pad_last(a, width):
    return jnp.pad(a, [(0, 0)] * (a.ndim - 1) + [(0, width - a.shape[-1])])


def _mla_prep(proj, col, g_cq, w_uq, g_ckv, w_ukv, g_q, g_k, cos, sin, batch):
    t = proj.shape[0]
    q_rank, kv_rank = w_uq.shape[0], w_ukv.shape[0]
    hp = MLA_HEADS * MLA_QK_PAD
    w_q = _pad_last(w_uq.reshape(q_rank, MLA_HEADS, MLA_QK), MLA_QK_PAD).reshape(q_rank, hp).astype(BF16)
    w_kv = w_ukv.reshape(kv_rank, MLA_HEADS, MLA_NOPE + MLA_V)
    w_k = jnp.pad(w_kv[..., :MLA_NOPE], ((0, 0), (0, 0), (MLA_ROPE, MLA_QK_PAD - MLA_QK)))
    w_k = w_k.reshape(kv_rank, hp).astype(BF16)
    w_v = w_kv[..., MLA_NOPE:].reshape(kv_rank, MLA_HEADS * MLA_V).astype(BF16)
    g_qp = _pad_last(g_q, MLA_QK_PAD).reshape(1, MLA_QK_PAD)
    g_kp = _pad_last(g_k, MLA_QK_PAD).reshape(1, MLA_QK_PAD)

    bm = _block(t, 512)
    rows = lambda w, c: pl.BlockSpec((bm, w), lambda i, c=c: (i, c))
    full = lambda a: pl.BlockSpec(a.shape, lambda i: (0,) * a.ndim)
    table = pl.BlockSpec((1, bm, LANES), lambda i: (0, i, 0))
    g_cq2, g_ckv2 = g_cq.reshape(1, -1), g_ckv.reshape(1, -1)

    q = pl.pallas_call(
        _mla_q_kernel,
        grid=(t // bm,),
        in_specs=[rows(q_rank, col["c_q"] // q_rank), full(g_cq2), full(w_q), full(g_qp), table, table],
        out_specs=pl.BlockSpec((bm, hp), lambda i: (i, 0)),
        out_shape=jax.ShapeDtypeStruct((t, hp), BF16),
        compiler_params=_params("parallel"),
        name="mla_q_prep",
    )(proj, g_cq2, w_q, g_qp, cos, sin)

    k, v = pl.pallas_call(
        _mla_kv_kernel,
        grid=(t // bm,),
        in_specs=[rows(kv_rank, col["c_kv"] // kv_rank), rows(LANES, col["k_r"] // LANES), full(g_ckv2),
                  full(w_k), full(w_v), full(g_kp), table, table],
        out_specs=[pl.BlockSpec((bm, hp), lambda i: (i, 0)),
                   _heads_transposed_spec(MLA_HEADS, MLA_V, bm, t // batch)],
        out_shape=[jax.ShapeDtypeStruct((t, hp), BF16),
                   jax.ShapeDtypeStruct((batch, MLA_HEADS, MLA_V, t // batch), BF16)],
        compiler_params=_params("parallel"),
        name="mla_kv_prep",
    )(proj, proj, g_ckv2, w_k, w_v, g_kp, cos, sin)
    return q, k, v


ATTN_HEADS_PER_STEP = 3


def _attn_scratch(tq, dk, dv):
    per_head = [pltpu.VMEM((1, tq), F32), pltpu.VMEM((1, tq), F32), pltpu.VMEM((dv, tq), F32)]
    per_parity = [pltpu.VMEM((tq, tq), F32), pltpu.VMEM((1, tq), F32),
                  pltpu.VMEM((tq, tq), BF16), pltpu.VMEM((1, tq), F32)]
    return (per_head + 2 * per_parity) * ATTN_HEADS_PER_STEP + [pltpu.VMEM((ATTN_HEADS_PER_STEP, dk, tq), BF16)]


def _attn_kernel(q_ref, k_ref, vt_ref, *rest, tq, dk, dv, selected_keys):
    n_scratch = 11 * ATTN_HEADS_PER_STEP + 1
    scratch = rest[-n_scratch:]
    qt_ref = scratch[-1]
    sel_ref = rest[0] if selected_keys else None
    o_ref = rest[-n_scratch - 1]
    i = pl.program_id(2)
    heads = range(ATTN_HEADS_PER_STEP)
    m_of, l_of, acc_of, s_of, mx_of, p_of, al_of = [], [], [], [], [], [], []
    for h in heads:
        refs = scratch[11 * h:11 * (h + 1)]
        m_of.append(refs[0]); l_of.append(refs[1]); acc_of.append(refs[2])
        s_of.append((refs[3], refs[7])); mx_of.append((refs[4], refs[8]))
        p_of.append((refs[5], refs[9])); al_of.append((refs[6], refs[10]))
    for h in heads:
        qt_ref[h] = q_ref[:, h * dk:(h + 1) * dk].T

    for h in heads:
        m_of[h][...] = jnp.full((1, tq), NEG_INF, F32)
        l_of[h][...] = jnp.zeros((1, tq), F32)
        acc_of[h][...] = jnp.zeros((dv, tq), F32)
        s_of[h][1][...] = jnp.full((tq, tq), NEG_INF, F32)
        mx_of[h][1][...] = jnp.full((1, tq), NEG_INF, F32)
        p_of[h][0][...] = jnp.zeros((tq, tq), BF16)
        al_of[h][0][...] = jnp.zeros((1, tq), F32)

    def scores(block, par, diagonal):
        off = pl.multiple_of(block * tq, tq)
        if selected_keys:
            visible = sel_ref[pl.ds(off, tq), :].astype(F32) > 0.0
        elif diagonal:
            key_chunk = _chunk_of(lax.broadcasted_iota(jnp.int32, (tq, tq), 0))
            query_chunk = _chunk_of(lax.broadcasted_iota(jnp.int32, (tq, tq), 1))
            visible = key_chunk <= query_chunk
        else:
            visible = None
        for h in heads:
            s = _dot(k_ref[pl.ds(off, tq), h * dk:(h + 1) * dk], qt_ref[h])
            if visible is not None:
                s = jnp.where(visible, s, NEG_INF)
            s_of[h][par][...] = s
            mx_of[h][par][...] = jnp.max(s, axis=0, keepdims=True)

    def softmax(par):
        for h in heads:
            m_prev = m_of[h][...]
            m_new = jnp.maximum(m_prev, mx_of[h][par][...])
            m_safe = jnp.where(m_new == NEG_INF, 0.0, m_new)
            alpha = jnp.exp2(m_prev - m_safe)
            p = jnp.exp2(s_of[h][par][...] - m_safe)
            l_of[h][...] = alpha * l_of[h][...] + jnp.sum(p, axis=0, keepdims=True)
            p_of[h][par][...] = p.astype(BF16)
            al_of[h][par][...] = alpha
            m_of[h][...] = m_new

    def values(block, par):
        off = pl.multiple_of(jnp.maximum(block, 0) * tq, tq)
        for h in heads:
            acc_of[h][...] = al_of[h][par][...] * acc_of[h][...] + _dot(vt_ref[h, :, pl.ds(off, tq)],
                                                                         p_of[h][par][...])

    def step(t, par, diagonal=False):
        values(t - 2, par)
        softmax(1 - par)
        scores(t, par, diagonal)

    def pair(u, carry):
        step(2 * u, 0)
        step(2 * u + 1, 1)
        return carry

    lax.fori_loop(0, lax.shift_right_logical(i, 1), pair, 0)

    def drain(par):
        step(i, par, diagonal=True)
        values(i - 1, 1 - par)
        softmax(par)
        values(i, par)

    odd = (i & 1) == 1

    @pl.when(odd)
    def _():
        step(i - 1, 0)
        drain(1)

    @pl.when(jnp.logical_not(odd))
    def _():
        drain(0)

    for h in heads:
        o = acc_of[h][...] / l_of[h][...]
        o_ref[:, h * dv:(h + 1) * dv] = o.T.astype(o_ref.dtype)


def _attention(q, k, vt, selected_t, batch, heads, dk, dv, name, tq=256):
    t = q.shape[0]
    s = t // batch
    tq = _block(s, tq)
    nq = s // tq
    hp = ATTN_HEADS_PER_STEP
    in_specs = [pl.BlockSpec((tq, hp * dk), lambda b, h, i: (b * nq + i, h)),
                pl.BlockSpec((s, hp * dk), lambda b, h, i: (b, h)),
                pl.BlockSpec((None, hp, dv, s), lambda b, h, i: (b, h, 0, 0))]
    operands = [q, k, vt]
    if selected_t is not None:
        in_specs.append(pl.BlockSpec((None, s, tq), lambda b, h, i: (b, 0, i)))
        operands.append(selected_t)
    return pl.pallas_call(
        functools.partial(_attn_kernel, tq=tq, dk=dk, dv=dv, selected_keys=selected_t is not None),
        grid=(batch, heads // hp, nq),
        in_specs=in_specs,
        out_specs=pl.BlockSpec((tq, hp * dv), lambda b, h, i: (b * nq + i, h)),
        out_shape=jax.ShapeDtypeStruct((t, heads * dv), BF16),
        scratch_shapes=_attn_scratch(tq, dk, dv),
        compiler_params=_params("parallel", "parallel", "parallel"),
        name=name,
    )(*operands)


def _heads_transposed_spec(heads, hd, bm, seq):
    blocks_per_batch = seq // bm
    return pl.BlockSpec((None, heads, hd, bm), lambda i: (i // blocks_per_batch, 0, 0, i % blocks_per_batch))


def _heads_transposed_kernel(x_ref, vt_ref):
    _store_heads_transposed(x_ref[...], vt_ref)


def _heads_transposed(proj, col_block, batch, heads, hd):
    t = proj.shape[0]
    bm = _block(t // batch, 512)
    return pl.pallas_call(
        _heads_transposed_kernel,
        grid=(t // bm,),
        in_specs=[pl.BlockSpec((bm, heads * hd), lambda i: (i, col_block))],
        out_specs=_heads_transposed_spec(heads, hd, bm, t // batch),
        out_shape=jax.ShapeDtypeStruct((batch, heads, hd, t // batch), BF16),
        compiler_params=_params("parallel"),
        name="heads_transposed",
    )(proj)


def _head_norm_rope_kernel(x_ref, g_ref, cos_ref, sin_ref, o_ref, *, heads, premultiplier):
    cos, sin = cos_ref[0], sin_ref[0]
    g = g_ref[...] * premultiplier
    for h in range(heads):
        x = x_ref[:, h * LANES:(h + 1) * LANES]
        y = x * _rms_scale(x, LANES) * g
        o_ref[:, h * LANES:(h + 1) * LANES] = _rope_tile(y, cos, sin, DSA_ROT // 2, LANES).astype(o_ref.dtype)


def _head_norm_rope(proj, col_block, g, cos, sin, premultiplier=1.0):
    t = proj.shape[0]
    width = DSA_HEADS * DSA_HEAD_DIM
    bm = _block(t, 512)
    table = pl.BlockSpec((1, bm, LANES), lambda i: (1, i, 0))
    return pl.pallas_call(
        functools.partial(_head_norm_rope_kernel, heads=DSA_HEADS, premultiplier=premultiplier),
        grid=(t // bm,),
        in_specs=[pl.BlockSpec((bm, width), lambda i: (i, col_block)),
                  pl.BlockSpec((1, LANES), lambda i: (0, 0)), table, table],
        out_specs=pl.BlockSpec((bm, width), lambda i: (i, 0)),
        out_shape=jax.ShapeDtypeStruct((t, width), BF16),
        compiler_params=_params("parallel"),
        name="head_norm_rope",
    )(proj, g.reshape(1, LANES), cos, sin)


def _rope_only_kernel(x_ref, cos_ref, sin_ref, o_ref):
    cos, sin = cos_ref[0], sin_ref[0]
    for c in range(x_ref.shape[1] // LANES):
        x = x_ref[:, c * LANES:(c + 1) * LANES]
        o_ref[:, c * LANES:(c + 1) * LANES] = _rope_tile(x, cos, sin, IDX_ROT // 2, IDX_DIM).astype(o_ref.dtype)


def _rope_only(proj, width, col_block, cos, sin):
    t = proj.shape[0]
    bm = _block(t, 512)
    table = pl.BlockSpec((1, bm, LANES), lambda i: (2, i, 0))
    return pl.pallas_call(
        _rope_only_kernel,
        grid=(t // bm,),
        in_specs=[pl.BlockSpec((bm, width), lambda i: (i, col_block)), table, table],
        out_specs=pl.BlockSpec((bm, width), lambda i: (i, 0)),
        out_shape=jax.ShapeDtypeStruct((t, width), BF16),
        compiler_params=_params("parallel"),
        name="indexer_rope",
    )(proj, cos, sin)


SCORE_COLS = 256
COUNT_ACC_ROWS = 64


def _sortable_key(x):
    bits = lax.bitcast_convert_type(x, jnp.int32)
    return bits ^ ((bits >> 31) & 0x7FFFFFFF)


def _indexer_kernel(qi_ref, kit_ref, w_ref, *rest, tq, tk, topk, seq, w_scale, n_casts):
    cast_src, sel_ref, cast_dst = rest[:n_casts], rest[n_casts], rest[n_casts + 1:2 * n_casts + 1]
    key_ref, wb_ref, qh_ref = rest[2 * n_casts + 1:]
    _cast_in_kernel(cast_src, cast_dst)

    i = pl.program_id(1)
    n_tiles = pl.cdiv((i + 1) * tq, tk)
    row_chunk = _chunk_of(i * tq + lax.broadcasted_iota(jnp.int32, (tq, 1), 0))

    w = w_ref[...] * w_scale
    for h in range(IDX_HEADS):
        wb_ref[h] = jnp.broadcast_to(w[:, h:h + 1], (tq, LANES))
        qh_ref[h] = qi_ref[:, h * IDX_DIM:(h + 1) * IDX_DIM]

    def score_tile(j, carry):
        for part in range(tk // SCORE_COLS):
            off = pl.multiple_of(j * tk + part * SCORE_COLS, SCORE_COLS)
            kt = kit_ref[:, pl.ds(off, SCORE_COLS)]
            acc = [jnp.zeros((tq, LANES), F32) for _ in range(SCORE_COLS // LANES)]
            for h in range(IDX_HEADS):
                logits = _dot(qh_ref[h], kt)
                wh = wb_ref[h]
                for c in range(SCORE_COLS // LANES):
                    acc[c] = acc[c] + jnp.maximum(logits[:, c * LANES:(c + 1) * LANES], 0.0) * wh
            for c in range(SCORE_COLS // LANES):
                col = off + c * LANES + lax.broadcasted_iota(jnp.int32, (1, LANES), 1)
                score = jnp.where(_chunk_of(col) <= row_chunk, acc[c], NEG_INF)
                key_ref[pl.ds(off + c * LANES, LANES), :] = _sortable_key(score.T)
        return carry

    lax.fori_loop(0, n_tiles, score_tile, 0)

    def count(pred):
        rows = min(tk, COUNT_ACC_ROWS)

        def tile(j, acc):
            off = pl.multiple_of(j * tk, tk)
            hit = jnp.where(pred(key_ref[pl.ds(off, tk), :], off), 1.0, 0.0)
            return acc + jnp.sum(hit.reshape(tk // rows, rows, tq), axis=0)
        acc = lax.fori_loop(0, n_tiles, tile, jnp.zeros((rows, tq), F32))
        return jnp.sum(acc, axis=0, keepdims=True)

    def value_bit(b, lo):
        cand = lo + lax.shift_left(jnp.int32(1), 31 - b)
        n_ge = count(lambda kt, _: kt >= cand)
        return jnp.where(n_ge >= topk, cand, lo)

    thr = lax.fori_loop(0, 32, value_bit, jnp.full((1, tq), INT_MIN, jnp.int32))
    n_ge = count(lambda kt, _: kt >= thr)
    surplus = jnp.logical_and(n_ge > topk, thr > NEG_INF_KEY)

    def write_tile(j, carry):
        off = pl.multiple_of(j * tk, tk)
        kt = key_ref[pl.ds(off, tk), :]
        chosen = jnp.logical_and(kt >= thr, kt > NEG_INF_KEY)
        sel_ref[pl.ds(off, tk), :] = jnp.where(chosen, 1.0, 0.0).astype(sel_ref.dtype)
        return carry

    lax.fori_loop(0, n_tiles, write_tile, 0)

    def clear_tile(j, carry):
        off = pl.multiple_of(j * tk, tk)
        sel_ref[pl.ds(off, tk), :] = jnp.zeros((tk, tq), sel_ref.dtype)
        return carry

    lax.fori_loop(n_tiles, seq // tk, clear_tile, 0)

    @pl.when(jnp.max(jnp.where(surplus, 1.0, 0.0)) > 0.0)
    def _():
        n_gt = count(lambda kt, _: kt > thr)
        need = topk - n_gt

        def position(first_key):
            return first_key + lax.broadcasted_iota(jnp.int32, (tk, tq), 0)

        def position_bit(b, base):
            cand = base + lax.shift_left(jnp.int32(1), (seq - 1).bit_length() - 1 - b)
            n_before = count(lambda kt, k0: jnp.logical_and(kt == thr, position(k0) < cand))
            return jnp.where(n_before < need, cand, base)

        last = lax.fori_loop(0, (seq - 1).bit_length(), position_bit, jnp.zeros((1, tq), jnp.int32))

        def rewrite_tile(j, carry):
            off = pl.multiple_of(j * tk, tk)
            kt = key_ref[pl.ds(off, tk), :]
            chosen = jnp.logical_or(kt > thr, jnp.logical_and(kt == thr, position(off) <= last))
            chosen = jnp.logical_and(chosen, kt > NEG_INF_KEY)
            sel_ref[pl.ds(off, tk), :] = jnp.where(chosen, 1.0, 0.0).astype(sel_ref.dtype)
            return carry

        lax.fori_loop(0, n_tiles, rewrite_tile, 0)


def _indexer_select(qi, kit, w, w_col_block, batch, topk, weights_to_cast=(), tq=128, tk=512):
    s = kit.shape[2]
    tq = _block(s, tq)
    tk = _block(s, tk)
    assert tk % SCORE_COLS == 0
    assert tq == LANES, "score tiles are transposed one (LANES, LANES) tile at a time"
    nq = s // tq
    steps = batch * nq
    w_scale = (IDX_DIM ** -0.5) * (IDX_HEADS ** -0.5)

    slabs = _cast_slabs(weights_to_cast)
    slab_spec = lambda a: _cast_slab_spec(a, steps, lambda b, i: b * nq + i)

    outs = pl.pallas_call(
        functools.partial(_indexer_kernel, tq=tq, tk=tk, topk=topk, seq=s, w_scale=w_scale, n_casts=len(slabs)),
        grid=(batch, nq),
        in_specs=[pl.BlockSpec((tq, IDX_HEADS * IDX_DIM), lambda b, i: (b * nq + i, 0)),
                  pl.BlockSpec((None, IDX_DIM, s), lambda b, i: (b, 0, 0)),
                  pl.BlockSpec((tq, LANES), lambda b, i: (b * nq + i, w_col_block))]
                 + [slab_spec(a) for a in slabs],
        out_specs=[pl.BlockSpec((None, s, tq), lambda b, i: (b, 0, i))] + [slab_spec(a) for a in slabs],
        out_shape=[jax.ShapeDtypeStruct((batch, s, s), BF16)]
                  + [jax.ShapeDtypeStruct(a.shape, BF16) for a in slabs],
        scratch_shapes=[pltpu.VMEM((s, tq), jnp.int32), pltpu.VMEM((IDX_HEADS, tq, LANES), F32),
                        pltpu.VMEM((IDX_HEADS, tq, IDX_DIM), BF16)],
        compiler_params=_params("arbitrary", "arbitrary"),
        name="indexer_select",
    )(qi, kit, w, *slabs)
    return outs[0], tuple(o.reshape(a.shape) for o, a in zip(outs[1:], weights_to_cast))


def _memory_attn_kernel(q_ref, kv_ref, gq_ref, gk_ref, o_ref, *, scale):
    hd = MEM_HEAD_DIM
    gq, gk = gq_ref[...], gk_ref[...]
    for h in range(MEM_HEADS):
        q = q_ref[:, h * hd:(h + 1) * hd]
        k = kv_ref[:, h * hd:(h + 1) * hd]
        v = kv_ref[:, (MEM_HEADS + h) * hd:(MEM_HEADS + h + 1) * hd].astype(BF16)
        qn = (q * _rms_scale(q, hd) * gq).astype(BF16)
        kn = (k * _rms_scale(k, hd) * gk).astype(BF16)
        s = _dot_nt(qn, kn) * scale
        e = jnp.exp(s - jnp.max(s, axis=-1, keepdims=True))
        p = e / jnp.sum(e, axis=-1, keepdims=True)
        o_ref[:, h * hd:(h + 1) * hd] = _dot(p.astype(BF16), v).astype(o_ref.dtype)


def _memory_attention(proj, col_block, kv, g_q, g_k, batch, tq=512):
    t = proj.shape[0]
    s = t // batch
    m = kv.shape[0] // batch
    width = MEM_HEADS * MEM_HEAD_DIM
    tq = _block(s, tq)
    nq = s // tq
    g_spec = pl.BlockSpec((1, MEM_HEAD_DIM), lambda b, i: (0, 0))
    return pl.pallas_call(
        functools.partial(_memory_attn_kernel, scale=MEM_HEAD_DIM ** -0.5),
        grid=(batch, nq),
        in_specs=[pl.BlockSpec((tq, width), lambda b, i: (b * nq + i, col_block)),
                  pl.BlockSpec((m, 2 * width), lambda b, i: (b, 0)), g_spec, g_spec],
        out_specs=pl.BlockSpec((tq, width), lambda b, i: (b * nq + i, 0)),
        out_shape=jax.ShapeDtypeStruct((t, width), BF16),
        compiler_params=_params("parallel", "parallel"),
        name="memory_attention",
    )(proj, kv, g_q.reshape(1, -1), g_k.reshape(1, -1))


def _reordered_input_projection(w_in, q_rank, kv_rank):
    dsa = DSA_HEADS * DSA_HEAD_DIM
    assert q_rank == dsa, "the reordered projection layout assumes equal widths for these groups"
    sizes = [("c_q", q_rank), ("c_kv", kv_rank), ("k_r", MLA_ROPE), ("q_d", dsa), ("k_d", dsa), ("v_d", dsa),
             ("q_i", IDX_HEADS * IDX_DIM), ("k_i", IDX_DIM), ("w_i", IDX_HEADS), ("q_m", MEM_HEADS * MEM_HEAD_DIM)]
    src, start = {}, 0
    for name, n in sizes:
        src[name] = (start, n)
        start += n
    assert start == w_in.shape[1]
    take = lambda name: w_in[:, src[name][0]:src[name][0] + src[name][1]]
    pieces = [("c_q", take("c_q")), ("q_d", take("q_d")), ("k_d", take("k_d")), ("v_d", take("v_d")),
              ("q_i", take("q_i")), ("q_m", take("q_m")), ("c_kv", take("c_kv")),
              ("k_r", jnp.concatenate([take("k_r"), take("k_i")], axis=1)),
              ("w_i", _pad_last(take("w_i"), LANES))]
    col, start = {}, 0
    for name, piece in pieces:
        assert start % piece.shape[1] == 0, name
        col[name] = start
        start += piece.shape[1]
    return jnp.concatenate([p for _, p in pieces], axis=1), col


def _hybrid_mixer(x, h, mem, positions, w_in_r, col, g_cq, w_uq, g_ckv, w_ukv, g_q_mla, g_k_mla, g_q_dsa, g_k_dsa,
                  g_mem, w_mem_kv, g_q_mem, g_k_mem, w_o_mla, w_o_dsa, w_o_mem, w_gate, w_out, batch, later_weights):
    t, d = h.shape
    s = t // batch
    dsa = DSA_HEADS * DSA_HEAD_DIM
    proj = _matmul(h, w_in_r, F32, bm=1024, bn=768, name="input_projection")
    cos, sin = _rope_tables(positions)

    q_a, k_a, vt_a = _mla_prep(proj, col, g_cq, w_uq, g_ckv, w_ukv, g_q_mla, g_k_mla, cos, sin, batch)
    y_mla = _attention(q_a, k_a, vt_a, None, batch, MLA_HEADS,
                       MLA_QK_PAD, MLA_V, "chunk_causal_attention")

    q_b = _head_norm_rope(proj, col["q_d"] // dsa, g_q_dsa, cos, sin, _query_premultiplier(DSA_HEAD_DIM))
    k_b = _head_norm_rope(proj, col["k_d"] // dsa, g_k_dsa, cos, sin)
    vt_b = _heads_transposed(proj, col["v_d"] // dsa, batch, DSA_HEADS, DSA_HEAD_DIM)
    qi_width = IDX_HEADS * IDX_DIM
    qi = _rope_only(proj, qi_width, col["q_i"] // qi_width, cos, sin)
    ki = _rope_only(proj, LANES, col["k_r"] // LANES, cos, sin)[:, IDX_DIM:]
    kit = ki.reshape(batch, s, IDX_DIM).transpose(0, 2, 1)
    topk = min(DSA_MAX_TOPK, s // 4)
    own_weights = (w_gate, w_o_mla, w_o_dsa, w_o_mem, w_out)
    selected_t, cast = _indexer_select(qi, kit, proj, col["w_i"] // LANES, batch, topk,
                                       own_weights + tuple(later_weights))
    w_gate, w_o_mla, w_o_dsa, w_o_mem, w_out = cast[:len(own_weights)]
    y_dsa = _attention(q_b, k_b, vt_b, selected_t, batch, DSA_HEADS,
                       DSA_HEAD_DIM, DSA_HEAD_DIM, "masked_attention")

    m_len = mem.shape[1]
    mem_n = _rmsnorm(mem.reshape(batch * m_len, d), g_mem)
    kv_mem = _matmul(mem_n, w_mem_kv.astype(BF16), F32, bm=512, bn=512, name="memory_kv")
    y_mem = _memory_attention(proj, col["q_m"] // (MEM_HEADS * MEM_HEAD_DIM), kv_mem, g_q_mem, g_k_mem, batch)

    gated = _gate_combine(h, (y_mla, y_dsa, y_mem), w_gate, (w_o_mla, w_o_dsa, w_o_mem))
    return (_matmul_residual(gated, w_out, x, 1.0, bm=1024, name="mixer_out"), cast[len(own_weights):])


def kernel(x, mem, positions, g_ff1, w_ff1_gate, w_ff1_up, w_ff1_down, g_mix, w_in, g_cq, w_uq, g_ckv, w_ukv, g_q_mla, g_k_mla, g_q_dsa, g_k_dsa, g_mem, w_mem_kv, g_q_mem, g_k_mem, w_o_mla, w_o_dsa, w_o_mem, w_gate, w_out, g_ff2, w_ff2_gate, w_ff2_up, w_ff2_down):
    batch, seq, d = x.shape
    xt = x.reshape(batch * seq, d)
    for l in range(g_ff1.shape[0]):
        xt = _ffn_half_step(xt, g_ff1[l], w_ff1_gate[l], w_ff1_up[l], w_ff1_down[l])
        w_in_r, col = _reordered_input_projection(w_in[l], w_uq.shape[1], w_ukv.shape[1])
        h = _rmsnorm(xt, g_mix[l])
        xt, ff2 = _hybrid_mixer(xt, h, mem, positions, w_in_r.astype(BF16), col, g_cq[l], w_uq[l], g_ckv[l], w_ukv[l],
                                g_q_mla[l], g_k_mla[l], g_q_dsa[l], g_k_dsa[l], g_mem[l], w_mem_kv[l],
                                g_q_mem[l], g_k_mem[l], w_o_mla[l], w_o_dsa[l], w_o_mem[l], w_gate[l], w_out[l],
                                batch, (w_ff2_gate[l], w_ff2_up[l], w_ff2_down[l]))
        xt = _ffn_half_step(xt, g_ff2[l], *ff2)
    return xt.reshape(batch, seq, d)
```

```python
import functools

import jax
import jax.numpy as jnp
from jax import lax
from jax.experimental import pallas as pl
from jax.experimental.pallas import tpu as pltpu

CHUNK = 64
ROPE_THETA = 500000.0
EPS = 1e-6

MLA_HEADS = 12
MLA_NOPE = 128
MLA_ROPE = 64
MLA_V = 128
MLA_QK = MLA_ROPE + MLA_NOPE
MLA_QK_PAD = 256

DSA_HEADS = 12
DSA_HEAD_DIM = 128
DSA_ROT = DSA_HEAD_DIM // 4
DSA_MAX_TOPK = 256
IDX_HEADS = 32
IDX_DIM = 64
IDX_ROT = IDX_DIM // 4

MEM_HEADS = 4
MEM_HEAD_DIM = 256

LANES = 128
SUBLANES = 8
BF16_SUBLANES = 16
VMEM_LIMIT_BYTES = 56 * 1024 * 1024

F32 = jnp.float32
BF16 = jnp.bfloat16
NEG_INF = float("-inf")
INT_MIN = -2 ** 31
LOG2_E = 1.4426950408889634
NEG_INF_KEY = (0xFF800000 - (1 << 32)) ^ 0x7FFFFFFF


def _params(*semantics):
    return pltpu.CompilerParams(dimension_semantics=semantics, vmem_limit_bytes=VMEM_LIMIT_BYTES)


def _block(dim, preferred):
    if dim <= preferred:
        return dim
    b = preferred - preferred % LANES
    while b >= LANES:
        if dim % b == 0:
            return b
        b -= LANES
    return dim


def _rms_scale(x, width):
    return lax.rsqrt(jnp.sum(x * x, axis=-1, keepdims=True) * (1.0 / width) + EPS)


def _dot(a, b):
    return jnp.dot(a, b, preferred_element_type=F32)


def _dot_nt(a, b):
    return lax.dot_general(a, b, (((1,), (1,)), ((), ())), preferred_element_type=F32)


def _sigmoid(x):
    return 1.0 / (1.0 + jnp.exp(-x))


def _query_premultiplier(head_dim):
    return head_dim ** -0.5 * LOG2_E


def _chunk_of(frame):
    return lax.shift_right_logical(frame, CHUNK.bit_length() - 1)


def _rmsnorm_kernel(x_ref, g_ref, o_ref):
    x = x_ref[...]
    o_ref[...] = (x * _rms_scale(x, x.shape[-1]) * g_ref[...]).astype(o_ref.dtype)


def _rmsnorm(x, g, out_dtype=BF16):
    m, d = x.shape
    bm = _block(m, 512)
    return pl.pallas_call(
        _rmsnorm_kernel,
        grid=(m // bm,),
        in_specs=[pl.BlockSpec((bm, d), lambda i: (i, 0)), pl.BlockSpec((1, d), lambda i: (0, 0))],
        out_specs=pl.BlockSpec((bm, d), lambda i: (i, 0)),
        out_shape=jax.ShapeDtypeStruct((m, d), out_dtype),
        compiler_params=_params("parallel"),
        name="rmsnorm",
    )(x, g.reshape(1, d))


def _matmul_kernel(a_ref, b_ref, o_ref):
    o_ref[...] = _dot(a_ref[...], b_ref[...]).astype(o_ref.dtype)


def _matmul(a, b, out_dtype, bm=1024, bn=512, name="matmul"):
    m, k = a.shape
    n = b.shape[1]
    bm, bn = _block(m, bm), _block(n, bn)
    return pl.pallas_call(
        _matmul_kernel,
        grid=(m // bm, n // bn),
        in_specs=[pl.BlockSpec((bm, k), lambda i, j: (i, 0)), pl.BlockSpec((k, bn), lambda i, j: (0, j))],
        out_specs=pl.BlockSpec((bm, bn), lambda i, j: (i, j)),
        out_shape=jax.ShapeDtypeStruct((m, n), out_dtype),
        compiler_params=_params("parallel", "parallel"),
        name=name,
    )(a, b)


def _cast_slabs(weights):
    return [a.reshape(-1, a.shape[-1]) for a in weights]


def _cast_slab_spec(a, steps, step_of):
    rows = a.shape[0]
    r = next(r for r in range(BF16_SUBLANES, rows + 1, BF16_SUBLANES) if rows % r == 0 and rows // r <= steps)
    last = rows // r - 1
    return pl.BlockSpec((r, a.shape[1]), lambda *g: (jnp.minimum(step_of(*g), last), 0))


def _cast_in_kernel(src_refs, dst_refs):
    for src, dst in zip(src_refs, dst_refs):
        dst[...] = src[...].astype(dst.dtype)


def _swiglu_up_kernel(a_ref, wg_ref, wu_ref, o_ref):
    a = a_ref[...]
    g = _dot(a, wg_ref[...])
    u = _dot(a, wu_ref[...])
    o_ref[...] = (g * _sigmoid(g) * u).astype(o_ref.dtype)


def _swiglu_up(a, wg, wu, bm=1024, bn=512):
    m, k = a.shape
    n = wg.shape[1]
    bm, bn = _block(m, bm), _block(n, bn)
    return pl.pallas_call(
        _swiglu_up_kernel,
        grid=(m // bm, n // bn),
        in_specs=[pl.BlockSpec((bm, k), lambda i, j: (i, 0)),
                  pl.BlockSpec((k, bn), lambda i, j: (0, j)),
                  pl.BlockSpec((k, bn), lambda i, j: (0, j))],
        out_specs=pl.BlockSpec((bm, bn), lambda i, j: (i, j)),
        out_shape=jax.ShapeDtypeStruct((m, n), BF16),
        compiler_params=_params("parallel", "parallel"),
        name="swiglu_up",
    )(a, wg, wu)


def _matmul_residual_kernel(a_ref, b_ref, x_ref, o_ref, *, scale):
    o_ref[...] = x_ref[...] + scale * _dot(a_ref[...], b_ref[...])


def _matmul_residual(a, b, x, scale, bm=512, bn=512, name="matmul_residual"):
    m, k = a.shape
    n = b.shape[1]
    bm, bn = _block(m, bm), _block(n, bn)
    return pl.pallas_call(
        functools.partial(_matmul_residual_kernel, scale=scale),
        grid=(m // bm, n // bn),
        in_specs=[pl.BlockSpec((bm, k), lambda i, j: (i, 0)),
                  pl.BlockSpec((k, bn), lambda i, j: (0, j)),
                  pl.BlockSpec((bm, bn), lambda i, j: (i, j))],
        out_specs=pl.BlockSpec((bm, bn), lambda i, j: (i, j)),
        out_shape=jax.ShapeDtypeStruct((m, n), F32),
        compiler_params=_params("parallel", "parallel"),
        name=name,
    )(a, b, x)


def _ffn_half_step(x, g, w_gate, w_up, w_down):
    h = _rmsnorm(x, g)
    a = _swiglu_up(h, w_gate.astype(BF16), w_up.astype(BF16))
    return _matmul_residual(a, w_down.astype(BF16), x, 0.5, name="ffn_down")


def _gate_combine_kernel(h_ref, ya_ref, yb_ref, yc_ref, wg_ref, wa_ref, wb_ref, wc_ref, o_ref):
    h = h_ref[...]
    out = _sigmoid(_dot(h, wg_ref[0])) * _dot(ya_ref[...], wa_ref[...])
    out = out + _sigmoid(_dot(h, wg_ref[1])) * _dot(yb_ref[...], wb_ref[...])
    out = out + _sigmoid(_dot(h, wg_ref[2])) * _dot(yc_ref[...], wc_ref[...])
    o_ref[...] = out.astype(o_ref.dtype)


def _gate_combine(h, ys, w_gate, w_os, bm=512, bn=512):
    m, d = h.shape
    n = w_gate.shape[-1]
    bm, bn = _block(m, bm), _block(n, bn)
    row = lambda w: pl.BlockSpec((bm, w), lambda i, j: (i, 0))
    col = lambda k: pl.BlockSpec((k, bn), lambda i, j: (0, j))
    return pl.pallas_call(
        _gate_combine_kernel,
        grid=(m // bm, n // bn),
        in_specs=[row(d)] + [row(y.shape[1]) for y in ys]
                 + [pl.BlockSpec((3, d, bn), lambda i, j: (0, 0, j))] + [col(w.shape[0]) for w in w_os],
        out_specs=pl.BlockSpec((bm, bn), lambda i, j: (i, j)),
        out_shape=jax.ShapeDtypeStruct((m, n), BF16),
        compiler_params=_params("parallel", "parallel"),
        name="gate_combine",
    )(h, *ys, w_gate, *w_os)


def _rope_tables_kernel(pos_ref, invf_ref, sgn_ref, cos_ref, sin_ref):
    pos = pos_ref[...].astype(F32)
    for t in range(invf_ref.shape[0]):
        ang = pos * invf_ref[t:t + 1, :]
        cos_ref[t] = jnp.cos(ang)
        sin_ref[t] = jnp.sin(ang) * sgn_ref[t:t + 1, :]


def _rope_lane_tables(rot, period):
    half = rot // 2
    inv_freq = ROPE_THETA ** (-jnp.arange(half, dtype=F32) / half)
    lane = jnp.arange(LANES) % period
    invf = jnp.where(lane < rot, inv_freq[lane % half], 0.0)
    sgn = jnp.where(lane < half, -1.0, jnp.where(lane < rot, 1.0, 0.0))
    return invf.astype(F32), sgn.astype(F32)


def _rope_tables(positions):
    t = positions.size
    specs = [_rope_lane_tables(MLA_ROPE, LANES), _rope_lane_tables(DSA_ROT, LANES),
             _rope_lane_tables(IDX_ROT, IDX_DIM)]
    invf = jnp.stack([s[0] for s in specs])
    sgn = jnp.stack([s[1] for s in specs])
    bm = _block(t, 1024)
    n = invf.shape[0]
    const = pl.BlockSpec((n, LANES), lambda i: (0, 0))
    out = pl.BlockSpec((n, bm, LANES), lambda i: (0, i, 0))
    return pl.pallas_call(
        _rope_tables_kernel,
        grid=(t // bm,),
        in_specs=[pl.BlockSpec((bm, 1), lambda i: (i, 0)), const, const],
        out_specs=[out, out],
        out_shape=[jax.ShapeDtypeStruct((n, t, LANES), F32)] * 2,
        compiler_params=_params("parallel"),
        name="rope_tables",
    )(positions.reshape(t, 1), invf, sgn)


def _rope_tile(x, cos, sin, half, period):
    lane = lax.broadcasted_iota(jnp.int32, x.shape, 1) & (period - 1)
    partner = jnp.where(lane < half, pltpu.roll(x, LANES - half, 1), pltpu.roll(x, half, 1))
    return x * cos + partner * sin


def _mla_q_kernel(cq_ref, gcq_ref, w_ref, gq_ref, cos_ref, sin_ref, o_ref):
    x = cq_ref[...]
    xn = (x * _rms_scale(x, x.shape[-1]) * gcq_ref[...]).astype(BF16)
    q = _dot(xn, w_ref[...])
    cos, sin = cos_ref[0], sin_ref[0]
    g = gq_ref[...] * _query_premultiplier(MLA_QK)
    for h in range(MLA_HEADS):
        lo = h * MLA_QK_PAD
        qh = q[:, lo:lo + MLA_QK_PAD]
        y = qh * _rms_scale(qh, MLA_QK) * g
        o_ref[:, lo:lo + LANES] = _rope_tile(y[:, :LANES], cos, sin, MLA_ROPE // 2, LANES).astype(o_ref.dtype)
        o_ref[:, lo + LANES:lo + MLA_QK_PAD] = y[:, LANES:].astype(o_ref.dtype)


def _store_heads_transposed(x, vt_ref):
    heads, hd, _ = vt_ref.shape
    for h in range(heads):
        vt_ref[h] = x[:, h * hd:(h + 1) * hd].T.astype(vt_ref.dtype)


def _mla_kv_kernel(ckv_ref, kr_ref, gckv_ref, wk_ref, wv_ref, gk_ref, cos_ref, sin_ref, k_ref, vt_ref):
    x = ckv_ref[...]
    xn = (x * _rms_scale(x, x.shape[-1]) * gckv_ref[...]).astype(BF16)
    k_nope = _dot(xn, wk_ref[...])
    _store_heads_transposed(_dot(xn, wv_ref[...]), vt_ref)
    kr = kr_ref[...]
    lane = lax.broadcasted_iota(jnp.int32, kr.shape, 1)
    kr = jnp.where(lane < MLA_ROPE, kr, 0.0)
    cos, sin = cos_ref[0], sin_ref[0]
    g = gk_ref[...]
    for h in range(MLA_HEADS):
        lo = h * MLA_QK_PAD
        ka = k_nope[:, lo:lo + LANES] + kr
        kb = k_nope[:, lo + LANES:lo + MLA_QK_PAD]
        ms = (jnp.sum(ka * ka, axis=-1, keepdims=True) + jnp.sum(kb * kb, axis=-1, keepdims=True)) * (1.0 / MLA_QK)
        r = lax.rsqrt(ms + EPS)
        ya = ka * r * g[:, :LANES]
        yb = kb * r * g[:, LANES:]
        k_ref[:, lo:lo + LANES] = _rope_tile(ya, cos, sin, MLA_ROPE // 2, LANES).astype(k_ref.dtype)
        k_ref[:, lo + LANES:lo + MLA_QK_PAD] = yb.astype(k_ref.dtype)


def _pad_last(a, width):
    return jnp.pad(a, [(0, 0)] * (a.ndim - 1) + [(0, width - a.shape[-1])])


def _mla_prep(proj, col, g_cq, w_uq, g_ckv, w_ukv, g_q, g_k, cos, sin, batch):
    t = proj.shape[0]
    q_rank, kv_rank = w_uq.shape[0], w_ukv.shape[0]
    hp = MLA_HEADS * MLA_QK_PAD
    w_q = _pad_last(w_uq.reshape(q_rank, MLA_HEADS, MLA_QK), MLA_QK_PAD).reshape(q_rank, hp).astype(BF16)
    w_kv = w_ukv.reshape(kv_rank, MLA_HEADS, MLA_NOPE + MLA_V)
    w_k = jnp.pad(w_kv[..., :MLA_NOPE], ((0, 0), (0, 0), (MLA_ROPE, MLA_QK_PAD - MLA_QK)))
    w_k = w_k.reshape(kv_rank, hp).astype(BF16)
    w_v = w_kv[..., MLA_NOPE:].reshape(kv_rank, MLA_HEADS * MLA_V).astype(BF16)
    g_qp = _pad_last(g_q, MLA_QK_PAD).reshape(1, MLA_QK_PAD)
    g_kp = _pad_last(g_k, MLA_QK_PAD).reshape(1, MLA_QK_PAD)

    bm = _block(t, 512)
    rows = lambda w, c: pl.BlockSpec((bm, w), lambda i, c=c: (i, c))
    full = lambda a: pl.BlockSpec(a.shape, lambda i: (0,) * a.ndim)
    table = pl.BlockSpec((1, bm, LANES), lambda i: (0, i, 0))
    g_cq2, g_ckv2 = g_cq.reshape(1, -1), g_ckv.reshape(1, -1)

    q = pl.pallas_call(
        _mla_q_kernel,
        grid=(t // bm,),
        in_specs=[rows(q_rank, col["c_q"] // q_rank), full(g_cq2), full(w_q), full(g_qp), table, table],
        out_specs=pl.BlockSpec((bm, hp), lambda i: (i, 0)),
        out_shape=jax.ShapeDtypeStruct((t, hp), BF16),
        compiler_params=_params("parallel"),
        name="mla_q_prep",
    )(proj, g_cq2, w_q, g_qp, cos, sin)

    k, v = pl.pallas_call(
        _mla_kv_kernel,
        grid=(t // bm,),
        in_specs=[rows(kv_rank, col["c_kv"] // kv_rank), rows(LANES, col["k_r"] // LANES), full(g_ckv2),
                  full(w_k), full(w_v), full(g_kp), table, table],
        out_specs=[pl.BlockSpec((bm, hp), lambda i: (i, 0)),
                   _heads_transposed_spec(MLA_HEADS, MLA_V, bm, t // batch)],
        out_shape=[jax.ShapeDtypeStruct((t, hp), BF16),
                   jax.ShapeDtypeStruct((batch, MLA_HEADS, MLA_V, t // batch), BF16)],
        compiler_params=_params("parallel"),
        name="mla_kv_prep",
    )(proj, proj, g_ckv2, w_k, w_v, g_kp, cos, sin)
    return q, k, v


ATTN_HEADS_PER_STEP = 3


def _attn_scratch(tq, dk, dv):
    per_head = [pltpu.VMEM((1, tq), F32), pltpu.VMEM((1, tq), F32), pltpu.VMEM((dv, tq), F32)]
    per_parity = [pltpu.VMEM((tq, tq), F32), pltpu.VMEM((1, tq), F32),
                  pltpu.VMEM((tq, tq), BF16), pltpu.VMEM((1, tq), F32)]
    return (per_head + 2 * per_parity) * ATTN_HEADS_PER_STEP + [pltpu.VMEM((ATTN_HEADS_PER_STEP, dk, tq), BF16)]


def _attn_kernel(q_ref, k_ref, vt_ref, *rest, tq, dk, dv, selected_keys):
    n_scratch = 11 * ATTN_HEADS_PER_STEP + 1
    scratch = rest[-n_scratch:]
    qt_ref = scratch[-1]
    sel_ref = rest[0] if selected_keys else None
    o_ref = rest[-n_scratch - 1]
    i = pl.program_id(2)
    heads = range(ATTN_HEADS_PER_STEP)
    m_of, l_of, acc_of, s_of, mx_of, p_of, al_of = [], [], [], [], [], [], []
    for h in heads:
        refs = scratch[11 * h:11 * (h + 1)]
        m_of.append(refs[0]); l_of.append(refs[1]); acc_of.append(refs[2])
        s_of.append((refs[3], refs[7])); mx_of.append((refs[4], refs[8]))
        p_of.append((refs[5], refs[9])); al_of.append((refs[6], refs[10]))
    for h in heads:
        qt_ref[h] = q_ref[:, h * dk:(h + 1) * dk].T

    for h in heads:
        m_of[h][...] = jnp.full((1, tq), NEG_INF, F32)
        l_of[h][...] = jnp.zeros((1, tq), F32)
        acc_of[h][...] = jnp.zeros((dv, tq), F32)
        s_of[h][1][...] = jnp.full((tq, tq), NEG_INF, F32)
        mx_of[h][1][...] = jnp.full((1, tq), NEG_INF, F32)
        p_of[h][0][...] = jnp.zeros((tq, tq), BF16)
        al_of[h][0][...] = jnp.zeros((1, tq), F32)

    def scores(block, par, diagonal):
        off = pl.multiple_of(block * tq, tq)
        if selected_keys:
            visible = sel_ref[pl.ds(off, tq), :].astype(F32) > 0.0
        elif diagonal:
            key_chunk = _chunk_of(lax.broadcasted_iota(jnp.int32, (tq, tq), 0))
            query_chunk = _chunk_of(lax.broadcasted_iota(jnp.int32, (tq, tq), 1))
            visible = key_chunk <= query_chunk
        else:
            visible = None
        for h in heads:
            s = _dot(k_ref[pl.ds(off, tq), h * dk:(h + 1) * dk], qt_ref[h])
            if visible is not None:
                s = jnp.where(visible, s, NEG_INF)
            s_of[h][par][...] = s
            mx_of[h][par][...] = jnp.max(s, axis=0, keepdims=True)

    def softmax(par):
        for h in heads:
            m_prev = m_of[h][...]
            m_new = jnp.maximum(m_prev, mx_of[h][par][...])
            m_safe = jnp.where(m_new == NEG_INF, 0.0, m_new)
            alpha = jnp.exp2(m_prev - m_safe)
            p = jnp.exp2(s_of[h][par][...] - m_safe)
            l_of[h][...] = alpha * l_of[h][...] + jnp.sum(p, axis=0, keepdims=True)
            p_of[h][par][...] = p.astype(BF16)
            al_of[h][par][...] = alpha
            m_of[h][...] = m_new

    def values(block, par):
        off = pl.multiple_of(jnp.maximum(block, 0) * tq, tq)
        for h in heads:
            acc_of[h][...] = al_of[h][par][...] * acc_of[h][...] + _dot(vt_ref[h, :, pl.ds(off, tq)],
                                                                         p_of[h][par][...])

    def step(t, par, diagonal=False):
        values(t - 2, par)
        softmax(1 - par)
        scores(t, par, diagonal)

    def pair(u, carry):
        step(2 * u, 0)
        step(2 * u + 1, 1)
        return carry

    lax.fori_loop(0, lax.shift_right_logical(i, 1), pair, 0)

    def drain(par):
        step(i, par, diagonal=True)
        values(i - 1, 1 - par)
        softmax(par)
        values(i, par)

    odd = (i & 1) == 1

    @pl.when(odd)
    def _():
        step(i - 1, 0)
        drain(1)

    @pl.when(jnp.logical_not(odd))
    def _():
        drain(0)

    for h in heads:
        o = acc_of[h][...] / l_of[h][...]
        o_ref[:, h * dv:(h + 1) * dv] = o.T.astype(o_ref.dtype)


def _attention(q, k, vt, selected_t, batch, heads, dk, dv, name, tq=256):
    t = q.shape[0]
    s = t // batch
    tq = _block(s, tq)
    nq = s // tq
    hp = ATTN_HEADS_PER_STEP
    in_specs = [pl.BlockSpec((tq, hp * dk), lambda b, h, i: (b * nq + i, h)),
                pl.BlockSpec((s, hp * dk), lambda b, h, i: (b, h)),
                pl.BlockSpec((None, hp, dv, s), lambda b, h, i: (b, h, 0, 0))]
    operands = [q, k, vt]
    if selected_t is not None:
        in_specs.append(pl.BlockSpec((None, s, tq), lambda b, h, i: (b, 0, i)))
        operands.append(selected_t)
    return pl.pallas_call(
        functools.partial(_attn_kernel, tq=tq, dk=dk, dv=dv, selected_keys=selected_t is not None),
        grid=(batch, heads // hp, nq),
        in_specs=in_specs,
        out_specs=pl.BlockSpec((tq, hp * dv), lambda b, h, i: (b * nq + i, h)),
        out_shape=jax.ShapeDtypeStruct((t, heads * dv), BF16),
        scratch_shapes=_attn_scratch(tq, dk, dv),
        compiler_params=_params("parallel", "parallel", "parallel"),
        name=name,
    )(*operands)


def _heads_transposed_spec(heads, hd, bm, seq):
    blocks_per_batch = seq // bm
    return pl.BlockSpec((None, heads, hd, bm), lambda i: (i // blocks_per_batch, 0, 0, i % blocks_per_batch))


def _heads_transposed_kernel(x_ref, vt_ref):
    _store_heads_transposed(x_ref[...], vt_ref)


def _heads_transposed(proj, col_block, batch, heads, hd):
    t = proj.shape[0]
    bm = _block(t // batch, 512)
    return pl.pallas_call(
        _heads_transposed_kernel,
        grid=(t // bm,),
        in_specs=[pl.BlockSpec((bm, heads * hd), lambda i: (i, col_block))],
        out_specs=_heads_transposed_spec(heads, hd, bm, t // batch),
        out_shape=jax.ShapeDtypeStruct((batch, heads, hd, t // batch), BF16),
        compiler_params=_params("parallel"),
        name="heads_transposed",
    )(proj)


def _head_norm_rope_kernel(x_ref, g_ref, cos_ref, sin_ref, o_ref, *, heads, premultiplier):
    cos, sin = cos_ref[0], sin_ref[0]
    g = g_ref[...] * premultiplier
    for h in range(heads):
        x = x_ref[:, h * LANES:(h + 1) * LANES]
        y = x * _rms_scale(x, LANES) * g
        o_ref[:, h * LANES:(h + 1) * LANES] = _rope_tile(y, cos, sin, DSA_ROT // 2, LANES).astype(o_ref.dtype)


def _head_norm_rope(proj, col_block, g, cos, sin, premultiplier=1.0):
    t = proj.shape[0]
    width = DSA_HEADS * DSA_HEAD_DIM
    bm = _block(t, 512)
    table = pl.BlockSpec((1, bm, LANES), lambda i: (1, i, 0))
    return pl.pallas_call(
        functools.partial(_head_norm_rope_kernel, heads=DSA_HEADS, premultiplier=premultiplier),
        grid=(t // bm,),
        in_specs=[pl.BlockSpec((bm, width), lambda i: (i, col_block)),
                  pl.BlockSpec((1, LANES), lambda i: (0, 0)), table, table],
        out_specs=pl.BlockSpec((bm, width), lambda i: (i, 0)),
        out_shape=jax.ShapeDtypeStruct((t, width), BF16),
        compiler_params=_params("parallel"),
        name="head_norm_rope",
    )(proj, g.reshape(1, LANES), cos, sin)


def _rope_only_kernel(x_ref, cos_ref, sin_ref, o_ref):
    cos, sin = cos_ref[0], sin_ref[0]
    for c in range(x_ref.shape[1] // LANES):
        x = x_ref[:, c * LANES:(c + 1) * LANES]
        o_ref[:, c * LANES:(c + 1) * LANES] = _rope_tile(x, cos, sin, IDX_ROT // 2, IDX_DIM).astype(o_ref.dtype)


def _rope_only(proj, width, col_block, cos, sin):
    t = proj.shape[0]
    bm = _block(t, 512)
    table = pl.BlockSpec((1, bm, LANES), lambda i: (2, i, 0))
    return pl.pallas_call(
        _rope_only_kernel,
        grid=(t // bm,),
        in_specs=[pl.BlockSpec((bm, width), lambda i: (i, col_block)), table, table],
        out_specs=pl.BlockSpec((bm, width), lambda i: (i, 0)),
        out_shape=jax.ShapeDtypeStruct((t, width), BF16),
        compiler_params=_params("parallel"),
        name="indexer_rope",
    )(proj, cos, sin)


SCORE_COLS = 256
HALF_MIN = -2 ** 15
COUNT_ACC_ROWS = 64


def _sortable_key(x):
    bits = lax.bitcast_convert_type(x, jnp.int32)
    return bits ^ ((bits >> 31) & 0x7FFFFFFF)


def _indexer_kernel(qi_ref, kit_ref, w_ref, *rest, tq, tk, topk, seq, w_scale, n_casts):
    cast_src, sel_ref, cast_dst = rest[:n_casts], rest[n_casts], rest[n_casts + 1:2 * n_casts + 1]
    key_ref, wb_ref, qh_ref, half_ref = rest[2 * n_casts + 1:]
    _cast_in_kernel(cast_src, cast_dst)

    i = pl.program_id(1)
    n_tiles = pl.cdiv((i + 1) * tq, tk)
    row_chunk = _chunk_of(i * tq + lax.broadcasted_iota(jnp.int32, (tq, 1), 0))

    w = w_ref[...] * w_scale
    for h in range(IDX_HEADS):
        wb_ref[h] = jnp.broadcast_to(w[:, h:h + 1], (tq, LANES))
        qh_ref[h] = qi_ref[:, h * IDX_DIM:(h + 1) * IDX_DIM]

    def score_tile(j, carry):
        for part in range(tk // SCORE_COLS):
            off = pl.multiple_of(j * tk + part * SCORE_COLS, SCORE_COLS)
            kt = kit_ref[:, pl.ds(off, SCORE_COLS)]
            acc = [jnp.zeros((tq, LANES), F32) for _ in range(SCORE_COLS // LANES)]
            for h in range(IDX_HEADS):
                logits = _dot(qh_ref[h], kt)
                wh = wb_ref[h]
                for c in range(SCORE_COLS // LANES):
                    acc[c] = acc[c] + jnp.maximum(logits[:, c * LANES:(c + 1) * LANES], 0.0) * wh
            for c in range(SCORE_COLS // LANES):
                col = off + c * LANES + lax.broadcasted_iota(jnp.int32, (1, LANES), 1)
                score = jnp.where(_chunk_of(col) <= row_chunk, acc[c], NEG_INF)
                key = _sortable_key(score.T)
                key_ref[pl.ds(off + c * LANES, LANES), :] = key
                half_ref[pl.ds(off + c * LANES, LANES), :] = (key >> 16).astype(jnp.int16)
        return carry

    lax.fori_loop(0, n_tiles, score_tile, 0)

    def count(pred):
        rows = min(tk, COUNT_ACC_ROWS)

        def tile(j, acc):
            off = pl.multiple_of(j * tk, tk)
            hit = jnp.where(pred(key_ref[pl.ds(off, tk), :], off), 1.0, 0.0)
            return acc + jnp.sum(hit.reshape(tk // rows, rows, tq), axis=0)
        acc = lax.fori_loop(0, n_tiles, tile, jnp.zeros((rows, tq), F32))
        return jnp.sum(acc, axis=0, keepdims=True)

    def count_half(pred):
        rows = min(tk, 2 * COUNT_ACC_ROWS)

        def tile(j, acc):
            off = pl.multiple_of(j * tk, tk)
            hit = jnp.where(pred(half_ref[pl.ds(off, tk), :]), jnp.int16(1), jnp.int16(0))
            for g in range(tk // rows):
                acc = acc + hit[g * rows:(g + 1) * rows]
            return acc
        acc = lax.fori_loop(0, n_tiles, tile, jnp.zeros((rows, tq), jnp.int16))
        return jnp.sum(acc.astype(F32), axis=0, keepdims=True)

    def kth_largest_half(k):
        def bit(b, lo):
            cand = lo + lax.shift_left(jnp.int32(1), 15 - b)
            cand_half = cand.astype(jnp.int16)
            return jnp.where(count_half(lambda t: t >= cand_half) >= k, cand, lo)
        return lax.fori_loop(0, 16, bit, jnp.full((1, tq), HALF_MIN, jnp.int32))

    thr_high = kth_largest_half(topk)
    thr_high_half = thr_high.astype(jnp.int16)
    n_above = count_half(lambda t: t > thr_high_half)

    def low_halves(j, carry):
        off = pl.multiple_of(j * tk, tk)
        kt = key_ref[pl.ds(off, tk), :]
        low = (kt & 0xFFFF) + HALF_MIN
        half_ref[pl.ds(off, tk), :] = jnp.where((kt >> 16) == thr_high, low, HALF_MIN).astype(jnp.int16)
        return carry

    lax.fori_loop(0, n_tiles, low_halves, 0)
    thr_low = kth_largest_half(topk - n_above)
    thr = lax.shift_left(thr_high, 16) + (thr_low - HALF_MIN)
    n_ge = count(lambda kt, _: kt >= thr)
    surplus = jnp.logical_and(n_ge > topk, thr > NEG_INF_KEY)

    def write_tile(j, carry):
        off = pl.multiple_of(j * tk, tk)
        kt = key_ref[pl.ds(off, tk), :]
        chosen = jnp.logical_and(kt >= thr, kt > NEG_INF_KEY)
        sel_ref[pl.ds(off, tk), :] = jnp.where(chosen, 1.0, 0.0).astype(sel_ref.dtype)
        return carry

    lax.fori_loop(0, n_tiles, write_tile, 0)

    def clear_tile(j, carry):
        off = pl.multiple_of(j * tk, tk)
        sel_ref[pl.ds(off, tk), :] = jnp.zeros((tk, tq), sel_ref.dtype)
        return carry

    lax.fori_loop(n_tiles, seq // tk, clear_tile, 0)

    @pl.when(jnp.max(jnp.where(surplus, 1.0, 0.0)) > 0.0)
    def _():
        n_gt = count(lambda kt, _: kt > thr)
        need = topk - n_gt

        def position(first_key):
            return first_key + lax.broadcasted_iota(jnp.int32, (tk, tq), 0)

        def position_bit(b, base):
            cand = base + lax.shift_left(jnp.int32(1), (seq - 1).bit_length() - 1 - b)
            n_before = count(lambda kt, k0: jnp.logical_and(kt == thr, position(k0) < cand))
            return jnp.where(n_before < need, cand, base)

        last = lax.fori_loop(0, (seq - 1).bit_length(), position_bit, jnp.zeros((1, tq), jnp.int32))

        def rewrite_tile(j, carry):
            off = pl.multiple_of(j * tk, tk)
            kt = key_ref[pl.ds(off, tk), :]
            chosen = jnp.logical_or(kt > thr, jnp.logical_and(kt == thr, position(off) <= last))
            chosen = jnp.logical_and(chosen, kt > NEG_INF_KEY)
            sel_ref[pl.ds(off, tk), :] = jnp.where(chosen, 1.0, 0.0).astype(sel_ref.dtype)
            return carry

        lax.fori_loop(0, n_tiles, rewrite_tile, 0)


def _indexer_select(qi, kit, w, w_col_block, batch, topk, weights_to_cast=(), tq=128, tk=512):
    s = kit.shape[2]
    tq = _block(s, tq)
    tk = _block(s, tk)
    assert tk % SCORE_COLS == 0
    assert tq == LANES, "score tiles are transposed one (LANES, LANES) tile at a time"
    nq = s // tq
    steps = batch * nq
    w_scale = (IDX_DIM ** -0.5) * (IDX_HEADS ** -0.5)

    slabs = _cast_slabs(weights_to_cast)
    slab_spec = lambda a: _cast_slab_spec(a, steps, lambda b, i: b * nq + i)

    outs = pl.pallas_call(
        functools.partial(_indexer_kernel, tq=tq, tk=tk, topk=topk, seq=s, w_scale=w_scale, n_casts=len(slabs)),
        grid=(batch, nq),
        in_specs=[pl.BlockSpec((tq, IDX_HEADS * IDX_DIM), lambda b, i: (b * nq + i, 0)),
                  pl.BlockSpec((None, IDX_DIM, s), lambda b, i: (b, 0, 0)),
                  pl.BlockSpec((tq, LANES), lambda b, i: (b * nq + i, w_col_block))]
                 + [slab_spec(a) for a in slabs],
        out_specs=[pl.BlockSpec((None, s, tq), lambda b, i: (b, 0, i))] + [slab_spec(a) for a in slabs],
        out_shape=[jax.ShapeDtypeStruct((batch, s, s), BF16)]
                  + [jax.ShapeDtypeStruct(a.shape, BF16) for a in slabs],
        scratch_shapes=[pltpu.VMEM((s, tq), jnp.int32), pltpu.VMEM((IDX_HEADS, tq, LANES), F32),
                        pltpu.VMEM((IDX_HEADS, tq, IDX_DIM), BF16), pltpu.VMEM((s, tq), jnp.int16)],
        compiler_params=_params("arbitrary", "arbitrary"),
        name="indexer_select",
    )(qi, kit, w, *slabs)
    return outs[0], tuple(o.reshape(a.shape) for o, a in zip(outs[1:], weights_to_cast))


def _memory_attn_kernel(q_ref, kv_ref, gq_ref, gk_ref, o_ref, *, scale):
    hd = MEM_HEAD_DIM
    gq, gk = gq_ref[...], gk_ref[...]
    for h in range(MEM_HEADS):
        q = q_ref[:, h * hd:(h + 1) * hd]
        k = kv_ref[:, h * hd:(h + 1) * hd]
        v = kv_ref[:, (MEM_HEADS + h) * hd:(MEM_HEADS + h + 1) * hd].astype(BF16)
        qn = (q * _rms_scale(q, hd) * gq).astype(BF16)
        kn = (k * _rms_scale(k, hd) * gk).astype(BF16)
        s = _dot_nt(qn, kn) * scale
        e = jnp.exp(s - jnp.max(s, axis=-1, keepdims=True))
        p = e / jnp.sum(e, axis=-1, keepdims=True)
        o_ref[:, h * hd:(h + 1) * hd] = _dot(p.astype(BF16), v).astype(o_ref.dtype)


def _memory_attention(proj, col_block, kv, g_q, g_k, batch, tq=512):
    t = proj.shape[0]
    s = t // batch
    m = kv.shape[0] // batch
    width = MEM_HEADS * MEM_HEAD_DIM
    tq = _block(s, tq)
    nq = s // tq
    g_spec = pl.BlockSpec((1, MEM_HEAD_DIM), lambda b, i: (0, 0))
    return pl.pallas_call(
        functools.partial(_memory_attn_kernel, scale=MEM_HEAD_DIM ** -0.5),
        grid=(batch, nq),
        in_specs=[pl.BlockSpec((tq, width), lambda b, i: (b * nq + i, col_block)),
                  pl.BlockSpec((m, 2 * width), lambda b, i: (b, 0)), g_spec, g_spec],
        out_specs=pl.BlockSpec((tq, width), lambda b, i: (b * nq + i, 0)),
        out_shape=jax.ShapeDtypeStruct((t, width), BF16),
        compiler_params=_params("parallel", "parallel"),
        name="memory_attention",
    )(proj, kv, g_q.reshape(1, -1), g_k.reshape(1, -1))


def _reordered_input_projection(w_in, q_rank, kv_rank):
    dsa = DSA_HEADS * DSA_HEAD_DIM
    assert q_rank == dsa, "the reordered projection layout assumes equal widths for these groups"
    sizes = [("c_q", q_rank), ("c_kv", kv_rank), ("k_r", MLA_ROPE), ("q_d", dsa), ("k_d", dsa), ("v_d", dsa),
             ("q_i", IDX_HEADS * IDX_DIM), ("k_i", IDX_DIM), ("w_i", IDX_HEADS), ("q_m", MEM_HEADS * MEM_HEAD_DIM)]
    src, start = {}, 0
    for name, n in sizes:
        src[name] = (start, n)
        start += n
    assert start == w_in.shape[1]
    take = lambda name: w_in[:, src[name][0]:src[name][0] + src[name][1]]
    pieces = [("c_q", take("c_q")), ("q_d", take("q_d")), ("k_d", take("k_d")), ("v_d", take("v_d")),
              ("q_i", take("q_i")), ("q_m", take("q_m")), ("c_kv", take("c_kv")),
              ("k_r", jnp.concatenate([take("k_r"), take("k_i")], axis=1)),
              ("w_i", _pad_last(take("w_i"), LANES))]
    col, start = {}, 0
    for name, piece in pieces:
        assert start % piece.shape[1] == 0, name
        col[name] = start
        start += piece.shape[1]
    return jnp.concatenate([p for _, p in pieces], axis=1), col


def _hybrid_mixer(x, h, mem, positions, w_in_r, col, g_cq, w_uq, g_ckv, w_ukv, g_q_mla, g_k_mla, g_q_dsa, g_k_dsa,
                  g_mem, w_mem_kv, g_q_mem, g_k_mem, w_o_mla, w_o_dsa, w_o_mem, w_gate, w_out, batch, later_weights):
    t, d = h.shape
    s = t // batch
    dsa = DSA_HEADS * DSA_HEAD_DIM
    proj = _matmul(h, w_in_r, F32, bm=1024, bn=768, name="input_projection")
    cos, sin = _rope_tables(positions)

    q_a, k_a, vt_a = _mla_prep(proj, col, g_cq, w_uq, g_ckv, w_ukv, g_q_mla, g_k_mla, cos, sin, batch)
    y_mla = _attention(q_a, k_a, vt_a, None, batch, MLA_HEADS,
                       MLA_QK_PAD, MLA_V, "chunk_causal_attention")

    q_b = _head_norm_rope(proj, col["q_d"] // dsa, g_q_dsa, cos, sin, _query_premultiplier(DSA_HEAD_DIM))
    k_b = _head_norm_rope(proj, col["k_d"] // dsa, g_k_dsa, cos, sin)
    vt_b = _heads_transposed(proj, col["v_d"] // dsa, batch, DSA_HEADS, DSA_HEAD_DIM)
    qi_width = IDX_HEADS * IDX_DIM
    qi = _rope_only(proj, qi_width, col["q_i"] // qi_width, cos, sin)
    ki = _rope_only(proj, LANES, col["k_r"] // LANES, cos, sin)[:, IDX_DIM:]
    kit = ki.reshape(batch, s, IDX_DIM).transpose(0, 2, 1)
    topk = min(DSA_MAX_TOPK, s // 4)
    own_weights = (w_gate, w_o_mla, w_o_dsa, w_o_mem, w_out)
    selected_t, cast = _indexer_select(qi, kit, proj, col["w_i"] // LANES, batch, topk,
                                       own_weights + tuple(later_weights))
    w_gate, w_o_mla, w_o_dsa, w_o_mem, w_out = cast[:len(own_weights)]
    y_dsa = _attention(q_b, k_b, vt_b, selected_t, batch, DSA_HEADS,
                       DSA_HEAD_DIM, DSA_HEAD_DIM, "masked_attention")

    m_len = mem.shape[1]
    mem_n = _rmsnorm(mem.reshape(batch * m_len, d), g_mem)
    kv_mem = _matmul(mem_n, w_mem_kv.astype(BF16), F32, bm=512, bn=512, name="memory_kv")
    y_mem = _memory_attention(proj, col["q_m"] // (MEM_HEADS * MEM_HEAD_DIM), kv_mem, g_q_mem, g_k_mem, batch)

    gated = _gate_combine(h, (y_mla, y_dsa, y_mem), w_gate, (w_o_mla, w_o_dsa, w_o_mem))
    return (_matmul_residual(gated, w_out, x, 1.0, bm=1024, name="mixer_out"), cast[len(own_weights):])


def kernel(x, mem, positions, g_ff1, w_ff1_gate, w_ff1_up, w_ff1_down, g_mix, w_in, g_cq, w_uq, g_ckv, w_ukv, g_q_mla, g_k_mla, g_q_dsa, g_k_dsa, g_mem, w_mem_kv, g_q_mem, g_k_mem, w_o_mla, w_o_dsa, w_o_mem, w_gate, w_out, g_ff2, w_ff2_gate, w_ff2_up, w_ff2_down):
    batch, seq, d = x.shape
    xt = x.reshape(batch * seq, d)
    for l in range(g_ff1.shape[0]):
        xt = _ffn_half_step(xt, g_ff1[l], w_ff1_gate[l], w_ff1_up[l], w_ff1_down[l])
        w_in_r, col = _reordered_input_projection(w_in[l], w_uq.shape[1], w_ukv.shape[1])
        h = _rmsnorm(xt, g_mix[l])
        xt, ff2 = _hybrid_mixer(xt, h, mem, positions, w_in_r.astype(BF16), col, g_cq[l], w_uq[l], g_ckv[l], w_ukv[l],
                                g_q_mla[l], g_k_mla[l], g_q_dsa[l], g_k_dsa[l], g_mem[l], w_mem_kv[l],
                                g_q_mem[l], g_k_mem[l], w_o_mla[l], w_o_dsa[l], w_o_mem[l], w_gate[l], w_out[l],
                                batch, (w_ff2_gate[l], w_ff2_up[l], w_ff2_down[l]))
        xt = _ffn_half_step(xt, g_ff2[l], *ff2)
    return xt.reshape(batch, seq, d)
```
